```python
import jax
import jax.numpy as jnp
from jax import lax
import numpy as np

D_MODEL = 2048
BATCH = 32
SEQ = 256
DEPTH = 2
DEC_BATCH = 2
DEC_SEQ = 2048
PAST_LEN = 512

GRID_W = 64
Q_BLOCK = 128
NORM_EPS = 1e-6
NEG_BIG = -1e30
NA_HEADS = 8
NA_HD = 64
NA_W = NA_HEADS * NA_HD
NA_KR = 8
NA_KW = 16
NA_QB = 16
NA_BAND = NA_QB + NA_KW
MLA_HEADS = 8
MLA_NOPE = 64
MLA_ROPE = 32
MLA_VD = 64
MLA_QK = MLA_NOPE + MLA_ROPE
MLA_W = MLA_HEADS * MLA_VD
MLA_Q_LORA = 512
MLA_KV_LORA = 256
ROPE_BASE = 10000.0
HG_HEADS = 8
HG_DK = 64
HG_DV = 64
HG_KW = HG_HEADS * HG_DK
HG_W = HG_HEADS * HG_DV
HG_CHUNK = 32
N_BRANCH = 3
PROJ_WIDTHS = (NA_W, NA_W, NA_W, NA_W, MLA_Q_LORA, MLA_KV_LORA, MLA_ROPE, MLA_W, HG_KW, HG_KW, HG_KW, HG_W, HG_W, N_BRANCH * D_MODEL)
PROJ_TOTAL = sum(PROJ_WIDTHS)

kernel_name = 'hybrid_na_mla_hgrn2_prefix_dit_step'


def rmsnorm(x, g):
    xf = x.astype(jnp.float32)
    y = xf * lax.rsqrt(jnp.mean(xf * xf, axis=-1, keepdims=True) + NORM_EPS)
    return (y * g.astype(jnp.float32)).astype(x.dtype)


def softmax32(s):
    e = jnp.exp(s - jnp.max(s, axis=-1, keepdims=True))
    return e / jnp.sum(e, axis=-1, keepdims=True)


def split_heads(a, n):
    return a.reshape(a.shape[:-1] + (n, a.shape[-1] // n))


def split_cols(z):
    idx = np.cumsum(PROJ_WIDTHS)[:-1].tolist()
    return jnp.split(z, idx, axis=-1)


def rope_2d(x):
    n = x.shape[1]
    t = jnp.arange(n)
    pos = jnp.stack([t // GRID_W, t % GRID_W]).astype(jnp.float32)
    axis_dim = x.shape[-1] // 2
    inv = ROPE_BASE ** (-jnp.arange(0, axis_dim, 2, dtype=jnp.float32) / axis_dim)
    ang = pos[:, :, None] * inv
    ang = jnp.concatenate([ang, ang], axis=-1)
    ang = jnp.concatenate([ang[0], ang[1]], axis=-1)
    bshape = (1, n) + (1,) * (x.ndim - 3) + (x.shape[-1],)
    cos = jnp.cos(ang).reshape(bshape)
    sin = jnp.sin(ang).reshape(bshape)
    xf = x.astype(jnp.float32)
    xs = xf.reshape(x.shape[:-1] + (2, 2, axis_dim // 2))
    rot = jnp.stack([-xs[..., 1, :], xs[..., 0, :]], axis=-2).reshape(x.shape)
    return (xf * cos + rot * sin).astype(x.dtype)


def attend_blocked(q, k, v, scale):
    B, nq, H, dq = q.shape
    qb = q.reshape(B, nq // Q_BLOCK, Q_BLOCK, H, dq).swapaxes(0, 1)

    def one(qblk):
        s = jnp.einsum('bqhd,bkhd->bhqk', qblk, k, preferred_element_type=jnp.float32) * scale
        p = softmax32(s).astype(v.dtype)
        return jnp.einsum('bhqk,bkhd->bqhd', p, v)

    o = lax.map(one, qb)
    return o.swapaxes(0, 1).reshape(B, nq, H, v.shape[-1])


def neighborhood_attention(q, k, v, k_ctx, v_ctx, rpb):
    B, N, H, Dh = q.shape
    rows = N // GRID_W
    kr = min(NA_KR, rows)
    n_qb = GRID_W // NA_QB
    r = jnp.arange(rows)
    key_rows = jnp.clip(r - NA_KR // 2, 0, rows - kr)[:, None] + jnp.arange(kr)
    qcol = jnp.arange(GRID_W).reshape(n_qb, NA_QB)
    band_cols = jnp.clip(qcol[:, 0] - NA_KW // 2, 0, GRID_W - NA_BAND)[:, None] + jnp.arange(NA_BAND)
    win0 = jnp.clip(qcol - NA_KW // 2, 0, GRID_W - NA_KW)
    bc = band_cols[:, None, :]
    col_ok = (bc >= win0[..., None]) & (bc < win0[..., None] + NA_KW)
    rel_r = key_rows - r[:, None] + NA_KR - 1
    rel_c = jnp.clip(bc - qcol[:, :, None], 1 - NA_KW, NA_KW - 1) + NA_KW - 1
    bias = rpb.astype(jnp.float32)[:, rel_r][:, :, :, rel_c]
    bias = bias.transpose(1, 3, 0, 4, 2, 5)
    bias = jnp.where(col_ok[None, :, None, :, None, :], bias, NEG_BIG)
    qg = q.reshape(B, rows, n_qb, NA_QB, H, Dh)
    kg = k.reshape(B, rows, GRID_W, H, Dh)[:, key_rows][:, :, :, band_cols]
    vg = v.reshape(B, rows, GRID_W, H, Dh)[:, key_rows][:, :, :, band_cols]
    scale = Dh ** -0.5
    s_loc = jnp.einsum('brpihd,brkpjhd->brphikj', qg, kg, preferred_element_type=jnp.float32) * scale + bias
    s_ctx = jnp.einsum('brpihd,bchd->brphic', qg, k_ctx, preferred_element_type=jnp.float32) * scale
    n_loc = kr * NA_BAND
    s = jnp.concatenate([s_loc.reshape(s_loc.shape[:5] + (n_loc,)), s_ctx], axis=-1)
    p = softmax32(s).astype(v.dtype)
    p_loc = p[..., :n_loc].reshape(s_loc.shape)
    o = (jnp.einsum('brphikj,brkpjhd->brpihd', p_loc, vg)
         + jnp.einsum('brphic,bchd->brpihd', p[..., n_loc:], v_ctx))
    return o.reshape(B, N, H * Dh)


def mla_kv(ckv_n, k_rope, w_ukv):
    B, N, _ = ckv_n.shape
    kv = (ckv_n @ w_ukv).reshape(B, N, MLA_HEADS, MLA_NOPE + MLA_VD)
    kr = jnp.broadcast_to(k_rope[:, :, None, :], (B, N, MLA_HEADS, MLA_ROPE))
    return jnp.concatenate([kv[..., :MLA_NOPE], kr], axis=-1), kv[..., MLA_NOPE:]


def lower_bounds(p):
    s = jax.nn.softmax(p.astype(jnp.float32), axis=0)
    return jnp.cumsum(s, axis=0) - s[0]


def hgrn_gates(z, lb):
    zf = z.astype(jnp.float32)
    pos = lb > 0
    log_lb = jnp.where(pos, jnp.log(jnp.where(pos, lb, 1.0)), NEG_BIG)
    logf = jnp.logaddexp(log_lb, jnp.log1p(-lb) + jax.nn.log_sigmoid(zf))
    return split_heads(logf, HG_HEADS), split_heads(-jnp.expm1(logf), HG_HEADS)


def hgrn_chunk_scan(q, k, v, logf, s0):
    B, N, H, _ = q.shape
    nc = N // HG_CHUNK

    def chunks(a):
        return a.reshape(B, nc, HG_CHUNK, H, a.shape[-1]).transpose(1, 0, 3, 2, 4)

    causal = jnp.tril(jnp.ones((HG_CHUNK, HG_CHUNK), dtype=bool))[:, :, None]

    def step(S, xs):
        qc, kc, vc, gc = xs
        b = jnp.cumsum(gc, axis=2)
        diff = b[:, :, :, None, :] - b[:, :, None, :, :]
        decay = jnp.where(causal, jnp.exp(jnp.minimum(diff, 0.0)), 0.0)
        attn = jnp.einsum('bhtk,bhsk,bhtsk->bhts', qc, kc, decay)
        o = jnp.einsum('bhts,bhsv->bhtv', attn, vc) + jnp.einsum('bhtk,bhkv->bhtv', qc * jnp.exp(b), S)
        b_end = b[:, :, -1:, :]
        S = jnp.exp(b_end[:, :, 0, :, None]) * S + jnp.einsum('bhsk,bhsv->bhkv', kc * jnp.exp(b_end - b), vc)
        return S, o

    s_fin, o = lax.scan(step, s0, (chunks(q), chunks(k), chunks(v), chunks(logf)))
    return o.transpose(1, 0, 3, 2, 4).reshape(B, N, H, v.shape[-1]), s_fin


def mixer(h, lp, ctx):
    B, N, _ = h.shape
    (na_q, na_k, na_v, na_g, mla_cq, mla_ckv, mla_kr, mla_g,
     hg_q, hg_ff, hg_fb, hg_i, hg_g, mg) = split_cols(h @ lp['w_in'])
    qa, ka, va = split_heads(na_q, NA_HEADS), split_heads(na_k, NA_HEADS), split_heads(na_v, NA_HEADS)
    if ctx is None:
        o_na = attend_blocked(qa, ka, va, NA_HD ** -0.5).reshape(B, N, NA_W)
    else:
        o_na = neighborhood_attention(qa, ka, va, ctx[0], ctx[1], lp['na_rpb'])
    ckv_n = rmsnorm(mla_ckv, lp['g_mla_kv'])
    qb = split_heads(rmsnorm(mla_cq, lp['g_mla_q']) @ lp['w_mla_uq'], MLA_HEADS)
    if ctx is None:
        kb, vb = mla_kv(ckv_n, mla_kr, lp['w_mla_ukv'])
    else:
        qb = jnp.concatenate([qb[..., :MLA_NOPE], rope_2d(qb[..., MLA_NOPE:])], axis=-1)
        kb, vb = mla_kv(jnp.concatenate([ctx[2], ckv_n], axis=1),
                        jnp.concatenate([ctx[3], rope_2d(mla_kr)], axis=1), lp['w_mla_ukv'])
    o_mla = attend_blocked(qb, kb, vb, MLA_QK ** -0.5).reshape(B, N, MLA_W)
    qh = split_heads(jax.nn.silu(hg_q.astype(jnp.float32)), HG_HEADS)
    vh = split_heads(hg_i.astype(jnp.float32), HG_HEADS)
    lf_f, k_f = hgrn_gates(hg_ff, lp['lb_fwd'])
    lf_b, k_b = hgrn_gates(hg_fb, lp['lb_bwd'])
    if ctx is None:
        s0_f = jnp.zeros((B, HG_HEADS, HG_DK, HG_DV), jnp.float32)
        s0_b = s0_f
    else:
        s0_f, s0_b = ctx[4].astype(jnp.float32), ctx[5].astype(jnp.float32)
    o_f, s_f = hgrn_chunk_scan(qh, k_f, vh, lf_f, s0_f)
    o_b, s_b = hgrn_chunk_scan(jnp.flip(qh, 1), jnp.flip(k_b, 1), jnp.flip(vh, 1), jnp.flip(lf_b, 1), s0_b)
    o_hg = rmsnorm(o_f + jnp.flip(o_b, 1), lp['g_hg_out'].reshape(HG_HEADS, HG_DV))
    o_hg = o_hg.reshape(B, N, HG_W).astype(h.dtype)
    mg_na, mg_mla, mg_hg = jnp.split(mg, N_BRANCH, axis=-1)
    merged = (jax.nn.sigmoid(mg_na) * ((o_na * jax.nn.silu(na_g)) @ lp['w_br_na'])
              + jax.nn.sigmoid(mg_mla) * ((o_mla * jax.nn.silu(mla_g)) @ lp['w_br_mla'])
              + jax.nn.sigmoid(mg_hg) * ((o_hg * jax.nn.silu(hg_g)) @ lp['w_br_hg']))
    out = merged @ lp['w_out']
    new_ctx = (ka, va, ckv_n, mla_kr, s_f, s_b) if ctx is None else None
    return out, new_ctx


def layer(x, mod, lp, ctx):
    shift, scale, gate = jnp.split(mod, 3, axis=-1)
    h = rmsnorm(x, lp['g_pre']) * (1 + scale) + shift
    out, new_ctx = mixer(h, lp, ctx)
    return x + gate * rmsnorm(out, lp['g_post']), new_ctx


def setup_inputs(seed: int = 0) -> dict:
    key = jax.random.key(seed)
    ks = jax.random.split(key, 32)

    def nrm(k, shape, s=1.0):
        return jax.random.normal(k, shape, jnp.float32) * s

    def gain(k, shape):
        return 1.0 + nrm(k, shape, 0.02)

    return {
        'x_prompt': nrm(ks[0], (BATCH, SEQ, D_MODEL)),
        'x_sample': nrm(ks[1], (DEC_BATCH, DEC_SEQ, D_MODEL)),
        'cache_na_k': nrm(ks[2], (DEC_BATCH, DEPTH, PAST_LEN, NA_HEADS, NA_HD)),
        'cache_na_v': nrm(ks[3], (DEC_BATCH, DEPTH, PAST_LEN, NA_HEADS, NA_HD)),
        'cache_mla_ckv': nrm(ks[4], (DEC_BATCH, DEPTH, PAST_LEN, MLA_KV_LORA)),
        'cache_mla_krope': nrm(ks[5], (DEC_BATCH, DEPTH, PAST_LEN, MLA_ROPE)),
        'state_hgrn_fwd': nrm(ks[6], (DEC_BATCH, DEPTH, HG_HEADS, HG_DK, HG_DV), 0.5),
        'state_hgrn_bwd': nrm(ks[7], (DEC_BATCH, DEPTH, HG_HEADS, HG_DK, HG_DV), 0.5),
        'c': nrm(ks[8], (DEC_BATCH, D_MODEL)),
        'c_ctx': nrm(ks[9], (D_MODEL,)),
        'w_ada': nrm(ks[10], (DEPTH, D_MODEL, 3 * D_MODEL), D_MODEL ** -0.5),
        'b_ada': nrm(ks[11], (DEPTH, 3 * D_MODEL), 0.02),
        'g_pre': gain(ks[12], (DEPTH, D_MODEL)),
        'g_post': gain(ks[13], (DEPTH, D_MODEL)),
        'w_in': nrm(ks[14], (DEPTH, D_MODEL, PROJ_TOTAL), D_MODEL ** -0.5),
        'na_rpb': nrm(ks[15], (DEPTH, NA_HEADS, 2 * NA_KR - 1, 2 * NA_KW - 1), 0.1),
        'g_mla_q': gain(ks[16], (DEPTH, MLA_Q_LORA)),
        'w_mla_uq': nrm(ks[17], (DEPTH, MLA_Q_LORA, MLA_HEADS * MLA_QK), MLA_Q_LORA ** -0.5),
        'g_mla_kv': gain(ks[18], (DEPTH, MLA_KV_LORA)),
        'w_mla_ukv': nrm(ks[19], (DEPTH, MLA_KV_LORA, MLA_HEADS * (MLA_NOPE + MLA_VD)), MLA_KV_LORA ** -0.5),
        'hg_lb_fwd': nrm(ks[20], (DEPTH, HG_KW)),
        'hg_lb_bwd': nrm(ks[21], (DEPTH, HG_KW)),
        'g_hg_out': gain(ks[22], (DEPTH, HG_W)),
        'w_br_na': nrm(ks[23], (DEPTH, NA_W, D_MODEL), NA_W ** -0.5),
        'w_br_mla': nrm(ks[24], (DEPTH, MLA_W, D_MODEL), MLA_W ** -0.5),
        'w_br_hg': nrm(ks[25], (DEPTH, HG_W, D_MODEL), HG_W ** -0.5),
        'w_out': nrm(ks[26], (DEPTH, D_MODEL, D_MODEL), D_MODEL ** -0.5),
    }


def reference(x_prompt, x_sample, cache_na_k, cache_na_v, cache_mla_ckv, cache_mla_krope,
              state_hgrn_fwd, state_hgrn_bwd, c, c_ctx, w_ada, b_ada, g_pre, g_post, w_in, na_rpb,
              g_mla_q, w_mla_uq, g_mla_kv, w_mla_ukv, hg_lb_fwd, hg_lb_bwd, g_hg_out,
              w_br_na, w_br_mla, w_br_hg, w_out):
    lb_f_all = lower_bounds(hg_lb_fwd)
    lb_b_all = lower_bounds(hg_lb_bwd)
    y_prompt, y_sample = x_prompt, x_sample
    ctx_out = []
    for l in range(DEPTH):
        lp = {'g_pre': g_pre[l], 'g_post': g_post[l], 'w_in': w_in[l], 'na_rpb': na_rpb[l],
              'g_mla_q': g_mla_q[l], 'w_mla_uq': w_mla_uq[l], 'g_mla_kv': g_mla_kv[l], 'w_mla_ukv': w_mla_ukv[l],
              'lb_fwd': lb_f_all[l], 'lb_bwd': lb_b_all[l], 'g_hg_out': g_hg_out[l],
              'w_br_na': w_br_na[l], 'w_br_mla': w_br_mla[l], 'w_br_hg': w_br_hg[l], 'w_out': w_out[l]}
        mod_ctx = (jax.nn.silu(c_ctx) @ w_ada[l] + b_ada[l])[None, None, :]
        mod_lat = (jax.nn.silu(c) @ w_ada[l] + b_ada[l])[:, None, :]
        y_prompt, ctx_l = layer(y_prompt, mod_ctx, lp, None)
        ctx_out.append(ctx_l)
        cached = (cache_na_k[:, l], cache_na_v[:, l], cache_mla_ckv[:, l], cache_mla_krope[:, l],
                  state_hgrn_fwd[:, l], state_hgrn_bwd[:, l])
        y_sample, _ = layer(y_sample, mod_lat, lp, cached)
    new_na_k = jnp.stack([t[0] for t in ctx_out], axis=1)
    new_na_v = jnp.stack([t[1] for t in ctx_out], axis=1)
    new_mla_ckv = jnp.stack([t[2] for t in ctx_out], axis=1)
    new_mla_krope = jnp.stack([t[3] for t in ctx_out], axis=1)
    new_state_fwd = jnp.stack([t[4] for t in ctx_out], axis=1)
    new_state_bwd = jnp.stack([t[5] for t in ctx_out], axis=1)
    return (y_prompt, y_sample, new_na_k, new_na_v, new_mla_ckv, new_mla_krope, new_state_fwd, new_state_bwd)
```

```python
import functools

import numpy as np
import jax
import jax.numpy as jnp
from jax import lax
from jax.experimental import pallas as pl
from jax.experimental.pallas import tpu as pltpu

D_MODEL = 2048
DEPTH = 2
GRID_W = 64
NORM_EPS = 1e-6
NEG_BIG = -1e30
NA_HEADS = 8
NA_HD = 64
NA_W = NA_HEADS * NA_HD
NA_KR = 8
NA_KW = 16
MLA_HEADS = 8
MLA_NOPE = 64
MLA_ROPE = 32
MLA_VD = 64
MLA_QK = MLA_NOPE + MLA_ROPE
MLA_W = MLA_HEADS * MLA_VD
MLA_Q_LORA = 512
MLA_KV_LORA = 256
ROPE_BASE = 10000.0
HG_HEADS = 8
HG_DK = 64
HG_DV = 64
HG_KW = HG_HEADS * HG_DK
HG_W = HG_HEADS * HG_DV
N_BRANCH = 3
PROJ_WIDTHS = (NA_W, NA_W, NA_W, NA_W, MLA_Q_LORA, MLA_KV_LORA, MLA_ROPE, MLA_W,
               HG_KW, HG_KW, HG_KW, HG_W, HG_W, N_BRANCH * D_MODEL)

F32 = jnp.float32
BF16 = jnp.bfloat16
HI = lax.Precision.HIGHEST

LANE = 128
ROW_BLK = 256
HG_CHUNK = 32
HG_SAFE_DECAY = 60.0
HG_EXP_CLAMP = 80.0
VMEM_LIMIT = 56 * 1024 * 1024

PACKED_W = 12288
C_MG = 0
C_NA_Q, C_NA_K, C_NA_V, C_NA_G = 6144, 6656, 7168, 7680
C_MLA_CQ, C_MLA_G = 8192, 8704
C_HG_Q, C_HG_FF, C_HG_FB, C_HG_I, C_HG_G = 9216, 9728, 10240, 10752, 11264
C_MLA_CKV, C_MLA_KR, C_MLA_KRP = 11776, 12032, 12160


def _nt(a, b, precision=None):
    return lax.dot_general(a, b, (((1,), (1,)), ((), ())), preferred_element_type=F32, precision=precision)


def _tn(a, b, precision=None):
    return lax.dot_general(a, b, (((0,), (0,)), ((), ())), preferred_element_type=F32, precision=precision)


def _mm(a, b, precision=None):
    return jnp.dot(a, b, preferred_element_type=F32, precision=precision)


def _rms(x, g):
    return x * lax.rsqrt(jnp.mean(x * x, axis=-1, keepdims=True) + NORM_EPS) * g


def _silu(x):
    return x * jax.nn.sigmoid(x)


def _params(sem):
    return pltpu.CompilerParams(dimension_semantics=sem, vmem_limit_bytes=VMEM_LIMIT)


def _const_spec(shape):
    nd = len(shape)
    return pl.BlockSpec(shape, lambda *_: (0,) * nd)


def _rope_perm():
    i = np.arange(MLA_ROPE)
    a, j, m = i // 16, (i % 16) // 8, i % 8
    src = a * 16 + (1 - j) * 8 + m
    sign = np.where(j == 0, -1.0, 1.0).astype(np.float32)
    return src, sign


def _pack_w_in(w):
    idx = np.cumsum(PROJ_WIDTHS)[:-1].tolist()
    (na_q, na_k, na_v, na_g, cq, ckv, kr, mla_g, hq, hff, hfb, hi, hg, mg) = jnp.split(w, idx, axis=-1)
    src, sign = _rope_perm()
    krp = kr[:, src] * sign
    pad = jnp.zeros((w.shape[0], LANE - MLA_ROPE), w.dtype)
    return jnp.concatenate([mg, na_q, na_k, na_v, na_g, cq, mla_g, hq, hff, hfb, hi, hg, ckv, kr, pad, krp, pad],
                           axis=-1).astype(BF16)


def _pack_w_uq(w):
    w3 = w.reshape(MLA_Q_LORA, MLA_HEADS, MLA_QK)
    nope, rope = w3[..., :MLA_NOPE], w3[..., MLA_NOPE:]
    src, sign = _rope_perm()
    ropep = rope[..., src] * sign

    def padh(a):
        z = jnp.zeros(a.shape[:-1] + (LANE - a.shape[-1],), a.dtype)
        return jnp.concatenate([a, z], axis=-1).reshape(MLA_Q_LORA, MLA_HEADS * LANE).astype(BF16)

    return padh(nope), padh(rope), padh(ropep)


def _rope_tables(n):
    t = np.arange(n)
    pos = np.stack([t // GRID_W, t % GRID_W]).astype(np.float32)
    axis_dim = MLA_ROPE // 2
    inv = ROPE_BASE ** (-jnp.arange(0, axis_dim, 2, dtype=F32) / axis_dim)
    ang = jnp.asarray(pos)[:, :, None] * inv
    ang = jnp.concatenate([ang, ang], axis=-1)
    ang = jnp.concatenate([ang[0], ang[1]], axis=-1)
    pad = jnp.zeros((n, LANE - MLA_ROPE), F32)
    return jnp.concatenate([jnp.cos(ang), pad], -1), jnp.concatenate([jnp.sin(ang), pad], -1)


def _na_bias_table(rpb):
    rows = 32
    out = []
    for start, r0 in ((0, 0), (4, 8), (20, 28)):
        r = r0 + np.arange(4)
        kr0 = np.clip(r - NA_KR // 2, 0, rows - NA_KR)
        kabs = start + np.arange(12)
        row_ok = (kabs[None, :] >= kr0[:, None]) & (kabs[None, :] < kr0[:, None] + NA_KR)
        rel_r = np.clip(kabs[None, :] - r[:, None] + NA_KR - 1, 0, 2 * NA_KR - 2)
        c = np.arange(GRID_W)
        win0 = np.clip(c - NA_KW // 2, 0, GRID_W - NA_KW)
        kc = np.arange(GRID_W)
        col_ok = (kc[None, :] >= win0[:, None]) & (kc[None, :] < win0[:, None] + NA_KW)
        rel_c = np.clip(kc[None, :] - c[:, None], 1 - NA_KW, NA_KW - 1) + NA_KW - 1
        b = rpb.astype(F32)[:, rel_r][:, :, :, rel_c]
        ok = row_ok[:, :, None, None] & col_ok[None, None, :, :]
        b = jnp.where(jnp.asarray(ok)[None], b, NEG_BIG)
        b = b.transpose(0, 1, 3, 2, 4).reshape(NA_HEADS, 4 * GRID_W, 12 * GRID_W)
        out.append(b)
    return jnp.stack(out)


def _lower_bounds(p):
    s = jax.nn.softmax(p.astype(F32), axis=0)
    return jnp.cumsum(s, axis=0) - s[0]


def _lb_logs(lb):
    pos = lb > 0
    log_lb = jnp.where(pos, jnp.log(jnp.where(pos, lb, 1.0)), NEG_BIG)
    return log_lb, jnp.log1p(-lb)


def _state_to_bd(s):
    b = s.shape[0]
    st = s.astype(F32).transpose(0, 1, 3, 2).reshape(b, 2, 4, HG_DV, HG_DK)
    eye = jnp.eye(4, dtype=F32)
    bd = st[:, :, :, :, None, :] * eye[None, None, :, None, :, None]
    return bd.reshape(b, 2, 4 * HG_DV, 4 * HG_DK)


def _bd_to_state(bd):
    b = bd.shape[0]
    x = bd.reshape(b, 2, 4, HG_DV, 4, HG_DK)
    x = jnp.stack([x[:, :, h, :, h, :] for h in range(4)], axis=2)
    return x.transpose(0, 1, 2, 4, 3).reshape(b, HG_HEADS, HG_DK, HG_DV)


def _mod_kernel(c_ref, w_ref, b_ref, o_ref):
    s = _silu(c_ref[...])
    o_ref[0] = _mm(s, w_ref[0], HI) + b_ref[0]


def _modulation(cvec, w_ada, b_ada):
    tn = 1024
    n3 = 3 * D_MODEL
    return pl.pallas_call(
        _mod_kernel,
        out_shape=jax.ShapeDtypeStruct((DEPTH, 8, n3), F32),
        grid=(DEPTH, n3 // tn),
        in_specs=[pl.BlockSpec((8, D_MODEL), lambda l, j: (0, 0)),
                  pl.BlockSpec((1, D_MODEL, tn), lambda l, j: (l, 0, j)),
                  pl.BlockSpec((1, 1, tn), lambda l, j: (l, 0, j))],
        out_specs=pl.BlockSpec((1, 8, tn), lambda l, j: (l, 0, j)),
        compiler_params=_params(("arbitrary", "arbitrary")),
        name="adaln_mod",
    )(cvec, w_ada, b_ada.reshape(DEPTH, 1, n3))


def _in_kernel(x_ref, mod_ref, g_ref, w_ref, z_ref, h_scr):
    @pl.when(pl.program_id(1) == 0)
    def _():
        y = _rms(x_ref[...], g_ref[...])
        h = y * (1.0 + mod_ref[0, 1:2, :]) + mod_ref[0, 0:1, :]
        h_scr[...] = h.astype(BF16)

    z_ref[...] = _mm(h_scr[...], w_ref[...])


def _in_proj(x, mod3, g_pre, w_packed, rows_per_mod, mod_base):
    m = x.shape[0]
    tm, tn = 512, 1024
    tiles_per_mod = rows_per_mod // tm
    return pl.pallas_call(
        _in_kernel,
        out_shape=jax.ShapeDtypeStruct((m, PACKED_W), F32),
        grid=(m // tm, PACKED_W // tn),
        in_specs=[pl.BlockSpec((tm, D_MODEL), lambda i, j: (i, 0)),
                  pl.BlockSpec((1, 3, D_MODEL), lambda i, j: (mod_base + i // tiles_per_mod, 0, 0)),
                  pl.BlockSpec((1, D_MODEL), lambda i, j: (0, 0)),
                  pl.BlockSpec((D_MODEL, tn), lambda i, j: (0, j))],
        out_specs=pl.BlockSpec((tm, tn), lambda i, j: (i, j)),
        scratch_shapes=[pltpu.VMEM((tm, D_MODEL), BF16)],
        compiler_params=_params(("arbitrary", "arbitrary")),
        name="in_proj",
    )(x, mod3, g_pre, w_packed)


def _softmax_pv(s_list, v_list):
    m = s_list[0].max(axis=-1, keepdims=True)
    for s in s_list[1:]:
        m = jnp.maximum(m, s.max(axis=-1, keepdims=True))
    acc, den = None, None
    for s, v in zip(s_list, v_list):
        e = jnp.exp(s - m)
        d = e.sum(axis=-1, keepdims=True)
        o = _mm(e.astype(BF16), v)
        acc = o if acc is None else acc + o
        den = d if den is None else den + d
    return acc, den


def _mla_heads(qn, qr_fn, kv, krp, o_ref):
    scale = MLA_QK ** -0.5
    outs = []
    for h in range(MLA_HEADS):
        sl = slice(h * LANE, (h + 1) * LANE)
        kvh = kv[:, sl]
        s = (_nt(qn[:, sl], kvh) + _nt(qr_fn(sl), krp)) * scale
        acc, den = _softmax_pv([s], [kvh])
        outs.append(acc[:, MLA_NOPE:] / den)
    o_ref[...] = jnp.concatenate(outs, axis=-1)


def _prompt_attn_kernel(qkv_ref, cq_ref, ckv_ref, kr_ref, gq_ref, gkv_ref, wn_ref, wr_ref, wukv_ref,
                        ona_ref, omla_ref, ckvn_ref):
    qkv = qkv_ref[...]
    scale = NA_HD ** -0.5
    outs = []
    for h in range(NA_HEADS):
        qh = qkv[:, h * NA_HD:(h + 1) * NA_HD].astype(BF16)
        kh = qkv[:, NA_W + h * NA_HD:NA_W + (h + 1) * NA_HD].astype(BF16)
        vh = qkv[:, 2 * NA_W + h * NA_HD:2 * NA_W + (h + 1) * NA_HD].astype(BF16)
        acc, den = _softmax_pv([_nt(qh, kh) * scale], [vh])
        outs.append(acc / den)
    ona_ref[...] = jnp.concatenate(outs, axis=-1)

    cqn = _rms(cq_ref[...], gq_ref[...]).astype(BF16)
    qn = _mm(cqn, wn_ref[...]).astype(BF16)
    qr = _mm(cqn, wr_ref[...]).astype(BF16)
    ckvn = _rms(ckv_ref[...], gkv_ref[...])
    ckvn_ref[...] = ckvn
    kv = _mm(ckvn.astype(BF16), wukv_ref[...]).astype(BF16)
    krp = kr_ref[...].astype(BF16)
    _mla_heads(qn, lambda sl: qr[:, sl], kv, krp, omla_ref)


def _prompt_attn(z, n_batch, g_q, g_kv, wn, wr, w_ukv):
    m = z.shape[0]
    t = ROW_BLK
    hw = MLA_HEADS * LANE
    return pl.pallas_call(
        _prompt_attn_kernel,
        out_shape=(jax.ShapeDtypeStruct((m, NA_W), F32), jax.ShapeDtypeStruct((m, MLA_W), F32),
                   jax.ShapeDtypeStruct((m, MLA_KV_LORA), F32)),
        grid=(n_batch,),
        in_specs=[pl.BlockSpec((t, 3 * NA_W), lambda b: (b, C_NA_Q // (3 * NA_W))),
                  pl.BlockSpec((t, MLA_Q_LORA), lambda b: (b, C_MLA_CQ // MLA_Q_LORA)),
                  pl.BlockSpec((t, MLA_KV_LORA), lambda b: (b, C_MLA_CKV // MLA_KV_LORA)),
                  pl.BlockSpec((t, LANE), lambda b: (b, C_MLA_KR // LANE)),
                  _const_spec((1, MLA_Q_LORA)), _const_spec((1, MLA_KV_LORA)),
                  _const_spec((MLA_Q_LORA, hw)), _const_spec((MLA_Q_LORA, hw)), _const_spec((MLA_KV_LORA, hw))],
        out_specs=(pl.BlockSpec((t, NA_W), lambda b: (b, 0)), pl.BlockSpec((t, MLA_W), lambda b: (b, 0)),
                   pl.BlockSpec((t, MLA_KV_LORA), lambda b: (b, 0))),
        compiler_params=_params(("arbitrary",)),
        name="prompt_attn",
    )(z, z, z, z, g_q, g_kv, wn, wr, w_ukv)


def _sample_na_kernel(q_ref, k0_ref, k1_ref, k2_ref, v0_ref, v1_ref, v2_ref, kc_ref, vc_ref, bias_ref, o_ref):
    scale = NA_HD ** -0.5
    q = q_ref[...]
    kl = [r[...] for r in (k0_ref, k1_ref, k2_ref)]
    vl = [r[...] for r in (v0_ref, v1_ref, v2_ref)]
    kc, vc = kc_ref[0, 0], vc_ref[0, 0]
    outs = []
    for h in range(NA_HEADS):
        sl = slice(h * NA_HD, (h + 1) * NA_HD)
        qh = q[:, sl].astype(BF16)
        s_list = [_nt(qh, kl[i][:, sl].astype(BF16)) * scale + bias_ref[0, h, :, i * ROW_BLK:(i + 1) * ROW_BLK]
                  for i in range(3)]
        s_list.append(_nt(qh, kc[:, sl].astype(BF16)) * scale)
        v_list = [vl[i][:, sl].astype(BF16) for i in range(3)] + [vc[:, sl].astype(BF16)]
        acc, den = _softmax_pv(s_list, v_list)
        outs.append(acc / den)
    o_ref[...] = jnp.concatenate(outs, axis=-1)


def _sample_na(z, cache_k, cache_v, bias_tab, layer, n_batch):
    m = z.shape[0]
    t = ROW_BLK
    nblk = m // n_batch // t
    past = cache_k.shape[2]

    def kv_map(col, i):
        return lambda b, rb: (b * nblk + jnp.clip(rb - 1, 0, nblk - 3) + i, col // NA_W)

    def variant(b, rb):
        return (jnp.where(rb == 0, 0, jnp.where(rb == nblk - 1, 2, 1)), 0, 0, 0)

    cache_spec = pl.BlockSpec((1, 1, past, NA_W), lambda b, rb: (b, layer, 0, 0))
    return pl.pallas_call(
        _sample_na_kernel,
        out_shape=jax.ShapeDtypeStruct((m, NA_W), F32),
        grid=(n_batch, nblk),
        in_specs=[pl.BlockSpec((t, NA_W), lambda b, rb: (b * nblk + rb, C_NA_Q // NA_W))]
                 + [pl.BlockSpec((t, NA_W), kv_map(C_NA_K, i)) for i in range(3)]
                 + [pl.BlockSpec((t, NA_W), kv_map(C_NA_V, i)) for i in range(3)]
                 + [cache_spec, cache_spec,
                    pl.BlockSpec((1, NA_HEADS, t, 3 * t), variant)],
        out_specs=pl.BlockSpec((t, NA_W), lambda b, rb: (b * nblk + rb, 0)),
        compiler_params=_params(("arbitrary", "arbitrary")),
        name="sample_na",
    )(z, z, z, z, z, z, z, cache_k, cache_v, bias_tab)


def _sample_mla_kv_kernel(cckv_ref, ckr_ref, ckv_ref, kr_ref, krp_ref, cos_ref, sin_ref, gkv_ref, wukv_ref,
                          kv_ref, kro_ref):
    j = pl.program_id(1)

    @pl.when(j == 0)
    def _():
        kv_ref[0] = _mm(cckv_ref[0, 0].astype(BF16), wukv_ref[...]).astype(BF16)
        kro_ref[0] = ckr_ref[0, 0].astype(BF16)

    @pl.when(j > 0)
    def _():
        ckvn = _rms(ckv_ref[...], gkv_ref[...])
        kv_ref[0] = _mm(ckvn.astype(BF16), wukv_ref[...]).astype(BF16)
        kro_ref[0] = (kr_ref[...] * cos_ref[...] + krp_ref[...] * sin_ref[...]).astype(BF16)


def _sample_mla_kv(z, cache_ckv, cache_kr_pad, cos_t, sin_t, g_kv, w_ukv, layer, n_batch):
    m = z.shape[0]
    n = m // n_batch
    past = cache_ckv.shape[2]
    t = past
    nb = n // t
    hw = MLA_HEADS * LANE

    def zrow(b, j):
        return b * nb + jnp.maximum(j - 1, 0)

    return pl.pallas_call(
        _sample_mla_kv_kernel,
        out_shape=(jax.ShapeDtypeStruct((n_batch, past + n, hw), BF16),
                   jax.ShapeDtypeStruct((n_batch, past + n, LANE), BF16)),
        grid=(n_batch, nb + 1),
        in_specs=[pl.BlockSpec((1, 1, past, MLA_KV_LORA), lambda b, j: (b, layer, 0, 0)),
                  pl.BlockSpec((1, 1, past, LANE), lambda b, j: (b, layer, 0, 0)),
                  pl.BlockSpec((t, MLA_KV_LORA), lambda b, j: (zrow(b, j), C_MLA_CKV // MLA_KV_LORA)),
                  pl.BlockSpec((t, LANE), lambda b, j: (zrow(b, j), C_MLA_KR // LANE)),
                  pl.BlockSpec((t, LANE), lambda b, j: (zrow(b, j), C_MLA_KRP // LANE)),
                  pl.BlockSpec((t, LANE), lambda b, j: (jnp.maximum(j - 1, 0), 0)),
                  pl.BlockSpec((t, LANE), lambda b, j: (jnp.maximum(j - 1, 0), 0)),
                  _const_spec((1, MLA_KV_LORA)), _const_spec((MLA_KV_LORA, hw))],
        out_specs=(pl.BlockSpec((1, t, hw), lambda b, j: (b, j, 0)),
                   pl.BlockSpec((1, t, LANE), lambda b, j: (b, j, 0))),
        compiler_params=_params(("arbitrary", "arbitrary")),
        name="sample_mla_kv",
    )(cache_ckv, cache_kr_pad, z, z, z, cos_t, sin_t, g_kv, w_ukv)


def _sample_mla_attn_kernel(cq_ref, cos_ref, sin_ref, gq_ref, wn_ref, wr_ref, wrp_ref, kv_ref, kr_ref, o_ref):
    cqn = _rms(cq_ref[...], gq_ref[...]).astype(BF16)
    qn = _mm(cqn, wn_ref[...]).astype(BF16)
    qr = _mm(cqn, wr_ref[...])
    qrp = _mm(cqn, wrp_ref[...])
    cos, sin = cos_ref[...], sin_ref[...]
    _mla_heads(qn, lambda sl: (qr[:, sl] * cos + qrp[:, sl] * sin).astype(BF16), kv_ref[0], kr_ref[0], o_ref)


def _sample_mla_attn(z, cos_t, sin_t, g_q, wn, wr, wrp, kv_all, kr_all, n_batch):
    m = z.shape[0]
    t = ROW_BLK
    nblk = m // n_batch // t
    nk = kv_all.shape[1]
    hw = MLA_HEADS * LANE
    return pl.pallas_call(
        _sample_mla_attn_kernel,
        out_shape=jax.ShapeDtypeStruct((m, MLA_W), F32),
        grid=(n_batch, nblk),
        in_specs=[pl.BlockSpec((t, MLA_Q_LORA), lambda b, i: (b * nblk + i, C_MLA_CQ // MLA_Q_LORA)),
                  pl.BlockSpec((t, LANE), lambda b, i: (i, 0)),
                  pl.BlockSpec((t, LANE), lambda b, i: (i, 0)),
                  _const_spec((1, MLA_Q_LORA)),
                  _const_spec((MLA_Q_LORA, hw)), _const_spec((MLA_Q_LORA, hw)), _const_spec((MLA_Q_LORA, hw)),
                  pl.BlockSpec((1, nk, hw), lambda b, i: (b, 0, 0)),
                  pl.BlockSpec((1, nk, LANE), lambda b, i: (b, 0, 0))],
        out_specs=pl.BlockSpec((t, MLA_W), lambda b, i: (b * nblk + i, 0)),
        compiler_params=_params(("arbitrary", "arbitrary")),
        name="sample_mla_attn",
    )(z, cos_t, sin_t, g_q, wn, wr, wrp, kv_all, kr_all)


def _hg_direction(fwd, q_ref, zf_ref, v_ref, loglb_ref, l1m_ref, s_ref, o_ref, qs, ks, bs, vs):
    t = ROW_BLK
    nchunk = t // HG_CHUNK
    q = _silu(q_ref[...])
    zf = zf_ref[...]
    v = v_ref[...]
    soft = jnp.log1p(jnp.exp(-jnp.abs(zf)))
    a1 = jnp.broadcast_to(loglb_ref[...], zf.shape)
    a2 = l1m_ref[...] + (jnp.minimum(zf, 0.0) - soft)
    logf = jnp.maximum(a1, a2) + jnp.log1p(jnp.exp(-jnp.abs(a1 - a2)))
    k = jnp.exp(l1m_ref[...] - jnp.maximum(zf, 0.0) - soft)

    ri = lax.broadcasted_iota(jnp.int32, (t, t), 0)
    ci = lax.broadcasted_iota(jnp.int32, (t, t), 1)
    same = (ri // HG_CHUNK) == (ci // HG_CHUNK)
    tri = same & ((ri >= ci) if fwd else (ri <= ci))
    b = _mm(tri.astype(F32), logf, HI)
    tot = _mm(same.astype(F32), logf, HI)

    qt = q * jnp.exp(b)
    kt = k * jnp.exp(jnp.minimum(-b, HG_EXP_CLAMP))
    kh = k * jnp.exp(tot - b)
    qt16, kt16, kh16, v16 = qt.astype(BF16), kt.astype(BF16), kh.astype(BF16), v.astype(BF16)

    outs = []
    for h in range(HG_HEADS):
        sl = slice(h * HG_DK, (h + 1) * HG_DK)
        a = jnp.where(tri, _nt(qt16[:, sl], kt16[:, sl]), 0.0)
        outs.append(_mm(a.astype(BF16), v16[:, sl]))
    o_ref[...] = jnp.concatenate(outs, axis=-1)

    qs[...] = q
    ks[...] = k
    bs[...] = b
    vs[...] = v
    lane = lax.broadcasted_iota(jnp.int32, (HG_KW, LANE), 0) // HG_DK
    col = lax.broadcasted_iota(jnp.int32, (HG_KW, LANE), 1)
    head_sum = (lane == col).astype(F32)
    srow = lax.broadcasted_iota(jnp.int32, (HG_CHUNK, 1), 0)
    for c in range(nchunk):
        r0 = c * HG_CHUNK
        worst = jnp.max(-tot[r0:r0 + 1, :])

        @pl.when(worst > HG_SAFE_DECAY)
        def _():
            kc = ks[r0:r0 + HG_CHUNK, :]
            bc = bs[r0:r0 + HG_CHUNK, :]
            vc = vs[r0:r0 + HG_CHUNK, :]

            def body(i, carry):
                qrow = qs[pl.ds(r0 + i, 1), :]
                brow = bs[pl.ds(r0 + i, 1), :]
                p = qrow * kc * jnp.exp(jnp.minimum(brow - bc, 0.0))
                keep = (srow <= i) if fwd else (srow >= i)
                p = jnp.where(keep, p, 0.0)
                a = _mm(p, head_sum, HI)
                a_full = _nt(a, head_sum, HI)
                o_ref[pl.ds(r0 + i, 1), :] = jnp.sum(a_full * vc, axis=0, keepdims=True)
                return carry

            lax.fori_loop(0, HG_CHUNK, body, 0)

    gw = 4 * HG_DK
    bd_r = lax.broadcasted_iota(jnp.int32, (gw, gw), 0) // HG_DV
    bd_c = lax.broadcasted_iota(jnp.int32, (gw, gw), 1) // HG_DK
    bd = bd_r == bd_c
    order = range(nchunk) if fwd else range(nchunk - 1, -1, -1)
    for g in range(2):
        ls = slice(g * gw, (g + 1) * gw)
        s = s_ref[g]
        for c in order:
            rs = slice(c * HG_CHUNK, (c + 1) * HG_CHUNK)
            o_ref[rs, ls] += _nt(qt16[rs, ls], s.astype(BF16))
            u = _tn(v16[rs, ls], kh16[rs, ls])
            s = s * jnp.exp(tot[c * HG_CHUNK:c * HG_CHUNK + 1, ls]) + jnp.where(bd, u, 0.0)
        s_ref[g] = s


def _hgrn_kernel(qf_ref, ff_ref, vf_ref, qb_ref, fb_ref, vb_ref, lbf_ref, l1f_ref, lbb_ref, l1b_ref,
                 s0f_ref, s0b_ref, of_ref, ob_ref, sf_out, sb_out, sf, sb, qs, ks, bs, vs):
    i = pl.program_id(1)

    @pl.when(i == 0)
    def _():
        sf[...] = s0f_ref[0]
        sb[...] = s0b_ref[0]

    _hg_direction(True, qf_ref, ff_ref, vf_ref, lbf_ref, l1f_ref, sf, of_ref, qs, ks, bs, vs)
    _hg_direction(False, qb_ref, fb_ref, vb_ref, lbb_ref, l1b_ref, sb, ob_ref, qs, ks, bs, vs)

    @pl.when(i == pl.num_programs(1) - 1)
    def _():
        sf_out[0] = sf[...]
        sb_out[0] = sb[...]


def _hgrn(z, n_batch, lb_f, lb_b, s0f_bd, s0b_bd):
    m = z.shape[0]
    t = ROW_BLK
    nblk = m // n_batch // t
    loglb_f, l1m_f = _lb_logs(lb_f)
    loglb_b, l1m_b = _lb_logs(lb_b)
    row = lambda a: a.reshape(1, HG_KW)

    def fmap(col):
        return lambda b, i: (b * nblk + i, col // HG_KW)

    def bmap(col):
        return lambda b, i: (b * nblk + nblk - 1 - i, col // HG_KW)

    blk = lambda imap: pl.BlockSpec((t, HG_KW), imap)
    st_spec = pl.BlockSpec((1, 2, 4 * HG_DV, 4 * HG_DK), lambda b, i: (b, 0, 0, 0))
    st_shape = jax.ShapeDtypeStruct((n_batch, 2, 4 * HG_DV, 4 * HG_DK), F32)
    return pl.pallas_call(
        _hgrn_kernel,
        out_shape=(jax.ShapeDtypeStruct((m, HG_W), F32), jax.ShapeDtypeStruct((m, HG_W), F32), st_shape, st_shape),
        grid=(n_batch, nblk),
        in_specs=[blk(fmap(C_HG_Q)), blk(fmap(C_HG_FF)), blk(fmap(C_HG_I)),
                  blk(bmap(C_HG_Q)), blk(bmap(C_HG_FB)), blk(bmap(C_HG_I)),
                  _const_spec((1, HG_KW)), _const_spec((1, HG_KW)), _const_spec((1, HG_KW)), _const_spec((1, HG_KW)),
                  st_spec, st_spec],
        out_specs=(pl.BlockSpec((t, HG_W), lambda b, i: (b * nblk + i, 0)),
                   pl.BlockSpec((t, HG_W), lambda b, i: (b * nblk + nblk - 1 - i, 0)),
                   st_spec, st_spec),
        scratch_shapes=[pltpu.VMEM((2, 4 * HG_DV, 4 * HG_DK), F32), pltpu.VMEM((2, 4 * HG_DV, 4 * HG_DK), F32)]
                       + [pltpu.VMEM((t, HG_KW), F32)] * 4,
        compiler_params=_params(("arbitrary", "arbitrary")),
        name="hgrn_scan",
    )(z, z, z, z, z, z, row(loglb_f), row(l1m_f), row(loglb_b), row(l1m_b), s0f_bd, s0b_bd)


def _out_kernel(x_ref, mod_ref, gpost_ref, ghg_ref, ona_ref, omla_ref, of_ref, ob_ref,
                gna_ref, gmla_ref, ghgate_ref, mg_ref, wna_ref, wmla_ref, whg_ref, wout_ref, y_ref):
    o = of_ref[...] + ob_ref[...]
    hr = lax.broadcasted_iota(jnp.int32, (HG_W, HG_W), 0) // HG_DV
    hc = lax.broadcasted_iota(jnp.int32, (HG_W, HG_W), 1) // HG_DV
    head_mean = jnp.where(hr == hc, 1.0 / HG_DV, 0.0).astype(F32)
    ms = _mm(o * o, head_mean, HI)
    o_hg = o * lax.rsqrt(ms + NORM_EPS) * ghg_ref[...]

    def branch(o_b, gate_ref, w_ref):
        return _mm((o_b * _silu(gate_ref[...])).astype(BF16), w_ref[...])

    merged = (jax.nn.sigmoid(mg_ref[:, 0:D_MODEL]) * branch(ona_ref[...], gna_ref, wna_ref)
              + jax.nn.sigmoid(mg_ref[:, D_MODEL:2 * D_MODEL]) * branch(omla_ref[...], gmla_ref, wmla_ref)
              + jax.nn.sigmoid(mg_ref[:, 2 * D_MODEL:3 * D_MODEL]) * branch(o_hg, ghgate_ref, whg_ref))
    out = _mm(merged.astype(BF16), wout_ref[...])
    y_ref[...] = x_ref[...] + mod_ref[0, 2:3, :] * _rms(out, gpost_ref[...])


def _out_proj(x, z, mod3, g_post, g_hg, o_na, o_mla, o_f, o_b, w_na, w_mla, w_hg, w_out, rows_per_mod, mod_base):
    m = x.shape[0]
    t = ROW_BLK
    tiles_per_mod = rows_per_mod // t
    w512 = lambda: pl.BlockSpec((t, NA_W), lambda i: (i, 0))
    zcol = lambda col, w: pl.BlockSpec((t, w), lambda i: (i, col // w))
    one = pl.Buffered(1)
    return pl.pallas_call(
        _out_kernel,
        out_shape=jax.ShapeDtypeStruct((m, D_MODEL), F32),
        grid=(m // t,),
        in_specs=[pl.BlockSpec((t, D_MODEL), lambda i: (i, 0)),
                  pl.BlockSpec((1, 3, D_MODEL), lambda i: (mod_base + i // tiles_per_mod, 0, 0)),
                  _const_spec((1, D_MODEL)), _const_spec((1, HG_W)),
                  w512(), w512(), w512(), w512(),
                  zcol(C_NA_G, NA_W), zcol(C_MLA_G, MLA_W), zcol(C_HG_G, HG_W),
                  zcol(C_MG, N_BRANCH * D_MODEL),
                  pl.BlockSpec((NA_W, D_MODEL), lambda i: (0, 0), pipeline_mode=one),
                  pl.BlockSpec((MLA_W, D_MODEL), lambda i: (0, 0), pipeline_mode=one),
                  pl.BlockSpec((HG_W, D_MODEL), lambda i: (0, 0), pipeline_mode=one),
                  pl.BlockSpec((D_MODEL, D_MODEL), lambda i: (0, 0), pipeline_mode=one)],
        out_specs=pl.BlockSpec((t, D_MODEL), lambda i: (i, 0)),
        compiler_params=_params(("arbitrary",)),
        name="out_proj",
    )(x, mod3, g_post, g_hg, o_na, o_mla, o_f, o_b, z, z, z, z, w_na, w_mla, w_hg, w_out)


def kernel(x_prompt, x_sample, cache_na_k, cache_na_v, cache_mla_ckv, cache_mla_krope, state_hgrn_fwd, state_hgrn_bwd, c, c_ctx, w_ada, b_ada, g_pre, g_post, w_in, na_rpb, g_mla_q, w_mla_uq, g_mla_kv, w_mla_ukv, hg_lb_fwd, hg_lb_bwd, g_hg_out, w_br_na, w_br_mla, w_br_hg, w_out):
    bp, sp, _ = x_prompt.shape
    bs, ss, _ = x_sample.shape
    past = cache_na_k.shape[2]

    cvec = jnp.concatenate([c_ctx[None, :], c, jnp.zeros((8 - 1 - bs, D_MODEL), F32)], axis=0)
    mod = _modulation(cvec, w_ada, b_ada).reshape(DEPTH, 8, 3, D_MODEL)

    lb_f_all = _lower_bounds(hg_lb_fwd)
    lb_b_all = _lower_bounds(hg_lb_bwd)
    cos_t, sin_t = _rope_tables(ss)
    cache_k = cache_na_k.reshape(bs, DEPTH, past, NA_W)
    cache_v = cache_na_v.reshape(bs, DEPTH, past, NA_W)
    cache_kr_pad = jnp.pad(cache_mla_krope, ((0, 0), (0, 0), (0, 0), (0, LANE - MLA_ROPE)))
    zero_state = jnp.zeros((bp, 2, 4 * HG_DV, 4 * HG_DK), F32)

    yp = x_prompt.reshape(bp * sp, D_MODEL)
    ys = x_sample.reshape(bs * ss, D_MODEL)
    new_k, new_v, new_ckv, new_kr, new_sf, new_sb = [], [], [], [], [], []
    for l in range(DEPTH):
        w_packed = _pack_w_in(w_in[l])
        wn, wr, wrp = _pack_w_uq(w_mla_uq[l])
        w_ukv = w_mla_ukv[l].astype(BF16)
        w_na, w_mla, w_hg, w_o = (w.astype(BF16) for w in (w_br_na[l], w_br_mla[l], w_br_hg[l], w_out[l]))
        g_q, g_kv = g_mla_q[l][None, :], g_mla_kv[l][None, :]
        gpre, gpost, ghg = g_pre[l][None, :], g_post[l][None, :], g_hg_out[l][None, :]
        mod3 = mod[l]

        zp = _in_proj(yp, mod3, gpre, w_packed, bp * sp, 0)
        o_na, o_mla, ckvn = _prompt_attn(zp, bp, g_q, g_kv, wn, wr, w_ukv)
        o_f, o_b, sf, sb = _hgrn(zp, bp, lb_f_all[l], lb_b_all[l], zero_state, zero_state)
        yp = _out_proj(yp, zp, mod3, gpost, ghg, o_na, o_mla, o_f, o_b, w_na, w_mla, w_hg, w_o, bp * sp, 0)
        new_k.append(zp[:, C_NA_K:C_NA_K + NA_W].reshape(bp, sp, NA_HEADS, NA_HD))
        new_v.append(zp[:, C_NA_V:C_NA_V + NA_W].reshape(bp, sp, NA_HEADS, NA_HD))
        new_ckv.append(ckvn.reshape(bp, sp, MLA_KV_LORA))
        new_kr.append(zp[:, C_MLA_KR:C_MLA_KR + MLA_ROPE].reshape(bp, sp, MLA_ROPE))
        new_sf.append(_bd_to_state(sf))
        new_sb.append(_bd_to_state(sb))

        zs = _in_proj(ys, mod3, gpre, w_packed, ss, 1)
        o_na = _sample_na(zs, cache_k, cache_v, _na_bias_table(na_rpb[l]), l, bs)
        kv_all, kr_all = _sample_mla_kv(zs, cache_mla_ckv, cache_kr_pad, cos_t, sin_t, g_kv, w_ukv, l, bs)
        o_mla = _sample_mla_attn(zs, cos_t, sin_t, g_q, wn, wr, wrp, kv_all, kr_all, bs)
        o_f, o_b, _, _ = _hgrn(zs, bs, lb_f_all[l], lb_b_all[l],
                               _state_to_bd(state_hgrn_fwd[:, l]), _state_to_bd(state_hgrn_bwd[:, l]))
        ys = _out_proj(ys, zs, mod3, gpost, ghg, o_na, o_mla, o_f, o_b, w_na, w_mla, w_hg, w_o, ss, 1)

    return (yp.reshape(bp, sp, D_MODEL), ys.reshape(bs, ss, D_MODEL),
            jnp.stack(new_k, axis=1), jnp.stack(new_v, axis=1), jnp.stack(new_ckv, axis=1),
            jnp.stack(new_kr, axis=1), jnp.stack(new_sf, axis=1), jnp.stack(new_sb, axis=1))
```

```python
import functools

import numpy as np
import jax
import jax.numpy as jnp
from jax import lax
from jax.experimental import pallas as pl
from jax.experimental.pallas import tpu as pltpu

D_MODEL = 2048
DEPTH = 2
GRID_W = 64
NORM_EPS = 1e-6
NEG_BIG = -1e30
NA_HEADS = 8
NA_HD = 64
NA_W = NA_HEADS * NA_HD
NA_KR = 8
NA_KW = 16
MLA_HEADS = 8
MLA_NOPE = 64
MLA_ROPE = 32
MLA_VD = 64
MLA_QK = MLA_NOPE + MLA_ROPE
MLA_W = MLA_HEADS * MLA_VD
MLA_Q_LORA = 512
MLA_KV_LORA = 256
ROPE_BASE = 10000.0
HG_HEADS = 8
HG_DK = 64
HG_DV = 64
HG_KW = HG_HEADS * HG_DK
HG_W = HG_HEADS * HG_DV
N_BRANCH = 3
PROJ_WIDTHS = (NA_W, NA_W, NA_W, NA_W, MLA_Q_LORA, MLA_KV_LORA, MLA_ROPE, MLA_W,
               HG_KW, HG_KW, HG_KW, HG_W, HG_W, N_BRANCH * D_MODEL)

F32 = jnp.float32
BF16 = jnp.bfloat16
HI = lax.Precision.HIGHEST

LANE = 128
ROW_BLK = 256
HG_CHUNK = 32
HG_SUB = 16
HG_SAFE_DECAY = 72.0
HG_EXP_CLAMP = 80.0
VMEM_LIMIT = 56 * 1024 * 1024

PACKED_W = 12288
C_MG = 0
C_NA_Q, C_NA_K, C_NA_V, C_NA_G = 6144, 6656, 7168, 7680
C_MLA_CQ, C_MLA_G = 8192, 8704
C_HG_Q, C_HG_FF, C_HG_FB, C_HG_I, C_HG_G = 9216, 9728, 10240, 10752, 11264
C_MLA_CKV, C_MLA_KR, C_MLA_KRP = 11776, 12032, 12160


def _nt(a, b, precision=None):
    return lax.dot_general(a, b, (((1,), (1,)), ((), ())), preferred_element_type=F32, precision=precision)


def _tn(a, b, precision=None):
    return lax.dot_general(a, b, (((0,), (0,)), ((), ())), preferred_element_type=F32, precision=precision)


def _mm(a, b, precision=None):
    return jnp.dot(a, b, preferred_element_type=F32, precision=precision)


def _rms(x, g):
    return x * lax.rsqrt(jnp.mean(x * x, axis=-1, keepdims=True) + NORM_EPS) * g


def _silu(x):
    return x * jax.nn.sigmoid(x)


def _params(sem):
    return pltpu.CompilerParams(dimension_semantics=sem, vmem_limit_bytes=VMEM_LIMIT)


def _const_spec(shape):
    nd = len(shape)
    return pl.BlockSpec(shape, lambda *_: (0,) * nd)


def _rot_half(a):
    parts = []
    for ax in range(2):
        lo_, hi_ = a[..., ax * 16:ax * 16 + 8], a[..., ax * 16 + 8:ax * 16 + 16]
        parts += [-hi_, lo_]
    return jnp.concatenate(parts, axis=-1)


def _pack_w_in(w16):
    idx = np.cumsum(PROJ_WIDTHS)[:-1].tolist()
    (na_q, na_k, na_v, na_g, cq, ckv, kr, mla_g, hq, hff, hfb, hi, hg, mg) = jnp.split(w16, idx, axis=-1)
    pad = jnp.zeros(kr.shape[:-1] + (LANE - MLA_ROPE,), w16.dtype)
    return jnp.concatenate([mg, na_q, na_k, na_v, na_g, cq, mla_g, hq, hff, hfb, hi, hg, ckv, kr, pad,
                            _rot_half(kr), pad], axis=-1)


def _pack_w_uq(w):
    w3 = w.reshape(MLA_Q_LORA, MLA_HEADS, MLA_QK)
    nope, rope = w3[..., :MLA_NOPE], w3[..., MLA_NOPE:]
    ropep = _rot_half(rope)

    def padh(a):
        z = jnp.zeros(a.shape[:-1] + (LANE - a.shape[-1],), a.dtype)
        return jnp.concatenate([a, z], axis=-1).reshape(MLA_Q_LORA, MLA_HEADS * LANE).astype(BF16)

    return padh(nope), padh(rope), padh(ropep)


def _rope_tables(n):
    t = np.arange(n)
    pos = np.stack([t // GRID_W, t % GRID_W]).astype(np.float32)
    axis_dim = MLA_ROPE // 2
    inv = ROPE_BASE ** (-jnp.arange(0, axis_dim, 2, dtype=F32) / axis_dim)
    ang = jnp.asarray(pos)[:, :, None] * inv
    ang = jnp.concatenate([ang, ang], axis=-1)
    ang = jnp.concatenate([ang[0], ang[1]], axis=-1)
    pad = jnp.zeros((n, LANE - MLA_ROPE), F32)
    return jnp.concatenate([jnp.cos(ang), pad], -1), jnp.concatenate([jnp.sin(ang), pad], -1)


def _na_bias_table(rpb):
    rows = 32
    c = np.arange(GRID_W)
    win0 = np.clip(c - NA_KW // 2, 0, GRID_W - NA_KW)
    kc = np.arange(GRID_W)
    col_ok = (kc[None, :] >= win0[:, None]) & (kc[None, :] < win0[:, None] + NA_KW)
    rpb = rpb.astype(F32)
    edge = GRID_W - NA_KW
    ext = jnp.concatenate([jnp.broadcast_to(rpb[..., :1], rpb.shape[:-1] + (edge,)), rpb,
                           jnp.broadcast_to(rpb[..., -1:], rpb.shape[:-1] + (edge,))], axis=-1)
    toep = jnp.stack([ext[..., GRID_W - 1 - ci:2 * GRID_W - 1 - ci] for ci in range(GRID_W)], axis=2)
    toep = jnp.where(jnp.asarray(col_ok), toep, NEG_BIG)
    masked = jnp.full((NA_HEADS, GRID_W, GRID_W), NEG_BIG, F32)
    out = []
    for start, r0 in ((0, 0), (4, 8), (20, 28)):
        per_q = []
        for qr in range(4):
            r = r0 + qr
            kr0 = min(max(r - NA_KR // 2, 0), rows - NA_KR)
            tiles = []
            for j in range(12):
                kabs = start + j
                ok = kr0 <= kabs < kr0 + NA_KR
                tiles.append(toep[:, kabs - r + NA_KR - 1] if ok else masked)
            per_q.append(jnp.concatenate(tiles, axis=-1))
        out.append(jnp.concatenate(per_q, axis=1))
    return jnp.stack(out)


def _lower_bounds(p):
    s = jax.nn.softmax(p.astype(F32), axis=0)
    return jnp.cumsum(s, axis=0) - s[0]


def _lb_logs(lb):
    pos = lb > 0
    log_lb = jnp.where(pos, jnp.log(jnp.where(pos, lb, 1.0)), NEG_BIG)
    return log_lb, jnp.log1p(-lb)


def _state_to_bd(s):
    b = s.shape[0]
    st = s.astype(F32).transpose(0, 1, 3, 2).reshape(b, 2, 4, HG_DV, HG_DK)
    eye = jnp.eye(4, dtype=F32)
    bd = st[:, :, :, :, None, :] * eye[None, None, :, None, :, None]
    return bd.reshape(b, 2, 4 * HG_DV, 4 * HG_DK)


def _bd_to_state(bd):
    b = bd.shape[0]
    x = bd.reshape(b, 2, 4, HG_DV, 4, HG_DK)
    x = jnp.stack([x[:, :, h, :, h, :] for h in range(4)], axis=2)
    return x.transpose(0, 1, 2, 4, 3).reshape(b, HG_HEADS, HG_DK, HG_DV)


def _mod_kernel(c_ref, w_ref, b_ref, o_ref):
    s = _silu(c_ref[...])
    o_ref[0] = _mm(s, w_ref[0], HI) + b_ref[0]


def _modulation(cvec, w_ada, b_ada):
    tn = 1024
    n3 = 3 * D_MODEL
    return pl.pallas_call(
        _mod_kernel,
        out_shape=jax.ShapeDtypeStruct((DEPTH, 8, n3), F32),
        grid=(DEPTH, n3 // tn),
        in_specs=[pl.BlockSpec((8, D_MODEL), lambda l, j: (0, 0)),
                  pl.BlockSpec((1, D_MODEL, tn), lambda l, j: (l, 0, j)),
                  pl.BlockSpec((1, 1, tn), lambda l, j: (l, 0, j))],
        out_specs=pl.BlockSpec((1, 8, tn), lambda l, j: (l, 0, j)),
        compiler_params=_params(("arbitrary", "arbitrary")),
        name="adaln_mod",
    )(cvec, w_ada, b_ada.reshape(DEPTH, 1, n3))


def _in_kernel(x_ref, mod_ref, g_ref, w_ref, z_ref, h_scr):
    @pl.when(pl.program_id(1) == 0)
    def _():
        y = _rms(x_ref[...], g_ref[...])
        h = y * (1.0 + mod_ref[0, 1:2, :]) + mod_ref[0, 0:1, :]
        h_scr[...] = h.astype(BF16)

    z_ref[...] = _mm(h_scr[...], w_ref[0])


def _in_proj(x, mod3, g_pre, w_packed, layer, rows_per_mod, mod_base):
    m = x.shape[0]
    tm, tn = 512, 1024
    tiles_per_mod = rows_per_mod // tm
    return pl.pallas_call(
        _in_kernel,
        out_shape=jax.ShapeDtypeStruct((m, PACKED_W), F32),
        grid=(m // tm, PACKED_W // tn),
        in_specs=[pl.BlockSpec((tm, D_MODEL), lambda i, j: (i, 0)),
                  pl.BlockSpec((1, 3, D_MODEL), lambda i, j: (mod_base + i // tiles_per_mod, 0, 0)),
                  pl.BlockSpec((1, D_MODEL), lambda i, j: (0, 0)),
                  pl.BlockSpec((1, D_MODEL, tn), lambda i, j: (layer, 0, j))],
        out_specs=pl.BlockSpec((tm, tn), lambda i, j: (i, j)),
        scratch_shapes=[pltpu.VMEM((tm, D_MODEL), BF16)],
        compiler_params=_params(("arbitrary", "arbitrary")),
        name="in_proj",
    )(x, mod3, g_pre, w_packed)


def _softmax_pv(s_list, v_list):
    m = s_list[0].max(axis=-1, keepdims=True)
    for s in s_list[1:]:
        m = jnp.maximum(m, s.max(axis=-1, keepdims=True))
    acc, den = None, None
    for s, v in zip(s_list, v_list):
        e = jnp.exp(s - m)
        d = e.sum(axis=-1, keepdims=True)
        o = _mm(e.astype(BF16), v)
        acc = o if acc is None else acc + o
        den = d if den is None else den + d
    return acc, den


def _mla_heads(qn, qr_fn, kv, krp, o_ref):
    scale = MLA_QK ** -0.5
    outs = []
    for h in range(MLA_HEADS):
        sl = slice(h * LANE, (h + 1) * LANE)
        kvh = kv[:, sl]
        s = (_nt(qn[:, sl], kvh) + _nt(qr_fn(sl), krp)) * scale
        acc, den = _softmax_pv([s], [kvh])
        outs.append(acc[:, MLA_NOPE:] / den)
    o_ref[...] = jnp.concatenate(outs, axis=-1)


def _prompt_attn_kernel(qkv_ref, cq_ref, ckv_ref, kr_ref, gq_ref, gkv_ref, wn_ref, wr_ref, wukv_ref,
                        ona_ref, omla_ref, ckvn_ref):
    qkv = qkv_ref[...]
    scale = NA_HD ** -0.5
    outs = []
    for h in range(NA_HEADS):
        qh = qkv[:, h * NA_HD:(h + 1) * NA_HD].astype(BF16)
        kh = qkv[:, NA_W + h * NA_HD:NA_W + (h + 1) * NA_HD].astype(BF16)
        vh = qkv[:, 2 * NA_W + h * NA_HD:2 * NA_W + (h + 1) * NA_HD].astype(BF16)
        acc, den = _softmax_pv([_nt(qh, kh) * scale], [vh])
        outs.append(acc / den)
    ona_ref[...] = jnp.concatenate(outs, axis=-1)

    cqn = _rms(cq_ref[...], gq_ref[...]).astype(BF16)
    qn = _mm(cqn, wn_ref[...]).astype(BF16)
    qr = _mm(cqn, wr_ref[...]).astype(BF16)
    ckvn = _rms(ckv_ref[...], gkv_ref[...])
    ckvn_ref[...] = ckvn
    kv = _mm(ckvn.astype(BF16), wukv_ref[...]).astype(BF16)
    krp = kr_ref[...].astype(BF16)
    _mla_heads(qn, lambda sl: qr[:, sl], kv, krp, omla_ref)


def _prompt_attn(z, n_batch, g_q, g_kv, wn, wr, w_ukv):
    m = z.shape[0]
    t = ROW_BLK
    hw = MLA_HEADS * LANE
    return pl.pallas_call(
        _prompt_attn_kernel,
        out_shape=(jax.ShapeDtypeStruct((m, NA_W), F32), jax.ShapeDtypeStruct((m, MLA_W), F32),
                   jax.ShapeDtypeStruct((m, MLA_KV_LORA), F32)),
        grid=(n_batch,),
        in_specs=[pl.BlockSpec((t, 3 * NA_W), lambda b: (b, C_NA_Q // (3 * NA_W))),
                  pl.BlockSpec((t, MLA_Q_LORA), lambda b: (b, C_MLA_CQ // MLA_Q_LORA)),
                  pl.BlockSpec((t, MLA_KV_LORA), lambda b: (b, C_MLA_CKV // MLA_KV_LORA)),
                  pl.BlockSpec((t, LANE), lambda b: (b, C_MLA_KR // LANE)),
                  _const_spec((1, MLA_Q_LORA)), _const_spec((1, MLA_KV_LORA)),
                  _const_spec((MLA_Q_LORA, hw)), _const_spec((MLA_Q_LORA, hw)), _const_spec((MLA_KV_LORA, hw))],
        out_specs=(pl.BlockSpec((t, NA_W), lambda b: (b, 0)), pl.BlockSpec((t, MLA_W), lambda b: (b, 0)),
                   pl.BlockSpec((t, MLA_KV_LORA), lambda b: (b, 0))),
        compiler_params=_params(("arbitrary",)),
        name="prompt_attn",
    )(z, z, z, z, g_q, g_kv, wn, wr, w_ukv)


def _sample_na_kernel(q_ref, k0_ref, k1_ref, k2_ref, v0_ref, v1_ref, v2_ref, kc_ref, vc_ref, bias_ref, o_ref):
    scale = NA_HD ** -0.5
    q = q_ref[...]
    kl = [r[...] for r in (k0_ref, k1_ref, k2_ref)]
    vl = [r[...] for r in (v0_ref, v1_ref, v2_ref)]
    kc, vc = kc_ref[0, 0], vc_ref[0, 0]
    outs = []
    for h in range(NA_HEADS):
        sl = slice(h * NA_HD, (h + 1) * NA_HD)
        qh = q[:, sl].astype(BF16)
        s_list = [_nt(qh, kl[i][:, sl].astype(BF16)) * scale + bias_ref[0, h, :, i * ROW_BLK:(i + 1) * ROW_BLK]
                  for i in range(3)]
        s_list.append(_nt(qh, kc[:, sl].astype(BF16)) * scale)
        v_list = [vl[i][:, sl].astype(BF16) for i in range(3)] + [vc[:, sl].astype(BF16)]
        acc, den = _softmax_pv(s_list, v_list)
        outs.append(acc / den)
    o_ref[...] = jnp.concatenate(outs, axis=-1)


def _sample_na(z, cache_k, cache_v, bias_tab, layer, n_batch):
    m = z.shape[0]
    t = ROW_BLK
    nblk = m // n_batch // t
    past = cache_k.shape[2]

    def kv_map(col, i):
        return lambda b, rb: (b * nblk + jnp.clip(rb - 1, 0, nblk - 3) + i, col // NA_W)

    def variant(b, rb):
        return (jnp.where(rb == 0, 0, jnp.where(rb == nblk - 1, 2, 1)), 0, 0, 0)

    cache_spec = pl.BlockSpec((1, 1, past, NA_W), lambda b, rb: (b, layer, 0, 0))
    return pl.pallas_call(
        _sample_na_kernel,
        out_shape=jax.ShapeDtypeStruct((m, NA_W), F32),
        grid=(n_batch, nblk),
        in_specs=[pl.BlockSpec((t, NA_W), lambda b, rb: (b * nblk + rb, C_NA_Q // NA_W))]
                 + [pl.BlockSpec((t, NA_W), kv_map(C_NA_K, i)) for i in range(3)]
                 + [pl.BlockSpec((t, NA_W), kv_map(C_NA_V, i)) for i in range(3)]
                 + [cache_spec, cache_spec,
                    pl.BlockSpec((1, NA_HEADS, t, 3 * t), variant)],
        out_specs=pl.BlockSpec((t, NA_W), lambda b, rb: (b * nblk + rb, 0)),
        compiler_params=_params(("arbitrary", "arbitrary")),
        name="sample_na",
    )(z, z, z, z, z, z, z, cache_k, cache_v, bias_tab)


def _sample_mla_kv_kernel(cckv_ref, ckr_ref, ckv_ref, kr_ref, krp_ref, cos_ref, sin_ref, gkv_ref, wukv_ref,
                          kv_ref, kro_ref):
    j = pl.program_id(1)

    @pl.when(j == 0)
    def _():
        kv_ref[0] = _mm(cckv_ref[0, 0].astype(BF16), wukv_ref[...]).astype(BF16)
        kro_ref[0] = ckr_ref[0, 0].astype(BF16)

    @pl.when(j > 0)
    def _():
        ckvn = _rms(ckv_ref[...], gkv_ref[...])
        kv_ref[0] = _mm(ckvn.astype(BF16), wukv_ref[...]).astype(BF16)
        kro_ref[0] = (kr_ref[...] * cos_ref[...] + krp_ref[...] * sin_ref[...]).astype(BF16)


def _sample_mla_kv(z, cache_ckv, cache_kr_pad, cos_t, sin_t, g_kv, w_ukv, layer, n_batch):
    m = z.shape[0]
    n = m // n_batch
    past = cache_ckv.shape[2]
    t = past
    nb = n // t
    hw = MLA_HEADS * LANE

    def zrow(b, j):
        return b * nb + jnp.maximum(j - 1, 0)

    return pl.pallas_call(
        _sample_mla_kv_kernel,
        out_shape=(jax.ShapeDtypeStruct((n_batch, past + n, hw), BF16),
                   jax.ShapeDtypeStruct((n_batch, past + n, LANE), BF16)),
        grid=(n_batch, nb + 1),
        in_specs=[pl.BlockSpec((1, 1, past, MLA_KV_LORA), lambda b, j: (b, layer, 0, 0)),
                  pl.BlockSpec((1, 1, past, LANE), lambda b, j: (b, layer, 0, 0)),
                  pl.BlockSpec((t, MLA_KV_LORA), lambda b, j: (zrow(b, j), C_MLA_CKV // MLA_KV_LORA)),
                  pl.BlockSpec((t, LANE), lambda b, j: (zrow(b, j), C_MLA_KR // LANE)),
                  pl.BlockSpec((t, LANE), lambda b, j: (zrow(b, j), C_MLA_KRP // LANE)),
                  pl.BlockSpec((t, LANE), lambda b, j: (jnp.maximum(j - 1, 0), 0)),
                  pl.BlockSpec((t, LANE), lambda b, j: (jnp.maximum(j - 1, 0), 0)),
                  _const_spec((1, MLA_KV_LORA)), _const_spec((MLA_KV_LORA, hw))],
        out_specs=(pl.BlockSpec((1, t, hw), lambda b, j: (b, j, 0)),
                   pl.BlockSpec((1, t, LANE), lambda b, j: (b, j, 0))),
        compiler_params=_params(("arbitrary", "arbitrary")),
        name="sample_mla_kv",
    )(cache_ckv, cache_kr_pad, z, z, z, cos_t, sin_t, g_kv, w_ukv)


def _sample_mla_attn_kernel(cq_ref, cos_ref, sin_ref, gq_ref, wn_ref, wr_ref, wrp_ref, kv_ref, kr_ref, o_ref):
    cqn = _rms(cq_ref[...], gq_ref[...]).astype(BF16)
    qn = _mm(cqn, wn_ref[...]).astype(BF16)
    qr = _mm(cqn, wr_ref[...])
    qrp = _mm(cqn, wrp_ref[...])
    cos, sin = cos_ref[...], sin_ref[...]
    _mla_heads(qn, lambda sl: (qr[:, sl] * cos + qrp[:, sl] * sin).astype(BF16), kv_ref[0], kr_ref[0], o_ref)


def _sample_mla_attn(z, cos_t, sin_t, g_q, wn, wr, wrp, kv_all, kr_all, n_batch):
    m = z.shape[0]
    t = ROW_BLK
    nblk = m // n_batch // t
    nk = kv_all.shape[1]
    hw = MLA_HEADS * LANE
    return pl.pallas_call(
        _sample_mla_attn_kernel,
        out_shape=jax.ShapeDtypeStruct((m, MLA_W), F32),
        grid=(n_batch, nblk),
        in_specs=[pl.BlockSpec((t, MLA_Q_LORA), lambda b, i: (b * nblk + i, C_MLA_CQ // MLA_Q_LORA)),
                  pl.BlockSpec((t, LANE), lambda b, i: (i, 0)),
                  pl.BlockSpec((t, LANE), lambda b, i: (i, 0)),
                  _const_spec((1, MLA_Q_LORA)),
                  _const_spec((MLA_Q_LORA, hw)), _const_spec((MLA_Q_LORA, hw)), _const_spec((MLA_Q_LORA, hw)),
                  pl.BlockSpec((1, nk, hw), lambda b, i: (b, 0, 0)),
                  pl.BlockSpec((1, nk, LANE), lambda b, i: (b, 0, 0))],
        out_specs=pl.BlockSpec((t, MLA_W), lambda b, i: (b * nblk + i, 0)),
        compiler_params=_params(("arbitrary", "arbitrary")),
        name="sample_mla_attn",
    )(z, cos_t, sin_t, g_q, wn, wr, wrp, kv_all, kr_all)


def _hg_direction(fwd, q_ref, zf_ref, v_ref, loglb_ref, l1m_ref, s_ref, o_ref, qs, ks, bs, vs):
    t = ROW_BLK
    nchunk = t // HG_CHUNK
    q = _silu(q_ref[...])
    zf = zf_ref[...]
    v = v_ref[...]
    soft = jnp.log1p(jnp.exp(-jnp.abs(zf)))
    a1 = jnp.broadcast_to(loglb_ref[...], zf.shape)
    a2 = l1m_ref[...] + (jnp.minimum(zf, 0.0) - soft)
    logf = jnp.maximum(a1, a2) + jnp.log1p(jnp.exp(-jnp.abs(a1 - a2)))
    k = jnp.exp(l1m_ref[...] - jnp.maximum(zf, 0.0) - soft)

    ri = lax.broadcasted_iota(jnp.int32, (t, t), 0)
    ci = lax.broadcasted_iota(jnp.int32, (t, t), 1)
    causal = (ri >= ci) if fwd else (ri <= ci)
    same_sub = (ri // HG_SUB) == (ci // HG_SUB)
    same = (ri // HG_CHUNK) == (ci // HG_CHUNK)
    tri_sub = same_sub & causal
    cross = same & jnp.logical_not(same_sub) & causal
    hi = logf.astype(BF16)
    lo = (logf - hi.astype(F32)).astype(BF16)

    def seg_sum(mask):
        m16 = jnp.where(mask, 1.0, 0.0).astype(BF16)
        return _mm(m16, hi) + _mm(m16, lo)

    b_sub = seg_sum(tri_sub)
    tot_sub = seg_sum(same_sub)
    tot = seg_sum(same)
    row = lax.broadcasted_iota(jnp.int32, (t, 1), 0) % HG_CHUNK
    later = (row >= HG_SUB) if fwd else (row < HG_SUB)
    b = b_sub + jnp.where(later, tot - tot_sub, 0.0)

    qt_sub = (q * jnp.exp(b_sub)).astype(BF16)
    kt_sub = (k * jnp.exp(jnp.minimum(-b_sub, HG_EXP_CLAMP))).astype(BF16)
    kh_sub = (k * jnp.exp(tot_sub - b_sub)).astype(BF16)
    qt16 = (q * jnp.exp(b)).astype(BF16)
    kh16 = (k * jnp.exp(tot - b)).astype(BF16)
    v16 = v.astype(BF16)

    outs = []
    for h in range(HG_HEADS):
        sl = slice(h * HG_DK, (h + 1) * HG_DK)
        a = (jnp.where(tri_sub, _nt(qt_sub[:, sl], kt_sub[:, sl]), 0.0)
             + jnp.where(cross, _nt(qt_sub[:, sl], kh_sub[:, sl]), 0.0))
        outs.append(_mm(a.astype(BF16), v16[:, sl]))
    o_ref[...] = jnp.concatenate(outs, axis=-1)

    qs[...] = q
    ks[...] = k
    bs[...] = b
    vs[...] = v
    lane = lax.broadcasted_iota(jnp.int32, (HG_KW, LANE), 0) // HG_DK
    col = lax.broadcasted_iota(jnp.int32, (HG_KW, LANE), 1)
    head_sum = (lane == col).astype(F32)
    srow = lax.broadcasted_iota(jnp.int32, (HG_CHUNK, 1), 0)
    for c in range(nchunk):
        r0 = c * HG_CHUNK
        worst = jnp.max(-jnp.minimum(tot_sub[r0:r0 + 1, :], tot_sub[r0 + HG_SUB:r0 + HG_SUB + 1, :]))

        @pl.when(worst > HG_SAFE_DECAY)
        def _():
            kc = ks[r0:r0 + HG_CHUNK, :]
            bc = bs[r0:r0 + HG_CHUNK, :]
            vc = vs[r0:r0 + HG_CHUNK, :]

            def body(i, carry):
                qrow = qs[pl.ds(r0 + i, 1), :]
                brow = bs[pl.ds(r0 + i, 1), :]
                p = qrow * kc * jnp.exp(jnp.minimum(brow - bc, 0.0))
                keep = (srow <= i) if fwd else (srow >= i)
                p = jnp.where(keep, p, 0.0)
                a = _mm(p, head_sum, HI)
                a_full = _nt(a, head_sum, HI)
                o_ref[pl.ds(r0 + i, 1), :] = jnp.sum(a_full * vc, axis=0, keepdims=True)
                return carry

            lax.fori_loop(0, HG_CHUNK, body, 0)

    gw = 4 * HG_DK
    bd_r = lax.broadcasted_iota(jnp.int32, (gw, gw), 0) // HG_DV
    bd_c = lax.broadcasted_iota(jnp.int32, (gw, gw), 1) // HG_DK
    bd = bd_r == bd_c
    order = range(nchunk) if fwd else range(nchunk - 1, -1, -1)
    for g in range(2):
        ls = slice(g * gw, (g + 1) * gw)
        s = s_ref[g]
        for c in order:
            rs = slice(c * HG_CHUNK, (c + 1) * HG_CHUNK)
            o_ref[rs, ls] += _nt(qt16[rs, ls], s.astype(BF16))
            u = _tn(v16[rs, ls], kh16[rs, ls])
            s = s * jnp.exp(tot[c * HG_CHUNK:c * HG_CHUNK + 1, ls]) + jnp.where(bd, u, 0.0)
        s_ref[g] = s


def _hgrn_kernel(qf_ref, ff_ref, vf_ref, qb_ref, fb_ref, vb_ref, lbf_ref, l1f_ref, lbb_ref, l1b_ref,
                 s0f_ref, s0b_ref, of_ref, ob_ref, sf_out, sb_out, sf, sb, qs, ks, bs, vs):
    i = pl.program_id(1)

    @pl.when(i == 0)
    def _():
        sf[...] = s0f_ref[0]
        sb[...] = s0b_ref[0]

    _hg_direction(True, qf_ref, ff_ref, vf_ref, lbf_ref, l1f_ref, sf, of_ref, qs, ks, bs, vs)
    _hg_direction(False, qb_ref, fb_ref, vb_ref, lbb_ref, l1b_ref, sb, ob_ref, qs, ks, bs, vs)

    @pl.when(i == pl.num_programs(1) - 1)
    def _():
        sf_out[0] = sf[...]
        sb_out[0] = sb[...]


def _hgrn(z, n_batch, lb_f, lb_b, s0f_bd, s0b_bd):
    m = z.shape[0]
    t = ROW_BLK
    nblk = m // n_batch // t
    loglb_f, l1m_f = _lb_logs(lb_f)
    loglb_b, l1m_b = _lb_logs(lb_b)
    row = lambda a: a.reshape(1, HG_KW)

    def fmap(col):
        return lambda b, i: (b * nblk + i, col // HG_KW)

    def bmap(col):
        return lambda b, i: (b * nblk + nblk - 1 - i, col // HG_KW)

    blk = lambda imap: pl.BlockSpec((t, HG_KW), imap)
    st_spec = pl.BlockSpec((1, 2, 4 * HG_DV, 4 * HG_DK), lambda b, i: (b, 0, 0, 0))
    st_shape = jax.ShapeDtypeStruct((n_batch, 2, 4 * HG_DV, 4 * HG_DK), F32)
    return pl.pallas_call(
        _hgrn_kernel,
        out_shape=(jax.ShapeDtypeStruct((m, HG_W), F32), jax.ShapeDtypeStruct((m, HG_W), F32), st_shape, st_shape),
        grid=(n_batch, nblk),
        in_specs=[blk(fmap(C_HG_Q)), blk(fmap(C_HG_FF)), blk(fmap(C_HG_I)),
                  blk(bmap(C_HG_Q)), blk(bmap(C_HG_FB)), blk(bmap(C_HG_I)),
                  _const_spec((1, HG_KW)), _const_spec((1, HG_KW)), _const_spec((1, HG_KW)), _const_spec((1, HG_KW)),
                  st_spec, st_spec],
        out_specs=(pl.BlockSpec((t, HG_W), lambda b, i: (b * nblk + i, 0)),
                   pl.BlockSpec((t, HG_W), lambda b, i: (b * nblk + nblk - 1 - i, 0)),
                   st_spec, st_spec),
        scratch_shapes=[pltpu.VMEM((2, 4 * HG_DV, 4 * HG_DK), F32), pltpu.VMEM((2, 4 * HG_DV, 4 * HG_DK), F32)]
                       + [pltpu.VMEM((t, HG_KW), F32)] * 4,
        compiler_params=_params(("arbitrary", "arbitrary")),
        name="hgrn_scan",
    )(z, z, z, z, z, z, row(loglb_f), row(l1m_f), row(loglb_b), row(l1m_b), s0f_bd, s0b_bd)


def _out_kernel(x_ref, mod_ref, gpost_ref, ghg_ref, ona_ref, omla_ref, of_ref, ob_ref,
                gna_ref, gmla_ref, ghgate_ref, mg_ref, wna_ref, wmla_ref, whg_ref, wout_ref, y_ref):
    o = of_ref[...] + ob_ref[...]
    hr = lax.broadcasted_iota(jnp.int32, (HG_W, HG_W), 0) // HG_DV
    hc = lax.broadcasted_iota(jnp.int32, (HG_W, HG_W), 1) // HG_DV
    head_mean = jnp.where(hr == hc, 1.0 / HG_DV, 0.0).astype(F32)
    ms = _mm(o * o, head_mean, HI)
    o_hg = o * lax.rsqrt(ms + NORM_EPS) * ghg_ref[...]

    def branch(o_b, gate_ref, w_ref):
        return _mm((o_b * _silu(gate_ref[...])).astype(BF16), w_ref[...])

    merged = (jax.nn.sigmoid(mg_ref[:, 0:D_MODEL]) * branch(ona_ref[...], gna_ref, wna_ref)
              + jax.nn.sigmoid(mg_ref[:, D_MODEL:2 * D_MODEL]) * branch(omla_ref[...], gmla_ref, wmla_ref)
              + jax.nn.sigmoid(mg_ref[:, 2 * D_MODEL:3 * D_MODEL]) * branch(o_hg, ghgate_ref, whg_ref))
    out = _mm(merged.astype(BF16), wout_ref[...])
    y_ref[...] = x_ref[...] + mod_ref[0, 2:3, :] * _rms(out, gpost_ref[...])


def _out_proj(x, z, mod3, g_post, g_hg, o_na, o_mla, o_f, o_b, w_na, w_mla, w_hg, w_out, rows_per_mod, mod_base):
    m = x.shape[0]
    t = ROW_BLK
    tiles_per_mod = rows_per_mod // t
    w512 = lambda: pl.BlockSpec((t, NA_W), lambda i: (i, 0))
    zcol = lambda col, w: pl.BlockSpec((t, w), lambda i: (i, col // w))
    one = pl.Buffered(1)
    return pl.pallas_call(
        _out_kernel,
        out_shape=jax.ShapeDtypeStruct((m, D_MODEL), F32),
        grid=(m // t,),
        in_specs=[pl.BlockSpec((t, D_MODEL), lambda i: (i, 0)),
                  pl.BlockSpec((1, 3, D_MODEL), lambda i: (mod_base + i // tiles_per_mod, 0, 0)),
                  _const_spec((1, D_MODEL)), _const_spec((1, HG_W)),
                  w512(), w512(), w512(), w512(),
                  zcol(C_NA_G, NA_W), zcol(C_MLA_G, MLA_W), zcol(C_HG_G, HG_W),
                  zcol(C_MG, N_BRANCH * D_MODEL),
                  pl.BlockSpec((NA_W, D_MODEL), lambda i: (0, 0), pipeline_mode=one),
                  pl.BlockSpec((MLA_W, D_MODEL), lambda i: (0, 0), pipeline_mode=one),
                  pl.BlockSpec((HG_W, D_MODEL), lambda i: (0, 0), pipeline_mode=one),
                  pl.BlockSpec((D_MODEL, D_MODEL), lambda i: (0, 0), pipeline_mode=one)],
        out_specs=pl.BlockSpec((t, D_MODEL), lambda i: (i, 0)),
        compiler_params=_params(("arbitrary",)),
        name="out_proj",
    )(x, mod3, g_post, g_hg, o_na, o_mla, o_f, o_b, z, z, z, z, w_na, w_mla, w_hg, w_out)


def kernel(x_prompt, x_sample, cache_na_k, cache_na_v, cache_mla_ckv, cache_mla_krope, state_hgrn_fwd, state_hgrn_bwd, c, c_ctx, w_ada, b_ada, g_pre, g_post, w_in, na_rpb, g_mla_q, w_mla_uq, g_mla_kv, w_mla_ukv, hg_lb_fwd, hg_lb_bwd, g_hg_out, w_br_na, w_br_mla, w_br_hg, w_out):
    bp, sp, _ = x_prompt.shape
    bs, ss, _ = x_sample.shape
    past = cache_na_k.shape[2]

    cvec = jnp.concatenate([c_ctx[None, :], c, jnp.zeros((8 - 1 - bs, D_MODEL), F32)], axis=0)
    mod = _modulation(cvec, w_ada, b_ada).reshape(DEPTH, 8, 3, D_MODEL)

    lb_f_all = _lower_bounds(hg_lb_fwd)
    lb_b_all = _lower_bounds(hg_lb_bwd)
    cos_t, sin_t = _rope_tables(ss)
    cache_k = cache_na_k.reshape(bs, DEPTH, past, NA_W)
    cache_v = cache_na_v.reshape(bs, DEPTH, past, NA_W)
    cache_kr_pad = jnp.pad(cache_mla_krope, ((0, 0), (0, 0), (0, 0), (0, LANE - MLA_ROPE)))
    zero_state = jnp.zeros((bp, 2, 4 * HG_DV, 4 * HG_DK), F32)

    yp = x_prompt.reshape(bp * sp, D_MODEL)
    ys = x_sample.reshape(bs * ss, D_MODEL)
    new_k, new_v, new_ckv, new_kr, new_sf, new_sb = [], [], [], [], [], []
    w_packed = _pack_w_in(w_in.astype(BF16))
    for l in range(DEPTH):
        wn, wr, wrp = _pack_w_uq(w_mla_uq[l])
        w_ukv = w_mla_ukv[l].astype(BF16)
        w_na, w_mla, w_hg, w_o = (w.astype(BF16) for w in (w_br_na[l], w_br_mla[l], w_br_hg[l], w_out[l]))
        g_q, g_kv = g_mla_q[l][None, :], g_mla_kv[l][None, :]
        gpre, gpost, ghg = g_pre[l][None, :], g_post[l][None, :], g_hg_out[l][None, :]
        mod3 = mod[l]

        zp = _in_proj(yp, mod3, gpre, w_packed, l, bp * sp, 0)
        o_na, o_mla, ckvn = _prompt_attn(zp, bp, g_q, g_kv, wn, wr, w_ukv)
        o_f, o_b, sf, sb = _hgrn(zp, bp, lb_f_all[l], lb_b_all[l], zero_state, zero_state)
        yp = _out_proj(yp, zp, mod3, gpost, ghg, o_na, o_mla, o_f, o_b, w_na, w_mla, w_hg, w_o, bp * sp, 0)
        new_k.append(zp[:, C_NA_K:C_NA_K + NA_W].reshape(bp, sp, NA_HEADS, NA_HD))
        new_v.append(zp[:, C_NA_V:C_NA_V + NA_W].reshape(bp, sp, NA_HEADS, NA_HD))
        new_ckv.append(ckvn.reshape(bp, sp, MLA_KV_LORA))
        new_kr.append(zp[:, C_MLA_KR:C_MLA_KR + MLA_ROPE].reshape(bp, sp, MLA_ROPE))
        new_sf.append(_bd_to_state(sf))
        new_sb.append(_bd_to_state(sb))

        zs = _in_proj(ys, mod3, gpre, w_packed, l, ss, 1)
        o_na = _sample_na(zs, cache_k, cache_v, _na_bias_table(na_rpb[l]), l, bs)
        kv_all, kr_all = _sample_mla_kv(zs, cache_mla_ckv, cache_kr_pad, cos_t, sin_t, g_kv, w_ukv, l, bs)
        o_mla = _sample_mla_attn(zs, cos_t, sin_t, g_q, wn, wr, wrp, kv_all, kr_all, bs)
        o_f, o_b, _, _ = _hgrn(zs, bs, lb_f_all[l], lb_b_all[l],
                               _state_to_bd(state_hgrn_fwd[:, l]), _state_to_bd(state_hgrn_bwd[:, l]))
        ys = _out_proj(ys, zs, mod3, gpost, ghg, o_na, o_mla, o_f, o_b, w_na, w_mla, w_hg, w_o, ss, 1)

    return (yp.reshape(bp, sp, D_MODEL), ys.reshape(bs, ss, D_MODEL),
            jnp.stack(new_k, axis=1), jnp.stack(new_v, axis=1), jnp.stack(new_ckv, axis=1),
            jnp.stack(new_kr, axis=1), jnp.stack(new_sf, axis=1), jnp.stack(new_sb, axis=1))
```

```python
import functools

import numpy as np
import jax
import jax.numpy as jnp
from jax import lax
from jax.experimental import pallas as pl
from jax.experimental.pallas import tpu as pltpu

D_MODEL = 2048
DEPTH = 2
GRID_W = 64
NORM_EPS = 1e-6
NEG_BIG = -1e30
NA_HEADS = 8
NA_HD = 64
NA_W = NA_HEADS * NA_HD
NA_KR = 8
NA_KW = 16
MLA_HEADS = 8
MLA_NOPE = 64
MLA_ROPE = 32
MLA_VD = 64
MLA_QK = MLA_NOPE + MLA_ROPE
MLA_W = MLA_HEADS * MLA_VD
MLA_Q_LORA = 512
MLA_KV_LORA = 256
ROPE_BASE = 10000.0
HG_HEADS = 8
HG_DK = 64
HG_DV = 64
HG_KW = HG_HEADS * HG_DK
HG_W = HG_HEADS * HG_DV
N_BRANCH = 3
PROJ_WIDTHS = (NA_W, NA_W, NA_W, NA_W, MLA_Q_LORA, MLA_KV_LORA, MLA_ROPE, MLA_W,
               HG_KW, HG_KW, HG_KW, HG_W, HG_W, N_BRANCH * D_MODEL)

F32 = jnp.float32
BF16 = jnp.bfloat16
HI = lax.Precision.HIGHEST

LANE = 128
ROW_BLK = 256
HG_CHUNK = 32
HG_SUB = 16
HG_SAFE_DECAY = 72.0
HG_EXP_CLAMP = 80.0
VMEM_LIMIT = 56 * 1024 * 1024

PACKED_W = 12288
HEAD_W = 3072
SPLIT_COL = 2848
C_NA_Q, C_NA_K, C_NA_V, C_NA_G = 0, 512, 1024, 1536
C_MLA_CQ, C_MLA_CKV, C_MLA_KR, C_MLA_KRP = 2048, 2560, 2816, 2944
C_MLA_G = 3072
C_HG_Q, C_HG_FF, C_HG_FB, C_HG_I, C_HG_G = 3584, 4096, 4608, 5120, 5632
C_MG = 6144


def _nt(a, b, precision=None):
    return lax.dot_general(a, b, (((1,), (1,)), ((), ())), preferred_element_type=F32, precision=precision)


def _tn(a, b, precision=None):
    return lax.dot_general(a, b, (((0,), (0,)), ((), ())), preferred_element_type=F32, precision=precision)


def _mm(a, b, precision=None):
    return jnp.dot(a, b, preferred_element_type=F32, precision=precision)


def _rms(x, g):
    return x * lax.rsqrt(jnp.mean(x * x, axis=-1, keepdims=True) + NORM_EPS) * g


def _silu(x):
    return x * jax.nn.sigmoid(x)


def _params(sem):
    return pltpu.CompilerParams(dimension_semantics=sem, vmem_limit_bytes=VMEM_LIMIT)


def _const_spec(shape):
    nd = len(shape)
    return pl.BlockSpec(shape, lambda *_: (0,) * nd)


def _rot_half(a):
    parts = []
    for ax in range(2):
        lo_, hi_ = a[..., ax * 16:ax * 16 + 8], a[..., ax * 16 + 8:ax * 16 + 16]
        parts += [-hi_, lo_]
    return jnp.concatenate(parts, axis=-1)


def _pack_w_in(w):
    assert sum(PROJ_WIDTHS[:7]) == SPLIT_COL and sum(PROJ_WIDTHS[7:]) == PACKED_W - HEAD_W
    head = w[..., :SPLIT_COL].astype(BF16)
    kr = head[..., C_MLA_KR:SPLIT_COL]
    pad = jnp.zeros(kr.shape[:-1] + (LANE - MLA_ROPE,), BF16)
    return jnp.concatenate([head, pad, _rot_half(kr), pad], axis=-1), w[..., SPLIT_COL:].astype(BF16)


def _pack_w_uq(w):
    w3 = w.reshape(MLA_Q_LORA, MLA_HEADS, MLA_QK)
    nope, rope = w3[..., :MLA_NOPE], w3[..., MLA_NOPE:]
    ropep = _rot_half(rope)

    def padh(a):
        z = jnp.zeros(a.shape[:-1] + (LANE - a.shape[-1],), a.dtype)
        return jnp.concatenate([a, z], axis=-1).reshape(MLA_Q_LORA, MLA_HEADS * LANE).astype(BF16)

    return padh(nope), padh(rope), padh(ropep)


def _rope_tables(n):
    t = np.arange(n)
    pos = np.stack([t // GRID_W, t % GRID_W]).astype(np.float32)
    axis_dim = MLA_ROPE // 2
    inv = ROPE_BASE ** (-jnp.arange(0, axis_dim, 2, dtype=F32) / axis_dim)
    ang = jnp.asarray(pos)[:, :, None] * inv
    ang = jnp.concatenate([ang, ang], axis=-1)
    ang = jnp.concatenate([ang[0], ang[1]], axis=-1)
    pad = jnp.zeros((n, LANE - MLA_ROPE), F32)
    return jnp.concatenate([jnp.cos(ang), pad], -1), jnp.concatenate([jnp.sin(ang), pad], -1)


def _na_bias_table(rpb):
    rows = 32
    c = np.arange(GRID_W)
    win0 = np.clip(c - NA_KW // 2, 0, GRID_W - NA_KW)
    kc = np.arange(GRID_W)
    col_ok = (kc[None, :] >= win0[:, None]) & (kc[None, :] < win0[:, None] + NA_KW)
    rpb = rpb.astype(F32)
    edge = GRID_W - NA_KW
    ext = jnp.concatenate([jnp.broadcast_to(rpb[..., :1], rpb.shape[:-1] + (edge,)), rpb,
                           jnp.broadcast_to(rpb[..., -1:], rpb.shape[:-1] + (edge,))], axis=-1)
    toep = jnp.stack([ext[..., GRID_W - 1 - ci:2 * GRID_W - 1 - ci] for ci in range(GRID_W)], axis=2)
    toep = jnp.where(jnp.asarray(col_ok), toep, NEG_BIG)
    masked = jnp.full((NA_HEADS, GRID_W, GRID_W), NEG_BIG, F32)
    out = []
    for start, r0 in ((0, 0), (4, 8), (20, 28)):
        per_q = []
        for qr in range(4):
            r = r0 + qr
            kr0 = min(max(r - NA_KR // 2, 0), rows - NA_KR)
            tiles = []
            for j in range(12):
                kabs = start + j
                ok = kr0 <= kabs < kr0 + NA_KR
                tiles.append(toep[:, kabs - r + NA_KR - 1] if ok else masked)
            per_q.append(jnp.concatenate(tiles, axis=-1))
        out.append(jnp.concatenate(per_q, axis=1))
    return jnp.stack(out)


def _lower_bounds(p):
    s = jax.nn.softmax(p.astype(F32), axis=0)
    return jnp.cumsum(s, axis=0) - s[0]


def _lb_logs(lb):
    pos = lb > 0
    log_lb = jnp.where(pos, jnp.log(jnp.where(pos, lb, 1.0)), NEG_BIG)
    return log_lb, jnp.log1p(-lb)


def _mod_kernel(c_ref, w_ref, b_ref, o_ref):
    s = _silu(c_ref[...])
    o_ref[0] = _mm(s, w_ref[0], HI) + b_ref[0]


def _modulation(cvec, w_ada, b_ada):
    tn = 1024
    n3 = 3 * D_MODEL
    return pl.pallas_call(
        _mod_kernel,
        out_shape=jax.ShapeDtypeStruct((DEPTH, 8, n3), F32),
        grid=(DEPTH, n3 // tn),
        in_specs=[pl.BlockSpec((8, D_MODEL), lambda l, j: (0, 0)),
                  pl.BlockSpec((1, D_MODEL, tn), lambda l, j: (l, 0, j)),
                  pl.BlockSpec((1, 1, tn), lambda l, j: (l, 0, j))],
        out_specs=pl.BlockSpec((1, 8, tn), lambda l, j: (l, 0, j)),
        compiler_params=_params(("arbitrary", "arbitrary")),
        name="adaln_mod",
    )(cvec, w_ada, b_ada.reshape(DEPTH, 1, n3))


IN_TM, IN_TN, IN_SLAB = 1024, 1024, 128
HEAD_TILES = HEAD_W // IN_TN


def _in_kernel(x_ref, mod_ref, g_ref, wh_ref, wt_ref, z_ref, h_scr):
    j = pl.program_id(1)

    @pl.when(j == 0)
    def _():
        def slab(s, carry):
            rows = pl.ds(pl.multiple_of(s * IN_SLAB, IN_SLAB), IN_SLAB)
            y = _rms(x_ref[rows, :], g_ref[...])
            h = y * (1.0 + mod_ref[0, 1:2, :]) + mod_ref[0, 0:1, :]
            h_scr[rows, :] = h.astype(BF16)
            return carry

        lax.fori_loop(0, IN_TM // IN_SLAB, slab, 0)

    @pl.when(j < HEAD_TILES)
    def _():
        z_ref[...] = _mm(h_scr[...], wh_ref[0])

    @pl.when(j >= HEAD_TILES)
    def _():
        z_ref[...] = _mm(h_scr[...], wt_ref[0])


def _in_proj(x, mod3, g_pre, w_head, w_tail, layer, rows_per_mod, mod_base):
    m = x.shape[0]
    tm, tn = IN_TM, IN_TN
    tiles_per_mod = rows_per_mod // tm
    return pl.pallas_call(
        _in_kernel,
        out_shape=jax.ShapeDtypeStruct((m, PACKED_W), F32),
        grid=(m // tm, PACKED_W // tn),
        in_specs=[pl.BlockSpec((tm, D_MODEL), lambda i, j: (i, 0)),
                  pl.BlockSpec((1, 3, D_MODEL), lambda i, j: (mod_base + i // tiles_per_mod, 0, 0)),
                  pl.BlockSpec((1, D_MODEL), lambda i, j: (0, 0)),
                  pl.BlockSpec((1, D_MODEL, tn), lambda i, j: (layer, 0, jnp.minimum(j, HEAD_TILES - 1))),
                  pl.BlockSpec((1, D_MODEL, tn), lambda i, j: (layer, 0, jnp.maximum(j - HEAD_TILES, 0)))],
        out_specs=pl.BlockSpec((tm, tn), lambda i, j: (i, j)),
        scratch_shapes=[pltpu.VMEM((tm, D_MODEL), BF16)],
        compiler_params=_params(("arbitrary", "arbitrary")),
        name="in_proj",
    )(x, mod3, g_pre, w_head, w_tail)


def _softmax_pv(s_list, v_list):
    m = s_list[0].max(axis=-1, keepdims=True)
    for s in s_list[1:]:
        m = jnp.maximum(m, s.max(axis=-1, keepdims=True))
    acc, den = None, None
    for s, v in zip(s_list, v_list):
        e = jnp.exp(s - m)
        d = e.sum(axis=-1, keepdims=True)
        o = _mm(e.astype(BF16), v)
        acc = o if acc is None else acc + o
        den = d if den is None else den + d
    return acc, den


def _mla_heads(qn, qr_fn, kv, krp, o_ref):
    scale = MLA_QK ** -0.5
    outs = []
    for h in range(MLA_HEADS):
        sl = slice(h * LANE, (h + 1) * LANE)
        kvh = kv[:, sl]
        s = (_nt(qn[:, sl], kvh) + _nt(qr_fn(sl), krp)) * scale
        acc, den = _softmax_pv([s], [kvh])
        outs.append(acc[:, MLA_NOPE:] / den)
    o_ref[...] = jnp.concatenate(outs, axis=-1)


def _prompt_attn_kernel(qkv_ref, cq_ref, ckv_ref, kr_ref, gq_ref, gkv_ref, wn_ref, wr_ref, wukv_ref,
                        ona_ref, omla_ref, ckvn_ref, k_ref, v_ref, kro_ref):
    qkv = qkv_ref[...]
    k_ref[...] = qkv[:, NA_W:2 * NA_W]
    v_ref[...] = qkv[:, 2 * NA_W:3 * NA_W]
    kro_ref[...] = kr_ref[:, 0:MLA_ROPE]
    scale = NA_HD ** -0.5
    outs = []
    for h in range(NA_HEADS):
        qh = qkv[:, h * NA_HD:(h + 1) * NA_HD].astype(BF16)
        kh = qkv[:, NA_W + h * NA_HD:NA_W + (h + 1) * NA_HD].astype(BF16)
        vh = qkv[:, 2 * NA_W + h * NA_HD:2 * NA_W + (h + 1) * NA_HD].astype(BF16)
        acc, den = _softmax_pv([_nt(qh, kh) * scale], [vh])
        outs.append(acc / den)
    ona_ref[...] = jnp.concatenate(outs, axis=-1)

    cqn = _rms(cq_ref[...], gq_ref[...]).astype(BF16)
    qn = _mm(cqn, wn_ref[...]).astype(BF16)
    qr = _mm(cqn, wr_ref[...]).astype(BF16)
    ckvn = _rms(ckv_ref[...], gkv_ref[...])
    ckvn_ref[...] = ckvn
    kv = _mm(ckvn.astype(BF16), wukv_ref[...]).astype(BF16)
    krp = kr_ref[...].astype(BF16)
    _mla_heads(qn, lambda sl: qr[:, sl], kv, krp, omla_ref)


def _prompt_attn(z, n_batch, g_q, g_kv, wn, wr, w_ukv):
    m = z.shape[0]
    t = ROW_BLK
    hw = MLA_HEADS * LANE
    return pl.pallas_call(
        _prompt_attn_kernel,
        out_shape=(jax.ShapeDtypeStruct((m, NA_W), F32), jax.ShapeDtypeStruct((m, MLA_W), F32),
                   jax.ShapeDtypeStruct((m, MLA_KV_LORA), F32), jax.ShapeDtypeStruct((m, NA_W), F32),
                   jax.ShapeDtypeStruct((m, NA_W), F32), jax.ShapeDtypeStruct((m, MLA_ROPE), F32)),
        grid=(n_batch,),
        in_specs=[pl.BlockSpec((t, 3 * NA_W), lambda b: (b, C_NA_Q // (3 * NA_W))),
                  pl.BlockSpec((t, MLA_Q_LORA), lambda b: (b, C_MLA_CQ // MLA_Q_LORA)),
                  pl.BlockSpec((t, MLA_KV_LORA), lambda b: (b, C_MLA_CKV // MLA_KV_LORA)),
                  pl.BlockSpec((t, LANE), lambda b: (b, C_MLA_KR // LANE)),
                  _const_spec((1, MLA_Q_LORA)), _const_spec((1, MLA_KV_LORA)),
                  _const_spec((MLA_Q_LORA, hw)), _const_spec((MLA_Q_LORA, hw)), _const_spec((MLA_KV_LORA, hw))],
        out_specs=(pl.BlockSpec((t, NA_W), lambda b: (b, 0)), pl.BlockSpec((t, MLA_W), lambda b: (b, 0)),
                   pl.BlockSpec((t, MLA_KV_LORA), lambda b: (b, 0)), pl.BlockSpec((t, NA_W), lambda b: (b, 0)),
                   pl.BlockSpec((t, NA_W), lambda b: (b, 0)), pl.BlockSpec((t, MLA_ROPE), lambda b: (b, 0))),
        compiler_params=_params(("arbitrary",)),
        name="prompt_attn",
    )(z, z, z, z, g_q, g_kv, wn, wr, w_ukv)


def _sample_na_kernel(q_ref, k0_ref, k1_ref, k2_ref, v0_ref, v1_ref, v2_ref, kc_ref, vc_ref, bias_ref, o_ref):
    scale = NA_HD ** -0.5
    q = q_ref[...]
    kl = [r[...] for r in (k0_ref, k1_ref, k2_ref)]
    vl = [r[...] for r in (v0_ref, v1_ref, v2_ref)]
    kc, vc = kc_ref[0, 0], vc_ref[0, 0]
    outs = []
    for h in range(NA_HEADS):
        sl = slice(h * NA_HD, (h + 1) * NA_HD)
        qh = q[:, sl].astype(BF16)
        s_list = [_nt(qh, kl[i][:, sl].astype(BF16)) * scale + bias_ref[0, h, :, i * ROW_BLK:(i + 1) * ROW_BLK]
                  for i in range(3)]
        s_list.append(_nt(qh, kc[:, sl].astype(BF16)) * scale)
        v_list = [vl[i][:, sl].astype(BF16) for i in range(3)] + [vc[:, sl].astype(BF16)]
        acc, den = _softmax_pv(s_list, v_list)
        outs.append(acc / den)
    o_ref[...] = jnp.concatenate(outs, axis=-1)


def _sample_na(z, cache_k, cache_v, bias_tab, layer, n_batch):
    m = z.shape[0]
    t = ROW_BLK
    nblk = m // n_batch // t
    past = cache_k.shape[2]

    def kv_map(col, i):
        return lambda b, rb: (b * nblk + jnp.clip(rb - 1, 0, nblk - 3) + i, col // NA_W)

    def variant(b, rb):
        return (jnp.where(rb == 0, 0, jnp.where(rb == nblk - 1, 2, 1)), 0, 0, 0)

    cache_spec = pl.BlockSpec((1, 1, past, NA_W), lambda b, rb: (b, layer, 0, 0))
    return pl.pallas_call(
        _sample_na_kernel,
        out_shape=jax.ShapeDtypeStruct((m, NA_W), F32),
        grid=(n_batch, nblk),
        in_specs=[pl.BlockSpec((t, NA_W), lambda b, rb: (b * nblk + rb, C_NA_Q // NA_W))]
                 + [pl.BlockSpec((t, NA_W), kv_map(C_NA_K, i)) for i in range(3)]
                 + [pl.BlockSpec((t, NA_W), kv_map(C_NA_V, i)) for i in range(3)]
                 + [cache_spec, cache_spec,
                    pl.BlockSpec((1, NA_HEADS, t, 3 * t), variant)],
        out_specs=pl.BlockSpec((t, NA_W), lambda b, rb: (b * nblk + rb, 0)),
        compiler_params=_params(("arbitrary", "arbitrary")),
        name="sample_na",
    )(z, z, z, z, z, z, z, cache_k, cache_v, bias_tab)


def _sample_mla_kv_kernel(cckv_ref, ckr_ref, ckv_ref, kr_ref, krp_ref, cos_ref, sin_ref, gkv_ref, wukv_ref,
                          kv_ref, kro_ref):
    j = pl.program_id(1)

    @pl.when(j == 0)
    def _():
        kv_ref[0] = _mm(cckv_ref[0, 0].astype(BF16), wukv_ref[...]).astype(BF16)
        kro_ref[0] = ckr_ref[0, 0].astype(BF16)

    @pl.when(j > 0)
    def _():
        ckvn = _rms(ckv_ref[...], gkv_ref[...])
        kv_ref[0] = _mm(ckvn.astype(BF16), wukv_ref[...]).astype(BF16)
        kro_ref[0] = (kr_ref[...] * cos_ref[...] + krp_ref[...] * sin_ref[...]).astype(BF16)


def _sample_mla_kv(z, cache_ckv, cache_kr_pad, cos_t, sin_t, g_kv, w_ukv, layer, n_batch):
    m = z.shape[0]
    n = m // n_batch
    past = cache_ckv.shape[2]
    t = past
    nb = n // t
    hw = MLA_HEADS * LANE

    def zrow(b, j):
        return b * nb + jnp.maximum(j - 1, 0)

    return pl.pallas_call(
        _sample_mla_kv_kernel,
        out_shape=(jax.ShapeDtypeStruct((n_batch, past + n, hw), BF16),
                   jax.ShapeDtypeStruct((n_batch, past + n, LANE), BF16)),
        grid=(n_batch, nb + 1),
        in_specs=[pl.BlockSpec((1, 1, past, MLA_KV_LORA), lambda b, j: (b, layer, 0, 0)),
                  pl.BlockSpec((1, 1, past, LANE), lambda b, j: (b, layer, 0, 0)),
                  pl.BlockSpec((t, MLA_KV_LORA), lambda b, j: (zrow(b, j), C_MLA_CKV // MLA_KV_LORA)),
                  pl.BlockSpec((t, LANE), lambda b, j: (zrow(b, j), C_MLA_KR // LANE)),
                  pl.BlockSpec((t, LANE), lambda b, j: (zrow(b, j), C_MLA_KRP // LANE)),
                  pl.BlockSpec((t, LANE), lambda b, j: (jnp.maximum(j - 1, 0), 0)),
                  pl.BlockSpec((t, LANE), lambda b, j: (jnp.maximum(j - 1, 0), 0)),
                  _const_spec((1, MLA_KV_LORA)), _const_spec((MLA_KV_LORA, hw))],
        out_specs=(pl.BlockSpec((1, t, hw), lambda b, j: (b, j, 0)),
                   pl.BlockSpec((1, t, LANE), lambda b, j: (b, j, 0))),
        compiler_params=_params(("arbitrary", "arbitrary")),
        name="sample_mla_kv",
    )(cache_ckv, cache_kr_pad, z, z, z, cos_t, sin_t, g_kv, w_ukv)


def _sample_mla_attn_kernel(cq_ref, cos_ref, sin_ref, gq_ref, wn_ref, wr_ref, wrp_ref, kv_ref, kr_ref, o_ref):
    cqn = _rms(cq_ref[...], gq_ref[...]).astype(BF16)
    qn = _mm(cqn, wn_ref[...]).astype(BF16)
    qr = _mm(cqn, wr_ref[...])
    qrp = _mm(cqn, wrp_ref[...])
    cos, sin = cos_ref[...], sin_ref[...]
    _mla_heads(qn, lambda sl: (qr[:, sl] * cos + qrp[:, sl] * sin).astype(BF16), kv_ref[0], kr_ref[0], o_ref)


def _sample_mla_attn(z, cos_t, sin_t, g_q, wn, wr, wrp, kv_all, kr_all, n_batch):
    m = z.shape[0]
    t = ROW_BLK
    nblk = m // n_batch // t
    nk = kv_all.shape[1]
    hw = MLA_HEADS * LANE
    return pl.pallas_call(
        _sample_mla_attn_kernel,
        out_shape=jax.ShapeDtypeStruct((m, MLA_W), F32),
        grid=(n_batch, nblk),
        in_specs=[pl.BlockSpec((t, MLA_Q_LORA), lambda b, i: (b * nblk + i, C_MLA_CQ // MLA_Q_LORA)),
                  pl.BlockSpec((t, LANE), lambda b, i: (i, 0)),
                  pl.BlockSpec((t, LANE), lambda b, i: (i, 0)),
                  _const_spec((1, MLA_Q_LORA)),
                  _const_spec((MLA_Q_LORA, hw)), _const_spec((MLA_Q_LORA, hw)), _const_spec((MLA_Q_LORA, hw)),
                  pl.BlockSpec((1, nk, hw), lambda b, i: (b, 0, 0)),
                  pl.BlockSpec((1, nk, LANE), lambda b, i: (b, 0, 0))],
        out_specs=pl.BlockSpec((t, MLA_W), lambda b, i: (b * nblk + i, 0)),
        compiler_params=_params(("arbitrary", "arbitrary")),
        name="sample_mla_attn",
    )(z, cos_t, sin_t, g_q, wn, wr, wrp, kv_all, kr_all)


def _hg_direction(fwd, q_ref, zf_ref, v_ref, loglb_ref, l1m_ref, s_ref, o_ref, qs, ks, bs, vs):
    t = ROW_BLK
    nchunk = t // HG_CHUNK
    q = _silu(q_ref[...])
    zf = zf_ref[...]
    v = v_ref[...]
    soft = jnp.log1p(jnp.exp(-jnp.abs(zf)))
    a1 = jnp.broadcast_to(loglb_ref[...], zf.shape)
    a2 = l1m_ref[...] + (jnp.minimum(zf, 0.0) - soft)
    logf = jnp.maximum(a1, a2) + jnp.log1p(jnp.exp(-jnp.abs(a1 - a2)))
    k = jnp.exp(l1m_ref[...] - jnp.maximum(zf, 0.0) - soft)

    ri = lax.broadcasted_iota(jnp.int32, (t, t), 0)
    ci = lax.broadcasted_iota(jnp.int32, (t, t), 1)
    causal = (ri >= ci) if fwd else (ri <= ci)
    same_sub = (ri // HG_SUB) == (ci // HG_SUB)
    same = (ri // HG_CHUNK) == (ci // HG_CHUNK)
    tri_sub = same_sub & causal
    cross = same & jnp.logical_not(same_sub) & causal
    hi = logf.astype(BF16)
    lo = (logf - hi.astype(F32)).astype(BF16)

    def seg_sum(mask):
        m16 = jnp.where(mask, 1.0, 0.0).astype(BF16)
        return _mm(m16, hi) + _mm(m16, lo)

    b_sub = seg_sum(tri_sub)
    tot_sub = seg_sum(same_sub)
    tot = seg_sum(same)
    row = lax.broadcasted_iota(jnp.int32, (t, 1), 0) % HG_CHUNK
    later = (row >= HG_SUB) if fwd else (row < HG_SUB)
    b = b_sub + jnp.where(later, tot - tot_sub, 0.0)

    qt_sub = (q * jnp.exp(b_sub)).astype(BF16)
    kt_sub = (k * jnp.exp(jnp.minimum(-b_sub, HG_EXP_CLAMP))).astype(BF16)
    kh_sub = (k * jnp.exp(tot_sub - b_sub)).astype(BF16)
    qt16 = (q * jnp.exp(b)).astype(BF16)
    kh16 = (k * jnp.exp(tot - b)).astype(BF16)
    v16 = v.astype(BF16)

    outs = []
    for h in range(HG_HEADS):
        sl = slice(h * HG_DK, (h + 1) * HG_DK)
        a = (jnp.where(tri_sub, _nt(qt_sub[:, sl], kt_sub[:, sl]), 0.0)
             + jnp.where(cross, _nt(qt_sub[:, sl], kh_sub[:, sl]), 0.0))
        outs.append(_mm(a.astype(BF16), v16[:, sl]))
    o_ref[...] = jnp.concatenate(outs, axis=-1)

    qs[...] = q
    ks[...] = k
    bs[...] = b
    vs[...] = v
    lane = lax.broadcasted_iota(jnp.int32, (HG_KW, LANE), 0) // HG_DK
    col = lax.broadcasted_iota(jnp.int32, (HG_KW, LANE), 1)
    head_sum = (lane == col).astype(F32)
    srow = lax.broadcasted_iota(jnp.int32, (HG_CHUNK, 1), 0)
    for c in range(nchunk):
        r0 = c * HG_CHUNK
        worst = jnp.max(-jnp.minimum(tot_sub[r0:r0 + 1, :], tot_sub[r0 + HG_SUB:r0 + HG_SUB + 1, :]))

        @pl.when(worst > HG_SAFE_DECAY)
        def _():
            kc = ks[r0:r0 + HG_CHUNK, :]
            bc = bs[r0:r0 + HG_CHUNK, :]
            vc = vs[r0:r0 + HG_CHUNK, :]

            def body(i, carry):
                qrow = qs[pl.ds(r0 + i, 1), :]
                brow = bs[pl.ds(r0 + i, 1), :]
                p = qrow * kc * jnp.exp(jnp.minimum(brow - bc, 0.0))
                keep = (srow <= i) if fwd else (srow >= i)
                p = jnp.where(keep, p, 0.0)
                a = _mm(p, head_sum, HI)
                a_full = _nt(a, head_sum, HI)
                o_ref[pl.ds(r0 + i, 1), :] = jnp.sum(a_full * vc, axis=0, keepdims=True)
                return carry

            lax.fori_loop(0, HG_CHUNK, body, 0)

    gw = 4 * HG_DK
    bd_r = lax.broadcasted_iota(jnp.int32, (gw, gw), 0) // HG_DV
    bd_c = lax.broadcasted_iota(jnp.int32, (gw, gw), 1) // HG_DK
    bd = bd_r == bd_c
    order = range(nchunk) if fwd else range(nchunk - 1, -1, -1)
    for g in range(2):
        ls = slice(g * gw, (g + 1) * gw)
        s = s_ref[g]
        for c in order:
            rs = slice(c * HG_CHUNK, (c + 1) * HG_CHUNK)
            o_ref[rs, ls] += _nt(qt16[rs, ls], s.astype(BF16))
            u = _tn(v16[rs, ls], kh16[rs, ls])
            s = s * jnp.exp(tot[c * HG_CHUNK:c * HG_CHUNK + 1, ls]) + jnp.where(bd, u, 0.0)
        s_ref[g] = s


def _head_block(h):
    g, hh = divmod(h, 4)
    return g, slice(hh * HG_DV, (hh + 1) * HG_DV), slice(hh * HG_DK, (hh + 1) * HG_DK)


def _hgrn_kernel(*refs, has_state):
    (qf_ref, ff_ref, vf_ref, qb_ref, fb_ref, vb_ref, lbf_ref, l1f_ref, lbb_ref, l1b_ref) = refs[:10]
    s0_refs = refs[10:12] if has_state else (None, None)
    of_ref, ob_ref, sf_out, sb_out, sf, sb, qs, ks, bs, vs = refs[12:] if has_state else refs[10:]
    i = pl.program_id(1)

    @pl.when(i == 0)
    def _():
        for scr, s0_ref in zip((sf, sb), s0_refs):
            scr[...] = jnp.zeros(scr.shape, F32)
            if s0_ref is not None:
                for h in range(HG_HEADS):
                    g, rv, ck = _head_block(h)
                    scr[g, rv, ck] = s0_ref[0, 0, h].T

    _hg_direction(True, qf_ref, ff_ref, vf_ref, lbf_ref, l1f_ref, sf, of_ref, qs, ks, bs, vs)
    _hg_direction(False, qb_ref, fb_ref, vb_ref, lbb_ref, l1b_ref, sb, ob_ref, qs, ks, bs, vs)

    @pl.when(i == pl.num_programs(1) - 1)
    def _():
        for scr, out in ((sf, sf_out), (sb, sb_out)):
            for h in range(HG_HEADS):
                g, rv, ck = _head_block(h)
                out[0, h] = scr[g, rv, ck].T


def _hgrn(z, n_batch, lb_f, lb_b, s0f=None, s0b=None, layer=0):
    m = z.shape[0]
    t = ROW_BLK
    nblk = m // n_batch // t
    has_state = s0f is not None
    loglb_f, l1m_f = _lb_logs(lb_f)
    loglb_b, l1m_b = _lb_logs(lb_b)
    row = lambda a: a.reshape(1, HG_KW)

    def fmap(col):
        return lambda b, i: (b * nblk + i, col // HG_KW)

    def bmap(col):
        return lambda b, i: (b * nblk + nblk - 1 - i, col // HG_KW)

    blk = lambda imap: pl.BlockSpec((t, HG_KW), imap)
    st_out = pl.BlockSpec((1, HG_HEADS, HG_DK, HG_DV), lambda b, i: (b, 0, 0, 0))
    st_shape = jax.ShapeDtypeStruct((n_batch, HG_HEADS, HG_DK, HG_DV), F32)
    st_in = pl.BlockSpec((1, 1, HG_HEADS, HG_DK, HG_DV), lambda b, i: (b, layer, 0, 0, 0))
    bd_scratch = pltpu.VMEM((2, 4 * HG_DV, 4 * HG_DK), F32)
    return pl.pallas_call(
        functools.partial(_hgrn_kernel, has_state=has_state),
        out_shape=(jax.ShapeDtypeStruct((m, HG_W), F32), jax.ShapeDtypeStruct((m, HG_W), F32), st_shape, st_shape),
        grid=(n_batch, nblk),
        in_specs=[blk(fmap(C_HG_Q)), blk(fmap(C_HG_FF)), blk(fmap(C_HG_I)),
                  blk(bmap(C_HG_Q)), blk(bmap(C_HG_FB)), blk(bmap(C_HG_I)),
                  _const_spec((1, HG_KW)), _const_spec((1, HG_KW)), _const_spec((1, HG_KW)), _const_spec((1, HG_KW))]
                 + ([st_in, st_in] if has_state else []),
        out_specs=(pl.BlockSpec((t, HG_W), lambda b, i: (b * nblk + i, 0)),
                   pl.BlockSpec((t, HG_W), lambda b, i: (b * nblk + nblk - 1 - i, 0)),
                   st_out, st_out),
        scratch_shapes=[bd_scratch, bd_scratch] + [pltpu.VMEM((t, HG_KW), F32)] * 4,
        compiler_params=_params(("arbitrary", "arbitrary")),
        name="hgrn_scan",
    )(z, z, z, z, z, z, row(loglb_f), row(l1m_f), row(loglb_b), row(l1m_b), *((s0f, s0b) if has_state else ()))


def _out_kernel(x_ref, mod_ref, gpost_ref, ghg_ref, ona_ref, omla_ref, of_ref, ob_ref,
                gna_ref, gmla_ref, ghgate_ref, mg_ref, wna_ref, wmla_ref, whg_ref, wout_ref, y_ref):
    o = of_ref[...] + ob_ref[...]
    hr = lax.broadcasted_iota(jnp.int32, (HG_W, HG_W), 0) // HG_DV
    hc = lax.broadcasted_iota(jnp.int32, (HG_W, HG_W), 1) // HG_DV
    head_mean = jnp.where(hr == hc, 1.0 / HG_DV, 0.0).astype(F32)
    ms = _mm(o * o, head_mean, HI)
    o_hg = o * lax.rsqrt(ms + NORM_EPS) * ghg_ref[...]

    def branch(o_b, gate_ref, w_ref):
        return _mm((o_b * _silu(gate_ref[...])).astype(BF16), w_ref[...])

    merged = (jax.nn.sigmoid(mg_ref[:, 0:D_MODEL]) * branch(ona_ref[...], gna_ref, wna_ref)
              + jax.nn.sigmoid(mg_ref[:, D_MODEL:2 * D_MODEL]) * branch(omla_ref[...], gmla_ref, wmla_ref)
              + jax.nn.sigmoid(mg_ref[:, 2 * D_MODEL:3 * D_MODEL]) * branch(o_hg, ghgate_ref, whg_ref))
    out = _mm(merged.astype(BF16), wout_ref[...])
    y_ref[...] = x_ref[...] + mod_ref[0, 2:3, :] * _rms(out, gpost_ref[...])


def _out_proj(x, z, mod3, g_post, g_hg, o_na, o_mla, o_f, o_b, w_na, w_mla, w_hg, w_out, rows_per_mod, mod_base):
    m = x.shape[0]
    t = ROW_BLK
    tiles_per_mod = rows_per_mod // t
    w512 = lambda: pl.BlockSpec((t, NA_W), lambda i: (i, 0))
    zcol = lambda col, w: pl.BlockSpec((t, w), lambda i: (i, col // w))
    one = pl.Buffered(1)
    return pl.pallas_call(
        _out_kernel,
        out_shape=jax.ShapeDtypeStruct((m, D_MODEL), F32),
        grid=(m // t,),
        in_specs=[pl.BlockSpec((t, D_MODEL), lambda i: (i, 0)),
                  pl.BlockSpec((1, 3, D_MODEL), lambda i: (mod_base + i // tiles_per_mod, 0, 0)),
                  _const_spec((1, D_MODEL)), _const_spec((1, HG_W)),
                  w512(), w512(), w512(), w512(),
                  zcol(C_NA_G, NA_W), zcol(C_MLA_G, MLA_W), zcol(C_HG_G, HG_W),
                  zcol(C_MG, N_BRANCH * D_MODEL),
                  pl.BlockSpec((NA_W, D_MODEL), lambda i: (0, 0), pipeline_mode=one),
                  pl.BlockSpec((MLA_W, D_MODEL), lambda i: (0, 0), pipeline_mode=one),
                  pl.BlockSpec((HG_W, D_MODEL), lambda i: (0, 0), pipeline_mode=one),
                  pl.BlockSpec((D_MODEL, D_MODEL), lambda i: (0, 0), pipeline_mode=one)],
        out_specs=pl.BlockSpec((t, D_MODEL), lambda i: (i, 0)),
        compiler_params=_params(("arbitrary",)),
        name="out_proj",
    )(x, mod3, g_post, g_hg, o_na, o_mla, o_f, o_b, z, z, z, z, w_na, w_mla, w_hg, w_out)


def kernel(x_prompt, x_sample, cache_na_k, cache_na_v, cache_mla_ckv, cache_mla_krope, state_hgrn_fwd, state_hgrn_bwd, c, c_ctx, w_ada, b_ada, g_pre, g_post, w_in, na_rpb, g_mla_q, w_mla_uq, g_mla_kv, w_mla_ukv, hg_lb_fwd, hg_lb_bwd, g_hg_out, w_br_na, w_br_mla, w_br_hg, w_out):
    bp, sp, _ = x_prompt.shape
    bs, ss, _ = x_sample.shape
    past = cache_na_k.shape[2]

    cvec = jnp.concatenate([c_ctx[None, :], c, jnp.zeros((8 - 1 - bs, D_MODEL), F32)], axis=0)
    mod = _modulation(cvec, w_ada, b_ada).reshape(DEPTH, 8, 3, D_MODEL)

    lb_f_all = _lower_bounds(hg_lb_fwd)
    lb_b_all = _lower_bounds(hg_lb_bwd)
    cos_t, sin_t = _rope_tables(ss)
    cache_k = cache_na_k.reshape(bs, DEPTH, past, NA_W)
    cache_v = cache_na_v.reshape(bs, DEPTH, past, NA_W)
    cache_kr_pad = jnp.pad(cache_mla_krope, ((0, 0), (0, 0), (0, 0), (0, LANE - MLA_ROPE)))

    yp = x_prompt.reshape(bp * sp, D_MODEL)
    ys = x_sample.reshape(bs * ss, D_MODEL)
    new_k, new_v, new_ckv, new_kr, new_sf, new_sb = [], [], [], [], [], []
    w_head, w_tail = _pack_w_in(w_in)
    for l in range(DEPTH):
        wn, wr, wrp = _pack_w_uq(w_mla_uq[l])
        w_ukv = w_mla_ukv[l].astype(BF16)
        w_na, w_mla, w_hg, w_o = (w.astype(BF16) for w in (w_br_na[l], w_br_mla[l], w_br_hg[l], w_out[l]))
        g_q, g_kv = g_mla_q[l][None, :], g_mla_kv[l][None, :]
        gpre, gpost, ghg = g_pre[l][None, :], g_post[l][None, :], g_hg_out[l][None, :]
        mod3 = mod[l]

        zp = _in_proj(yp, mod3, gpre, w_head, w_tail, l, bp * sp, 0)
        o_na, o_mla, ckvn, k_new, v_new, kr_new = _prompt_attn(zp, bp, g_q, g_kv, wn, wr, w_ukv)
        o_f, o_b, sf, sb = _hgrn(zp, bp, lb_f_all[l], lb_b_all[l])
        yp = _out_proj(yp, zp, mod3, gpost, ghg, o_na, o_mla, o_f, o_b, w_na, w_mla, w_hg, w_o, bp * sp, 0)
        new_k.append(k_new.reshape(bp, sp, NA_HEADS, NA_HD))
        new_v.append(v_new.reshape(bp, sp, NA_HEADS, NA_HD))
        new_ckv.append(ckvn.reshape(bp, sp, MLA_KV_LORA))
        new_kr.append(kr_new.reshape(bp, sp, MLA_ROPE))
        new_sf.append(sf)
        new_sb.append(sb)

        zs = _in_proj(ys, mod3, gpre, w_head, w_tail, l, ss, 1)
        o_na = _sample_na(zs, cache_k, cache_v, _na_bias_table(na_rpb[l]), l, bs)
        kv_all, kr_all = _sample_mla_kv(zs, cache_mla_ckv, cache_kr_pad, cos_t, sin_t, g_kv, w_ukv, l, bs)
        o_mla = _sample_mla_attn(zs, cos_t, sin_t, g_q, wn, wr, wrp, kv_all, kr_all, bs)
        o_f, o_b, _, _ = _hgrn(zs, bs, lb_f_all[l], lb_b_all[l], state_hgrn_fwd, state_hgrn_bwd, l)
        ys = _out_proj(ys, zs, mod3, gpost, ghg, o_na, o_mla, o_f, o_b, w_na, w_mla, w_hg, w_o, ss, 1)

    return (yp.reshape(bp, sp, D_MODEL), ys.reshape(bs, ss, D_MODEL),
            jnp.stack(new_k, axis=1), jnp.stack(new_v, axis=1), jnp.stack(new_ckv, axis=1),
            jnp.stack(new_kr, axis=1), jnp.stack(new_sf, axis=1), jnp.stack(new_sb, axis=1))
```

```python
import functools

import numpy as np
import jax
import jax.numpy as jnp
from jax import lax
from jax.experimental import pallas as pl
from jax.experimental.pallas import tpu as pltpu

D_MODEL = 2048
DEPTH = 2
GRID_W = 64
NORM_EPS = 1e-6
NEG_BIG = -1e30
NA_HEADS = 8
NA_HD = 64
NA_W = NA_HEADS * NA_HD
NA_KR = 8
NA_KW = 16
MLA_HEADS = 8
MLA_NOPE = 64
MLA_ROPE = 32
MLA_VD = 64
MLA_QK = MLA_NOPE + MLA_ROPE
MLA_W = MLA_HEADS * MLA_VD
MLA_Q_LORA = 512
MLA_KV_LORA = 256
ROPE_BASE = 10000.0
HG_HEADS = 8
HG_DK = 64
HG_DV = 64
HG_KW = HG_HEADS * HG_DK
HG_W = HG_HEADS * HG_DV
N_BRANCH = 3
PROJ_WIDTHS = (NA_W, NA_W, NA_W, NA_W, MLA_Q_LORA, MLA_KV_LORA, MLA_ROPE, MLA_W,
               HG_KW, HG_KW, HG_KW, HG_W, HG_W, N_BRANCH * D_MODEL)

F32 = jnp.float32
BF16 = jnp.bfloat16
HI = lax.Precision.HIGHEST

LANE = 128
ROW_BLK = 256
HG_CHUNK = 32
HG_SUB = 16
HG_SAFE_DECAY = 72.0
HG_EXP_CLAMP = 80.0
VMEM_LIMIT = 56 * 1024 * 1024

HEAD_W = 3072
MID_W = 3072
MG_W = N_BRANCH * D_MODEL
SPLIT_COL = 2848
C_NA_Q, C_NA_K, C_NA_V, C_NA_G = 0, 512, 1024, 1536
C_MLA_CQ, C_MLA_CKV, C_MLA_KR, C_MLA_KRP = 2048, 2560, 2816, 2944
M_MLA_G, M_HG_Q, M_HG_FF, M_HG_FB, M_HG_I, M_HG_G = 0, 512, 1024, 1536, 2048, 2560


def _nt(a, b, precision=None):
    return lax.dot_general(a, b, (((1,), (1,)), ((), ())), preferred_element_type=F32, precision=precision)


def _tn(a, b, precision=None):
    return lax.dot_general(a, b, (((0,), (0,)), ((), ())), preferred_element_type=F32, precision=precision)


def _mm(a, b, precision=None):
    return jnp.dot(a, b, preferred_element_type=F32, precision=precision)


def _rms(x, g):
    return x * lax.rsqrt(jnp.mean(x * x, axis=-1, keepdims=True) + NORM_EPS) * g


def _silu(x):
    return x * jax.nn.sigmoid(x)


def _params(sem):
    return pltpu.CompilerParams(dimension_semantics=sem, vmem_limit_bytes=VMEM_LIMIT)


def _const_spec(shape):
    nd = len(shape)
    return pl.BlockSpec(shape, lambda *_: (0,) * nd)


def _rot_half(a):
    parts = []
    for ax in range(2):
        lo_, hi_ = a[..., ax * 16:ax * 16 + 8], a[..., ax * 16 + 8:ax * 16 + 16]
        parts += [-hi_, lo_]
    return jnp.concatenate(parts, axis=-1)


def _pack_w_in(w):
    assert sum(PROJ_WIDTHS[:7]) == SPLIT_COL and sum(PROJ_WIDTHS[7:13]) == MID_W and PROJ_WIDTHS[13] == MG_W
    head = w[..., :SPLIT_COL].astype(BF16)
    kr = head[..., C_MLA_KR:SPLIT_COL]
    pad = jnp.zeros(kr.shape[:-1] + (LANE - MLA_ROPE,), BF16)
    return jnp.concatenate([head, pad, _rot_half(kr), pad], axis=-1), w[..., SPLIT_COL:].astype(BF16)


def _pack_w_uq(w):
    w3 = w.reshape(MLA_Q_LORA, MLA_HEADS, MLA_QK)
    nope, rope = w3[..., :MLA_NOPE], w3[..., MLA_NOPE:]
    ropep = _rot_half(rope)

    def padh(a):
        z = jnp.zeros(a.shape[:-1] + (LANE - a.shape[-1],), a.dtype)
        return jnp.concatenate([a, z], axis=-1).reshape(MLA_Q_LORA, MLA_HEADS * LANE).astype(BF16)

    return padh(nope), padh(rope), padh(ropep)


def _rope_tables(n):
    t = np.arange(n)
    pos = np.stack([t // GRID_W, t % GRID_W]).astype(np.float32)
    axis_dim = MLA_ROPE // 2
    inv = ROPE_BASE ** (-jnp.arange(0, axis_dim, 2, dtype=F32) / axis_dim)
    ang = jnp.asarray(pos)[:, :, None] * inv
    ang = jnp.concatenate([ang, ang], axis=-1)
    ang = jnp.concatenate([ang[0], ang[1]], axis=-1)
    pad = jnp.zeros((n, LANE - MLA_ROPE), F32)
    return jnp.concatenate([jnp.cos(ang), pad], -1), jnp.concatenate([jnp.sin(ang), pad], -1)


def _na_bias_table(rpb):
    rows = 32
    c = np.arange(GRID_W)
    win0 = np.clip(c - NA_KW // 2, 0, GRID_W - NA_KW)
    kc = np.arange(GRID_W)
    col_ok = (kc[None, :] >= win0[:, None]) & (kc[None, :] < win0[:, None] + NA_KW)
    rpb = rpb.astype(F32)
    edge = GRID_W - NA_KW
    ext = jnp.concatenate([jnp.broadcast_to(rpb[..., :1], rpb.shape[:-1] + (edge,)), rpb,
                           jnp.broadcast_to(rpb[..., -1:], rpb.shape[:-1] + (edge,))], axis=-1)
    toep = jnp.stack([ext[..., GRID_W - 1 - ci:2 * GRID_W - 1 - ci] for ci in range(GRID_W)], axis=2)
    toep = jnp.where(jnp.asarray(col_ok), toep, NEG_BIG)
    masked = jnp.full((NA_HEADS, GRID_W, GRID_W), NEG_BIG, F32)
    out = []
    for start, r0 in ((0, 0), (4, 8), (20, 28)):
        per_q = []
        for qr in range(4):
            r = r0 + qr
            kr0 = min(max(r - NA_KR // 2, 0), rows - NA_KR)
            tiles = []
            for j in range(12):
                kabs = start + j
                ok = kr0 <= kabs < kr0 + NA_KR
                tiles.append(toep[:, kabs - r + NA_KR - 1] if ok else masked)
            per_q.append(jnp.concatenate(tiles, axis=-1))
        out.append(jnp.concatenate(per_q, axis=1))
    return jnp.stack(out)


def _lower_bounds(p):
    s = jax.nn.softmax(p.astype(F32), axis=0)
    return jnp.cumsum(s, axis=0) - s[0]


def _lb_logs(lb):
    pos = lb > 0
    log_lb = jnp.where(pos, jnp.log(jnp.where(pos, lb, 1.0)), NEG_BIG)
    return log_lb, jnp.log1p(-lb)


def _mod_kernel(c_ref, w_ref, b_ref, o_ref):
    s = _silu(c_ref[...])
    o_ref[0] = _mm(s, w_ref[0], HI) + b_ref[0]


def _modulation(cvec, w_ada, b_ada):
    tn = 1024
    n3 = 3 * D_MODEL
    return pl.pallas_call(
        _mod_kernel,
        out_shape=jax.ShapeDtypeStruct((DEPTH, 8, n3), F32),
        grid=(DEPTH, n3 // tn),
        in_specs=[pl.BlockSpec((8, D_MODEL), lambda l, j: (0, 0)),
                  pl.BlockSpec((1, D_MODEL, tn), lambda l, j: (l, 0, j)),
                  pl.BlockSpec((1, 1, tn), lambda l, j: (l, 0, j))],
        out_specs=pl.BlockSpec((1, 8, tn), lambda l, j: (l, 0, j)),
        compiler_params=_params(("arbitrary", "arbitrary")),
        name="adaln_mod",
    )(cvec, w_ada, b_ada.reshape(DEPTH, 1, n3))


IN_TM, IN_TN = 2048, 1024


def _modulated_norm(x, g, mod_ref):
    return _rms(x, g) * (1.0 + mod_ref[0, 1:2, :]) + mod_ref[0, 0:1, :]


def _prenorm_kernel(x_ref, mod_ref, g_ref, h_ref):
    h_ref[...] = _modulated_norm(x_ref[...], g_ref[...], mod_ref).astype(BF16)


def _prenorm(x, mod3, g_pre, rows_per_mod, mod_base):
    m = x.shape[0]
    t = ROW_BLK
    tiles_per_mod = rows_per_mod // t
    return pl.pallas_call(
        _prenorm_kernel,
        out_shape=jax.ShapeDtypeStruct((m, D_MODEL), BF16),
        grid=(m // t,),
        in_specs=[pl.BlockSpec((t, D_MODEL), lambda i: (i, 0)),
                  pl.BlockSpec((1, 3, D_MODEL), lambda i: (mod_base + i // tiles_per_mod, 0, 0)),
                  _const_spec((1, D_MODEL))],
        out_specs=pl.BlockSpec((t, D_MODEL), lambda i: (i, 0)),
        compiler_params=_params(("arbitrary",)),
        name="prenorm",
    )(x, mod3, g_pre)


def _proj_kernel(h_ref, w_ref, z_ref):
    z_ref[...] = _mm(h_ref[...], w_ref[0]).astype(z_ref.dtype)


def _proj(h, w, layer, tile0, width, out_dtype):
    m = h.shape[0]
    tm, tn = IN_TM, IN_TN
    return pl.pallas_call(
        _proj_kernel,
        out_shape=jax.ShapeDtypeStruct((m, width), out_dtype),
        grid=(m // tm, width // tn),
        in_specs=[pl.BlockSpec((tm, D_MODEL), lambda i, j: (i, 0)),
                  pl.BlockSpec((1, D_MODEL, tn), lambda i, j: (layer, 0, tile0 + j))],
        out_specs=pl.BlockSpec((tm, tn), lambda i, j: (i, j)),
        compiler_params=_params(("arbitrary", "arbitrary")),
        name="in_proj",
    )(h, w)


def _in_proj(h, w_head, w_tail, layer):
    z_head = _proj(h, w_head, layer, 0, HEAD_W, F32)
    z_mid = _proj(h, w_tail, layer, 0, MID_W, BF16)
    z_mg = _proj(h, w_tail, layer, MID_W // IN_TN, MG_W, BF16)
    return z_head, z_mid, z_mg


def _softmax_pv(s_list, v_list):
    m = s_list[0].max(axis=-1, keepdims=True)
    for s in s_list[1:]:
        m = jnp.maximum(m, s.max(axis=-1, keepdims=True))
    acc, den = None, None
    for s, v in zip(s_list, v_list):
        e = jnp.exp(s - m)
        d = e.sum(axis=-1, keepdims=True)
        o = _mm(e.astype(BF16), v)
        acc = o if acc is None else acc + o
        den = d if den is None else den + d
    return acc, den


def _mla_heads(qn, qr_fn, kv, krp, o_ref):
    scale = MLA_QK ** -0.5
    outs = []
    for h in range(MLA_HEADS):
        sl = slice(h * LANE, (h + 1) * LANE)
        kvh = kv[:, sl]
        s = (_nt(qn[:, sl], kvh) + _nt(qr_fn(sl), krp)) * scale
        acc, den = _softmax_pv([s], [kvh])
        outs.append(acc[:, MLA_NOPE:] / den)
    o_ref[...] = jnp.concatenate(outs, axis=-1)


def _prompt_attn_kernel(qkv_ref, cq_ref, ckv_ref, kr_ref, gq_ref, gkv_ref, wn_ref, wr_ref, wukv_ref,
                        ona_ref, omla_ref, ckvn_ref, k_ref, v_ref, kro_ref):
    qkv = qkv_ref[...]
    k_ref[...] = qkv[:, NA_W:2 * NA_W]
    v_ref[...] = qkv[:, 2 * NA_W:3 * NA_W]
    kro_ref[...] = kr_ref[:, 0:MLA_ROPE]
    scale = NA_HD ** -0.5
    outs = []
    for h in range(NA_HEADS):
        qh = qkv[:, h * NA_HD:(h + 1) * NA_HD].astype(BF16)
        kh = qkv[:, NA_W + h * NA_HD:NA_W + (h + 1) * NA_HD].astype(BF16)
        vh = qkv[:, 2 * NA_W + h * NA_HD:2 * NA_W + (h + 1) * NA_HD].astype(BF16)
        acc, den = _softmax_pv([_nt(qh, kh) * scale], [vh])
        outs.append(acc / den)
    ona_ref[...] = jnp.concatenate(outs, axis=-1)

    cqn = _rms(cq_ref[...], gq_ref[...]).astype(BF16)
    qn = _mm(cqn, wn_ref[...]).astype(BF16)
    qr = _mm(cqn, wr_ref[...]).astype(BF16)
    ckvn = _rms(ckv_ref[...], gkv_ref[...])
    ckvn_ref[...] = ckvn
    kv = _mm(ckvn.astype(BF16), wukv_ref[...]).astype(BF16)
    krp = kr_ref[...].astype(BF16)
    _mla_heads(qn, lambda sl: qr[:, sl], kv, krp, omla_ref)


def _prompt_attn(z, n_batch, g_q, g_kv, wn, wr, w_ukv):
    m = z.shape[0]
    t = ROW_BLK
    hw = MLA_HEADS * LANE
    return pl.pallas_call(
        _prompt_attn_kernel,
        out_shape=(jax.ShapeDtypeStruct((m, NA_W), F32), jax.ShapeDtypeStruct((m, MLA_W), F32),
                   jax.ShapeDtypeStruct((m, MLA_KV_LORA), F32), jax.ShapeDtypeStruct((m, NA_W), F32),
                   jax.ShapeDtypeStruct((m, NA_W), F32), jax.ShapeDtypeStruct((m, MLA_ROPE), F32)),
        grid=(n_batch,),
        in_specs=[pl.BlockSpec((t, 3 * NA_W), lambda b: (b, C_NA_Q // (3 * NA_W))),
                  pl.BlockSpec((t, MLA_Q_LORA), lambda b: (b, C_MLA_CQ // MLA_Q_LORA)),
                  pl.BlockSpec((t, MLA_KV_LORA), lambda b: (b, C_MLA_CKV // MLA_KV_LORA)),
                  pl.BlockSpec((t, LANE), lambda b: (b, C_MLA_KR // LANE)),
                  _const_spec((1, MLA_Q_LORA)), _const_spec((1, MLA_KV_LORA)),
                  _const_spec((MLA_Q_LORA, hw)), _const_spec((MLA_Q_LORA, hw)), _const_spec((MLA_KV_LORA, hw))],
        out_specs=(pl.BlockSpec((t, NA_W), lambda b: (b, 0)), pl.BlockSpec((t, MLA_W), lambda b: (b, 0)),
                   pl.BlockSpec((t, MLA_KV_LORA), lambda b: (b, 0)), pl.BlockSpec((t, NA_W), lambda b: (b, 0)),
                   pl.BlockSpec((t, NA_W), lambda b: (b, 0)), pl.BlockSpec((t, MLA_ROPE), lambda b: (b, 0))),
        compiler_params=_params(("arbitrary",)),
        name="prompt_attn",
    )(z, z, z, z, g_q, g_kv, wn, wr, w_ukv)


def _sample_na_kernel(q_ref, k0_ref, k1_ref, k2_ref, v0_ref, v1_ref, v2_ref, kc_ref, vc_ref, bias_ref, o_ref):
    scale = NA_HD ** -0.5
    q = q_ref[...]
    kl = [r[...] for r in (k0_ref, k1_ref, k2_ref)]
    vl = [r[...] for r in (v0_ref, v1_ref, v2_ref)]
    kc, vc = kc_ref[0, 0], vc_ref[0, 0]
    outs = []
    for h in range(NA_HEADS):
        sl = slice(h * NA_HD, (h + 1) * NA_HD)
        qh = q[:, sl].astype(BF16)
        s_list = [_nt(qh, kl[i][:, sl].astype(BF16)) * scale + bias_ref[0, h, :, i * ROW_BLK:(i + 1) * ROW_BLK]
                  for i in range(3)]
        s_list.append(_nt(qh, kc[:, sl].astype(BF16)) * scale)
        v_list = [vl[i][:, sl].astype(BF16) for i in range(3)] + [vc[:, sl].astype(BF16)]
        acc, den = _softmax_pv(s_list, v_list)
        outs.append(acc / den)
    o_ref[...] = jnp.concatenate(outs, axis=-1)


def _sample_na(z, cache_k, cache_v, bias_tab, layer, n_batch):
    m = z.shape[0]
    t = ROW_BLK
    nblk = m // n_batch // t
    past = cache_k.shape[2]

    def kv_map(col, i):
        return lambda b, rb: (b * nblk + jnp.clip(rb - 1, 0, nblk - 3) + i, col // NA_W)

    def variant(b, rb):
        return (jnp.where(rb == 0, 0, jnp.where(rb == nblk - 1, 2, 1)), 0, 0, 0)

    cache_spec = pl.BlockSpec((1, 1, past, NA_W), lambda b, rb: (b, layer, 0, 0))
    return pl.pallas_call(
        _sample_na_kernel,
        out_shape=jax.ShapeDtypeStruct((m, NA_W), F32),
        grid=(n_batch, nblk),
        in_specs=[pl.BlockSpec((t, NA_W), lambda b, rb: (b * nblk + rb, C_NA_Q // NA_W))]
                 + [pl.BlockSpec((t, NA_W), kv_map(C_NA_K, i)) for i in range(3)]
                 + [pl.BlockSpec((t, NA_W), kv_map(C_NA_V, i)) for i in range(3)]
                 + [cache_spec, cache_spec,
                    pl.BlockSpec((1, NA_HEADS, t, 3 * t), variant)],
        out_specs=pl.BlockSpec((t, NA_W), lambda b, rb: (b * nblk + rb, 0)),
        compiler_params=_params(("arbitrary", "arbitrary")),
        name="sample_na",
    )(z, z, z, z, z, z, z, cache_k, cache_v, bias_tab)


def _sample_mla_kv_kernel(cckv_ref, ckr_ref, ckv_ref, kr_ref, krp_ref, cos_ref, sin_ref, gkv_ref, wukv_ref,
                          kv_ref, kro_ref):
    j = pl.program_id(1)

    @pl.when(j == 0)
    def _():
        kv_ref[0] = _mm(cckv_ref[0, 0].astype(BF16), wukv_ref[...]).astype(BF16)
        kro_ref[0] = ckr_ref[0, 0].astype(BF16)

    @pl.when(j > 0)
    def _():
        ckvn = _rms(ckv_ref[...], gkv_ref[...])
        kv_ref[0] = _mm(ckvn.astype(BF16), wukv_ref[...]).astype(BF16)
        kro_ref[0] = (kr_ref[...] * cos_ref[...] + krp_ref[...] * sin_ref[...]).astype(BF16)


def _sample_mla_kv(z, cache_ckv, cache_kr_pad, cos_t, sin_t, g_kv, w_ukv, layer, n_batch):
    m = z.shape[0]
    n = m // n_batch
    past = cache_ckv.shape[2]
    t = past
    nb = n // t
    hw = MLA_HEADS * LANE

    def zrow(b, j):
        return b * nb + jnp.maximum(j - 1, 0)

    return pl.pallas_call(
        _sample_mla_kv_kernel,
        out_shape=(jax.ShapeDtypeStruct((n_batch, past + n, hw), BF16),
                   jax.ShapeDtypeStruct((n_batch, past + n, LANE), BF16)),
        grid=(n_batch, nb + 1),
        in_specs=[pl.BlockSpec((1, 1, past, MLA_KV_LORA), lambda b, j: (b, layer, 0, 0)),
                  pl.BlockSpec((1, 1, past, LANE), lambda b, j: (b, layer, 0, 0)),
                  pl.BlockSpec((t, MLA_KV_LORA), lambda b, j: (zrow(b, j), C_MLA_CKV // MLA_KV_LORA)),
                  pl.BlockSpec((t, LANE), lambda b, j: (zrow(b, j), C_MLA_KR // LANE)),
                  pl.BlockSpec((t, LANE), lambda b, j: (zrow(b, j), C_MLA_KRP // LANE)),
                  pl.BlockSpec((t, LANE), lambda b, j: (jnp.maximum(j - 1, 0), 0)),
                  pl.BlockSpec((t, LANE), lambda b, j: (jnp.maximum(j - 1, 0), 0)),
                  _const_spec((1, MLA_KV_LORA)), _const_spec((MLA_KV_LORA, hw))],
        out_specs=(pl.BlockSpec((1, t, hw), lambda b, j: (b, j, 0)),
                   pl.BlockSpec((1, t, LANE), lambda b, j: (b, j, 0))),
        compiler_params=_params(("arbitrary", "arbitrary")),
        name="sample_mla_kv",
    )(cache_ckv, cache_kr_pad, z, z, z, cos_t, sin_t, g_kv, w_ukv)


def _sample_mla_attn_kernel(cq_ref, cos_ref, sin_ref, gq_ref, wn_ref, wr_ref, wrp_ref, kv_ref, kr_ref, o_ref):
    cqn = _rms(cq_ref[...], gq_ref[...]).astype(BF16)
    qn = _mm(cqn, wn_ref[...]).astype(BF16)
    qr = _mm(cqn, wr_ref[...])
    qrp = _mm(cqn, wrp_ref[...])
    cos, sin = cos_ref[...], sin_ref[...]
    _mla_heads(qn, lambda sl: (qr[:, sl] * cos + qrp[:, sl] * sin).astype(BF16), kv_ref[0], kr_ref[0], o_ref)


def _sample_mla_attn(z, cos_t, sin_t, g_q, wn, wr, wrp, kv_all, kr_all, n_batch):
    m = z.shape[0]
    t = ROW_BLK
    nblk = m // n_batch // t
    nk = kv_all.shape[1]
    hw = MLA_HEADS * LANE
    return pl.pallas_call(
        _sample_mla_attn_kernel,
        out_shape=jax.ShapeDtypeStruct((m, MLA_W), F32),
        grid=(n_batch, nblk),
        in_specs=[pl.BlockSpec((t, MLA_Q_LORA), lambda b, i: (b * nblk + i, C_MLA_CQ // MLA_Q_LORA)),
                  pl.BlockSpec((t, LANE), lambda b, i: (i, 0)),
                  pl.BlockSpec((t, LANE), lambda b, i: (i, 0)),
                  _const_spec((1, MLA_Q_LORA)),
                  _const_spec((MLA_Q_LORA, hw)), _const_spec((MLA_Q_LORA, hw)), _const_spec((MLA_Q_LORA, hw)),
                  pl.BlockSpec((1, nk, hw), lambda b, i: (b, 0, 0)),
                  pl.BlockSpec((1, nk, LANE), lambda b, i: (b, 0, 0))],
        out_specs=pl.BlockSpec((t, MLA_W), lambda b, i: (b * nblk + i, 0)),
        compiler_params=_params(("arbitrary", "arbitrary")),
        name="sample_mla_attn",
    )(z, cos_t, sin_t, g_q, wn, wr, wrp, kv_all, kr_all)


def _hg_direction(fwd, q_ref, zf_ref, v_ref, loglb_ref, l1m_ref, s_ref, o_ref, qs, ks, bs, vs):
    t = ROW_BLK
    nchunk = t // HG_CHUNK
    q = _silu(q_ref[...].astype(F32))
    zf = zf_ref[...].astype(F32)
    v = v_ref[...].astype(F32)
    soft = jnp.log1p(jnp.exp(-jnp.abs(zf)))
    a1 = jnp.broadcast_to(loglb_ref[...], zf.shape)
    a2 = l1m_ref[...] + (jnp.minimum(zf, 0.0) - soft)
    logf = jnp.maximum(a1, a2) + jnp.log1p(jnp.exp(-jnp.abs(a1 - a2)))
    k = jnp.exp(l1m_ref[...] - jnp.maximum(zf, 0.0) - soft)

    ri = lax.broadcasted_iota(jnp.int32, (t, t), 0)
    ci = lax.broadcasted_iota(jnp.int32, (t, t), 1)
    causal = (ri >= ci) if fwd else (ri <= ci)
    same_sub = (ri // HG_SUB) == (ci // HG_SUB)
    same = (ri // HG_CHUNK) == (ci // HG_CHUNK)
    tri_sub = same_sub & causal
    cross = same & jnp.logical_not(same_sub) & causal
    hi = logf.astype(BF16)
    lo = (logf - hi.astype(F32)).astype(BF16)

    def seg_sum(mask):
        m16 = jnp.where(mask, 1.0, 0.0).astype(BF16)
        return _mm(m16, hi) + _mm(m16, lo)

    b_sub = seg_sum(tri_sub)
    tot_sub = seg_sum(same_sub)
    tot = seg_sum(same)
    row = lax.broadcasted_iota(jnp.int32, (t, 1), 0) % HG_CHUNK
    later = (row >= HG_SUB) if fwd else (row < HG_SUB)
    b = b_sub + jnp.where(later, tot - tot_sub, 0.0)

    qt_sub = (q * jnp.exp(b_sub)).astype(BF16)
    kt_sub = (k * jnp.exp(jnp.minimum(-b_sub, HG_EXP_CLAMP))).astype(BF16)
    kh_sub = (k * jnp.exp(tot_sub - b_sub)).astype(BF16)
    qt16 = (q * jnp.exp(b)).astype(BF16)
    kh16 = (k * jnp.exp(tot - b)).astype(BF16)
    v16 = v.astype(BF16)

    outs = []
    for h in range(HG_HEADS):
        sl = slice(h * HG_DK, (h + 1) * HG_DK)
        a = (jnp.where(tri_sub, _nt(qt_sub[:, sl], kt_sub[:, sl]), 0.0)
             + jnp.where(cross, _nt(qt_sub[:, sl], kh_sub[:, sl]), 0.0))
        outs.append(_mm(a.astype(BF16), v16[:, sl]))
    o_ref[...] = jnp.concatenate(outs, axis=-1)

    qs[...] = q
    ks[...] = k
    bs[...] = b
    vs[...] = v
    lane = lax.broadcasted_iota(jnp.int32, (HG_KW, LANE), 0) // HG_DK
    col = lax.broadcasted_iota(jnp.int32, (HG_KW, LANE), 1)
    head_sum = (lane == col).astype(F32)
    srow = lax.broadcasted_iota(jnp.int32, (HG_CHUNK, 1), 0)
    for c in range(nchunk):
        r0 = c * HG_CHUNK
        worst = jnp.max(-jnp.minimum(tot_sub[r0:r0 + 1, :], tot_sub[r0 + HG_SUB:r0 + HG_SUB + 1, :]))

        @pl.when(worst > HG_SAFE_DECAY)
        def _():
            kc = ks[r0:r0 + HG_CHUNK, :]
            bc = bs[r0:r0 + HG_CHUNK, :]
            vc = vs[r0:r0 + HG_CHUNK, :]

            def body(i, carry):
                qrow = qs[pl.ds(r0 + i, 1), :]
                brow = bs[pl.ds(r0 + i, 1), :]
                p = qrow * kc * jnp.exp(jnp.minimum(brow - bc, 0.0))
                keep = (srow <= i) if fwd else (srow >= i)
                p = jnp.where(keep, p, 0.0)
                a = _mm(p, head_sum, HI)
                a_full = _nt(a, head_sum, HI)
                o_ref[pl.ds(r0 + i, 1), :] = jnp.sum(a_full * vc, axis=0, keepdims=True)
                return carry

            lax.fori_loop(0, HG_CHUNK, body, 0)

    gw = 4 * HG_DK
    bd_r = lax.broadcasted_iota(jnp.int32, (gw, gw), 0) // HG_DV
    bd_c = lax.broadcasted_iota(jnp.int32, (gw, gw), 1) // HG_DK
    bd = bd_r == bd_c
    order = range(nchunk) if fwd else range(nchunk - 1, -1, -1)
    for g in range(2):
        ls = slice(g * gw, (g + 1) * gw)
        s = s_ref[g]
        for c in order:
            rs = slice(c * HG_CHUNK, (c + 1) * HG_CHUNK)
            o_ref[rs, ls] += _nt(qt16[rs, ls], s.astype(BF16))
            u = _tn(v16[rs, ls], kh16[rs, ls])
            s = s * jnp.exp(tot[c * HG_CHUNK:c * HG_CHUNK + 1, ls]) + jnp.where(bd, u, 0.0)
        s_ref[g] = s


def _head_block(h):
    g, hh = divmod(h, 4)
    return g, slice(hh * HG_DV, (hh + 1) * HG_DV), slice(hh * HG_DK, (hh + 1) * HG_DK)


def _hgrn_kernel(*refs, has_state):
    (qf_ref, ff_ref, vf_ref, qb_ref, fb_ref, vb_ref, lbf_ref, l1f_ref, lbb_ref, l1b_ref) = refs[:10]
    s0_refs = refs[10:12] if has_state else (None, None)
    of_ref, ob_ref, sf_out, sb_out, sf, sb, qs, ks, bs, vs = refs[12:] if has_state else refs[10:]
    i = pl.program_id(1)

    @pl.when(i == 0)
    def _():
        for scr, s0_ref in zip((sf, sb), s0_refs):
            scr[...] = jnp.zeros(scr.shape, F32)
            if s0_ref is not None:
                for h in range(HG_HEADS):
                    g, rv, ck = _head_block(h)
                    scr[g, rv, ck] = s0_ref[0, 0, h].T

    _hg_direction(True, qf_ref, ff_ref, vf_ref, lbf_ref, l1f_ref, sf, of_ref, qs, ks, bs, vs)
    _hg_direction(False, qb_ref, fb_ref, vb_ref, lbb_ref, l1b_ref, sb, ob_ref, qs, ks, bs, vs)

    @pl.when(i == pl.num_programs(1) - 1)
    def _():
        for scr, out in ((sf, sf_out), (sb, sb_out)):
            for h in range(HG_HEADS):
                g, rv, ck = _head_block(h)
                out[0, h] = scr[g, rv, ck].T


def _hgrn(z, n_batch, lb_f, lb_b, s0f=None, s0b=None, layer=0):
    m = z.shape[0]
    t = ROW_BLK
    nblk = m // n_batch // t
    has_state = s0f is not None
    loglb_f, l1m_f = _lb_logs(lb_f)
    loglb_b, l1m_b = _lb_logs(lb_b)
    row = lambda a: a.reshape(1, HG_KW)

    def fmap(col):
        return lambda b, i: (b * nblk + i, col // HG_KW)

    def bmap(col):
        return lambda b, i: (b * nblk + nblk - 1 - i, col // HG_KW)

    blk = lambda imap: pl.BlockSpec((t, HG_KW), imap)
    st_out = pl.BlockSpec((1, HG_HEADS, HG_DK, HG_DV), lambda b, i: (b, 0, 0, 0))
    st_shape = jax.ShapeDtypeStruct((n_batch, HG_HEADS, HG_DK, HG_DV), F32)
    st_in = pl.BlockSpec((1, 1, HG_HEADS, HG_DK, HG_DV), lambda b, i: (b, layer, 0, 0, 0))
    bd_scratch = pltpu.VMEM((2, 4 * HG_DV, 4 * HG_DK), F32)
    return pl.pallas_call(
        functools.partial(_hgrn_kernel, has_state=has_state),
        out_shape=(jax.ShapeDtypeStruct((m, HG_W), F32), jax.ShapeDtypeStruct((m, HG_W), F32), st_shape, st_shape),
        grid=(n_batch, nblk),
        in_specs=[blk(fmap(M_HG_Q)), blk(fmap(M_HG_FF)), blk(fmap(M_HG_I)),
                  blk(bmap(M_HG_Q)), blk(bmap(M_HG_FB)), blk(bmap(M_HG_I)),
                  _const_spec((1, HG_KW)), _const_spec((1, HG_KW)), _const_spec((1, HG_KW)), _const_spec((1, HG_KW))]
                 + ([st_in, st_in] if has_state else []),
        out_specs=(pl.BlockSpec((t, HG_W), lambda b, i: (b * nblk + i, 0)),
                   pl.BlockSpec((t, HG_W), lambda b, i: (b * nblk + nblk - 1 - i, 0)),
                   st_out, st_out),
        scratch_shapes=[bd_scratch, bd_scratch] + [pltpu.VMEM((t, HG_KW), F32)] * 4,
        compiler_params=_params(("arbitrary", "arbitrary")),
        name="hgrn_scan",
    )(z, z, z, z, z, z, row(loglb_f), row(l1m_f), row(loglb_b), row(l1m_b), *((s0f, s0b) if has_state else ()))


def _out_kernel(*refs, has_next):
    (x_ref, mod_ref, gpost_ref, ghg_ref, ona_ref, omla_ref, of_ref, ob_ref,
     gna_ref, gmla_ref, ghgate_ref, mg_ref, wna_ref, wmla_ref, whg_ref, wout_ref) = refs[:16]
    y_ref = refs[-2] if has_next else refs[-1]
    o = of_ref[...] + ob_ref[...]
    hr = lax.broadcasted_iota(jnp.int32, (HG_W, HG_W), 0) // HG_DV
    hc = lax.broadcasted_iota(jnp.int32, (HG_W, HG_W), 1) // HG_DV
    head_mean = jnp.where(hr == hc, 1.0 / HG_DV, 0.0).astype(F32)
    ms = _mm(o * o, head_mean, HI)
    o_hg = o * lax.rsqrt(ms + NORM_EPS) * ghg_ref[...]

    def branch(o_b, gate_ref, w_ref):
        return _mm((o_b * _silu(gate_ref[...].astype(F32))).astype(BF16), w_ref[...])

    def merge_gate(i):
        return jax.nn.sigmoid(mg_ref[:, i * D_MODEL:(i + 1) * D_MODEL].astype(F32))

    merged = (merge_gate(0) * branch(ona_ref[...], gna_ref, wna_ref)
              + merge_gate(1) * branch(omla_ref[...], gmla_ref, wmla_ref)
              + merge_gate(2) * branch(o_hg, ghgate_ref, whg_ref))
    out = _mm(merged.astype(BF16), wout_ref[...])
    y = x_ref[...] + mod_ref[0, 2:3, :] * _rms(out, gpost_ref[...])
    y_ref[...] = y
    if has_next:
        modn_ref, gpren_ref, hn_ref = refs[16], refs[17], refs[-1]
        hn_ref[...] = _modulated_norm(y, gpren_ref[...], modn_ref).astype(BF16)


def _out_proj(x, z_head, z_mid, z_mg, mod3, g_post, g_hg, o_na, o_mla, o_f, o_b, w_na, w_mla, w_hg, w_out,
              rows_per_mod, mod_base, next_mod3=None, next_g_pre=None):
    m = x.shape[0]
    t = ROW_BLK
    tiles_per_mod = rows_per_mod // t
    has_next = next_mod3 is not None
    w512 = lambda: pl.BlockSpec((t, NA_W), lambda i: (i, 0))
    zcol = lambda col, w: pl.BlockSpec((t, w), lambda i: (i, col // w))
    mod_spec = pl.BlockSpec((1, 3, D_MODEL), lambda i: (mod_base + i // tiles_per_mod, 0, 0))
    row_spec = pl.BlockSpec((t, D_MODEL), lambda i: (i, 0))
    one = pl.Buffered(1)
    out = pl.pallas_call(
        functools.partial(_out_kernel, has_next=has_next),
        out_shape=(jax.ShapeDtypeStruct((m, D_MODEL), F32),)
                  + ((jax.ShapeDtypeStruct((m, D_MODEL), BF16),) if has_next else ()),
        grid=(m // t,),
        in_specs=[row_spec, mod_spec,
                  _const_spec((1, D_MODEL)), _const_spec((1, HG_W)),
                  w512(), w512(), w512(), w512(),
                  zcol(C_NA_G, NA_W), zcol(M_MLA_G, MLA_W), zcol(M_HG_G, HG_W),
                  zcol(0, MG_W),
                  pl.BlockSpec((NA_W, D_MODEL), lambda i: (0, 0), pipeline_mode=one),
                  pl.BlockSpec((MLA_W, D_MODEL), lambda i: (0, 0), pipeline_mode=one),
                  pl.BlockSpec((HG_W, D_MODEL), lambda i: (0, 0), pipeline_mode=one),
                  pl.BlockSpec((D_MODEL, D_MODEL), lambda i: (0, 0), pipeline_mode=one)]
                 + ([mod_spec, _const_spec((1, D_MODEL))] if has_next else []),
        out_specs=(row_spec,) + ((row_spec,) if has_next else ()),
        compiler_params=_params(("arbitrary",)),
        name="out_proj",
    )(x, mod3, g_post, g_hg, o_na, o_mla, o_f, o_b, z_head, z_mid, z_mid, z_mg, w_na, w_mla, w_hg, w_out,
      *((next_mod3, next_g_pre) if has_next else ()))
    return out if has_next else (out[0], None)


def kernel(x_prompt, x_sample, cache_na_k, cache_na_v, cache_mla_ckv, cache_mla_krope, state_hgrn_fwd, state_hgrn_bwd, c, c_ctx, w_ada, b_ada, g_pre, g_post, w_in, na_rpb, g_mla_q, w_mla_uq, g_mla_kv, w_mla_ukv, hg_lb_fwd, hg_lb_bwd, g_hg_out, w_br_na, w_br_mla, w_br_hg, w_out):
    bp, sp, _ = x_prompt.shape
    bs, ss, _ = x_sample.shape
    past = cache_na_k.shape[2]

    cvec = jnp.concatenate([c_ctx[None, :], c, jnp.zeros((8 - 1 - bs, D_MODEL), F32)], axis=0)
    mod = _modulation(cvec, w_ada, b_ada).reshape(DEPTH, 8, 3, D_MODEL)

    lb_f_all = _lower_bounds(hg_lb_fwd)
    lb_b_all = _lower_bounds(hg_lb_bwd)
    cos_t, sin_t = _rope_tables(ss)
    cache_k = cache_na_k.reshape(bs, DEPTH, past, NA_W)
    cache_v = cache_na_v.reshape(bs, DEPTH, past, NA_W)
    cache_kr_pad = jnp.pad(cache_mla_krope, ((0, 0), (0, 0), (0, 0), (0, LANE - MLA_ROPE)))

    yp = x_prompt.reshape(bp * sp, D_MODEL)
    ys = x_sample.reshape(bs * ss, D_MODEL)
    new_k, new_v, new_ckv, new_kr, new_sf, new_sb = [], [], [], [], [], []
    w_head, w_tail = _pack_w_in(w_in)
    hp = _prenorm(yp, mod[0], g_pre[0][None, :], bp * sp, 0)
    hs = _prenorm(ys, mod[0], g_pre[0][None, :], ss, 1)
    for l in range(DEPTH):
        wn, wr, wrp = _pack_w_uq(w_mla_uq[l])
        w_ukv = w_mla_ukv[l].astype(BF16)
        w_na, w_mla, w_hg, w_o = (w.astype(BF16) for w in (w_br_na[l], w_br_mla[l], w_br_hg[l], w_out[l]))
        g_q, g_kv = g_mla_q[l][None, :], g_mla_kv[l][None, :]
        gpost, ghg = g_post[l][None, :], g_hg_out[l][None, :]
        mod3 = mod[l]
        nxt = (mod[l + 1], g_pre[l + 1][None, :]) if l + 1 < DEPTH else (None, None)

        zp_head, zp_mid, zp_mg = _in_proj(hp, w_head, w_tail, l)
        o_na, o_mla, ckvn, k_new, v_new, kr_new = _prompt_attn(zp_head, bp, g_q, g_kv, wn, wr, w_ukv)
        o_f, o_b, sf, sb = _hgrn(zp_mid, bp, lb_f_all[l], lb_b_all[l])
        yp, hp = _out_proj(yp, zp_head, zp_mid, zp_mg, mod3, gpost, ghg, o_na, o_mla, o_f, o_b,
                           w_na, w_mla, w_hg, w_o, bp * sp, 0, *nxt)
        new_k.append(k_new.reshape(bp, sp, NA_HEADS, NA_HD))
        new_v.append(v_new.reshape(bp, sp, NA_HEADS, NA_HD))
        new_ckv.append(ckvn.reshape(bp, sp, MLA_KV_LORA))
        new_kr.append(kr_new.reshape(bp, sp, MLA_ROPE))
        new_sf.append(sf)
        new_sb.append(sb)

        zs_head, zs_mid, zs_mg = _in_proj(hs, w_head, w_tail, l)
        o_na = _sample_na(zs_head, cache_k, cache_v, _na_bias_table(na_rpb[l]), l, bs)
        kv_all, kr_all = _sample_mla_kv(zs_head, cache_mla_ckv, cache_kr_pad, cos_t, sin_t, g_kv, w_ukv, l, bs)
        o_mla = _sample_mla_attn(zs_head, cos_t, sin_t, g_q, wn, wr, wrp, kv_all, kr_all, bs)
        o_f, o_b, _, _ = _hgrn(zs_mid, bs, lb_f_all[l], lb_b_all[l], state_hgrn_fwd, state_hgrn_bwd, l)
        ys, hs = _out_proj(ys, zs_head, zs_mid, zs_mg, mod3, gpost, ghg, o_na, o_mla, o_f, o_b,
                           w_na, w_mla, w_hg, w_o, ss, 1, *nxt)

    return (yp.reshape(bp, sp, D_MODEL), ys.reshape(bs, ss, D_MODEL),
            jnp.stack(new_k, axis=1), jnp.stack(new_v, axis=1), jnp.stack(new_ckv, axis=1),
            jnp.stack(new_kr, axis=1), jnp.stack(new_sf, axis=1), jnp.stack(new_sb, axis=1))
```

```python
import functools

import numpy as np
import jax
import jax.numpy as jnp
from jax import lax
from jax.experimental import pallas as pl
from jax.experimental.pallas import tpu as pltpu

D_MODEL = 2048
DEPTH = 2
GRID_W = 64
NORM_EPS = 1e-6
NEG_BIG = -1e30
NA_HEADS = 8
NA_HD = 64
NA_W = NA_HEADS * NA_HD
NA_KR = 8
NA_KW = 16
MLA_HEADS = 8
MLA_NOPE = 64
MLA_ROPE = 32
MLA_VD = 64
MLA_QK = MLA_NOPE + MLA_ROPE
MLA_W = MLA_HEADS * MLA_VD
MLA_Q_LORA = 512
MLA_KV_LORA = 256
ROPE_BASE = 10000.0
HG_HEADS = 8
HG_DK = 64
HG_DV = 64
HG_KW = HG_HEADS * HG_DK
HG_W = HG_HEADS * HG_DV
N_BRANCH = 3
PROJ_WIDTHS = (NA_W, NA_W, NA_W, NA_W, MLA_Q_LORA, MLA_KV_LORA, MLA_ROPE, MLA_W,
               HG_KW, HG_KW, HG_KW, HG_W, HG_W, N_BRANCH * D_MODEL)

F32 = jnp.float32
BF16 = jnp.bfloat16
HI = lax.Precision.HIGHEST

LANE = 128
ROW_BLK = 256
HG_CHUNK = 32
HG_SUB = 16
HG_SAFE_DECAY = 72.0
HG_EXP_CLAMP = 80.0
VMEM_LIMIT = 56 * 1024 * 1024

HEAD_W = 3072
MID_W = 3072
MG_W = N_BRANCH * D_MODEL
SPLIT_COL = 2848
C_NA_Q, C_NA_K, C_NA_V, C_NA_G = 0, 512, 1024, 1536
C_MLA_CQ, C_MLA_CKV, C_MLA_KR, C_MLA_KRP = 2048, 2560, 2816, 2944
M_MLA_G, M_HG_Q, M_HG_FF, M_HG_FB, M_HG_I, M_HG_G = 0, 512, 1024, 1536, 2048, 2560


def _nt(a, b, precision=None):
    return lax.dot_general(a, b, (((1,), (1,)), ((), ())), preferred_element_type=F32, precision=precision)


def _tn(a, b, precision=None):
    return lax.dot_general(a, b, (((0,), (0,)), ((), ())), preferred_element_type=F32, precision=precision)


def _mm(a, b, precision=None):
    return jnp.dot(a, b, preferred_element_type=F32, precision=precision)


def _rms(x, g):
    return x * lax.rsqrt(jnp.mean(x * x, axis=-1, keepdims=True) + NORM_EPS) * g


def _silu(x):
    return x * jax.nn.sigmoid(x)


def _params(sem):
    return pltpu.CompilerParams(dimension_semantics=sem, vmem_limit_bytes=VMEM_LIMIT)


def _const_spec(shape):
    nd = len(shape)
    return pl.BlockSpec(shape, lambda *_: (0,) * nd)


def _rot_half(a):
    parts = []
    for ax in range(2):
        lo_, hi_ = a[..., ax * 16:ax * 16 + 8], a[..., ax * 16 + 8:ax * 16 + 16]
        parts += [-hi_, lo_]
    return jnp.concatenate(parts, axis=-1)


def _pack_w_in(w):
    assert sum(PROJ_WIDTHS[:7]) == SPLIT_COL and sum(PROJ_WIDTHS[7:13]) == MID_W and PROJ_WIDTHS[13] == MG_W
    head = w[..., :SPLIT_COL].astype(BF16)
    kr = head[..., C_MLA_KR:SPLIT_COL]
    return jnp.concatenate([head[..., :C_MLA_KR], _rope_lanes(kr), _rope_lanes(_rot_half(kr))], axis=-1), \
        w[..., SPLIT_COL:].astype(BF16)


def _rope_lanes(a):
    z = lambda n: jnp.zeros(a.shape[:-1] + (n,), a.dtype)
    return jnp.concatenate([z(MLA_NOPE), a, z(LANE - MLA_QK)], axis=-1)


def _pack_w_uq(w):
    w3 = w.reshape(MLA_Q_LORA, MLA_HEADS, MLA_QK)
    nope, rope = w3[..., :MLA_NOPE], w3[..., MLA_NOPE:]
    pad = jnp.zeros(nope.shape[:-1] + (LANE - MLA_QK,), w.dtype)
    flat = lambda a: a.reshape(MLA_Q_LORA, MLA_HEADS * LANE).astype(BF16)
    return flat(jnp.concatenate([nope, rope, pad], axis=-1)), flat(_rope_lanes(_rot_half(rope)))


def _pack_w_ukv(w):
    w3 = w.reshape(MLA_KV_LORA, MLA_HEADS, MLA_NOPE + MLA_VD)
    pad = jnp.zeros((MLA_KV_LORA, MLA_HEADS, LANE - MLA_NOPE), w.dtype)
    flat = lambda a: a.reshape(MLA_KV_LORA, MLA_HEADS * LANE).astype(BF16)
    return flat(jnp.concatenate([w3[..., :MLA_NOPE], pad], -1)), flat(jnp.concatenate([w3[..., MLA_NOPE:], pad], -1))


def _rope_tables(n):
    t = np.arange(n)
    pos = np.stack([t // GRID_W, t % GRID_W]).astype(np.float32)
    axis_dim = MLA_ROPE // 2
    inv = ROPE_BASE ** (-jnp.arange(0, axis_dim, 2, dtype=F32) / axis_dim)
    ang = jnp.asarray(pos)[:, :, None] * inv
    ang = jnp.concatenate([ang, ang], axis=-1)
    ang = jnp.concatenate([ang[0], ang[1]], axis=-1)
    cos_t = jnp.concatenate([jnp.ones((n, MLA_NOPE), F32), jnp.cos(ang), jnp.zeros((n, LANE - MLA_QK), F32)], -1)
    return cos_t, _rope_lanes(jnp.sin(ang))


def _na_bias_table(rpb):
    rows = 32
    c = np.arange(GRID_W)
    win0 = np.clip(c - NA_KW // 2, 0, GRID_W - NA_KW)
    kc = np.arange(GRID_W)
    col_ok = (kc[None, :] >= win0[:, None]) & (kc[None, :] < win0[:, None] + NA_KW)
    rpb = rpb.astype(F32)
    edge = GRID_W - NA_KW
    ext = jnp.concatenate([jnp.broadcast_to(rpb[..., :1], rpb.shape[:-1] + (edge,)), rpb,
                           jnp.broadcast_to(rpb[..., -1:], rpb.shape[:-1] + (edge,))], axis=-1)
    toep = jnp.stack([ext[..., GRID_W - 1 - ci:2 * GRID_W - 1 - ci] for ci in range(GRID_W)], axis=2)
    toep = jnp.where(jnp.asarray(col_ok), toep, NEG_BIG)
    masked = jnp.full((NA_HEADS, GRID_W, GRID_W), NEG_BIG, F32)
    out = []
    for start, r0 in ((0, 0), (4, 8), (20, 28)):
        per_q = []
        for qr in range(4):
            r = r0 + qr
            kr0 = min(max(r - NA_KR // 2, 0), rows - NA_KR)
            tiles = []
            for j in range(12):
                kabs = start + j
                ok = kr0 <= kabs < kr0 + NA_KR
                tiles.append(toep[:, kabs - r + NA_KR - 1] if ok else masked)
            per_q.append(jnp.concatenate(tiles, axis=-1))
        out.append(jnp.concatenate(per_q, axis=1))
    return jnp.stack(out)


def _lower_bounds(p):
    s = jax.nn.softmax(p.astype(F32), axis=0)
    return jnp.cumsum(s, axis=0) - s[0]


def _lb_logs(lb):
    pos = lb > 0
    log_lb = jnp.where(pos, jnp.log(jnp.where(pos, lb, 1.0)), NEG_BIG)
    return log_lb, jnp.log1p(-lb)


def _mod_kernel(c_ref, w_ref, b_ref, o_ref):
    s = _silu(c_ref[...])
    o_ref[0] = _mm(s, w_ref[0], HI) + b_ref[0]


def _modulation(cvec, w_ada, b_ada):
    tn = 1024
    n3 = 3 * D_MODEL
    return pl.pallas_call(
        _mod_kernel,
        out_shape=jax.ShapeDtypeStruct((DEPTH, 8, n3), F32),
        grid=(DEPTH, n3 // tn),
        in_specs=[pl.BlockSpec((8, D_MODEL), lambda l, j: (0, 0)),
                  pl.BlockSpec((1, D_MODEL, tn), lambda l, j: (l, 0, j)),
                  pl.BlockSpec((1, 1, tn), lambda l, j: (l, 0, j))],
        out_specs=pl.BlockSpec((1, 8, tn), lambda l, j: (l, 0, j)),
        compiler_params=_params(("arbitrary", "arbitrary")),
        name="adaln_mod",
    )(cvec, w_ada, b_ada.reshape(DEPTH, 1, n3))


IN_TM, IN_TN = 2048, 1024


def _modulated_norm(x, g, mod_ref):
    return _rms(x, g) * (1.0 + mod_ref[0, 1:2, :]) + mod_ref[0, 0:1, :]


def _prenorm_kernel(x_ref, mod_ref, g_ref, h_ref):
    h_ref[...] = _modulated_norm(x_ref[...], g_ref[...], mod_ref).astype(BF16)


def _prenorm(x, mod3, g_pre, rows_per_mod, mod_base):
    m = x.shape[0]
    t = ROW_BLK
    tiles_per_mod = rows_per_mod // t
    return pl.pallas_call(
        _prenorm_kernel,
        out_shape=jax.ShapeDtypeStruct((m, D_MODEL), BF16),
        grid=(m // t,),
        in_specs=[pl.BlockSpec((t, D_MODEL), lambda i: (i, 0)),
                  pl.BlockSpec((1, 3, D_MODEL), lambda i: (mod_base + i // tiles_per_mod, 0, 0)),
                  _const_spec((1, D_MODEL))],
        out_specs=pl.BlockSpec((t, D_MODEL), lambda i: (i, 0)),
        compiler_params=_params(("arbitrary",)),
        name="prenorm",
    )(x, mod3, g_pre)


def _proj_kernel(h_ref, w_ref, z_ref):
    z_ref[...] = _mm(h_ref[...], w_ref[0]).astype(z_ref.dtype)


def _proj(h, w, layer, tile0, width, out_dtype):
    m = h.shape[0]
    tm, tn = IN_TM, IN_TN
    return pl.pallas_call(
        _proj_kernel,
        out_shape=jax.ShapeDtypeStruct((m, width), out_dtype),
        grid=(m // tm, width // tn),
        in_specs=[pl.BlockSpec((tm, D_MODEL), lambda i, j: (i, 0)),
                  pl.BlockSpec((1, D_MODEL, tn), lambda i, j: (layer, 0, tile0 + j))],
        out_specs=pl.BlockSpec((tm, tn), lambda i, j: (i, j)),
        compiler_params=_params(("arbitrary", "arbitrary")),
        name="in_proj",
    )(h, w)


def _in_proj(h, w_head, w_tail, layer):
    z_head = _proj(h, w_head, layer, 0, HEAD_W, F32)
    z_mid = _proj(h, w_tail, layer, 0, MID_W, BF16)
    z_mg = _proj(h, w_tail, layer, MID_W // IN_TN, MG_W, BF16)
    return z_head, z_mid, z_mg


def _softmax_pv(s_list, v_list):
    m = s_list[0].max(axis=-1, keepdims=True)
    for s in s_list[1:]:
        m = jnp.maximum(m, s.max(axis=-1, keepdims=True))
    acc, den = None, None
    for s, v in zip(s_list, v_list):
        e = jnp.exp(s - m)
        d = e.sum(axis=-1, keepdims=True)
        o = _mm(e.astype(BF16), v)
        acc = o if acc is None else acc + o
        den = d if den is None else den + d
    return acc, den


def _ones_lane():
    return (lax.broadcasted_iota(jnp.int32, (1, LANE), 1) == MLA_VD).astype(F32)


def _mla_heads(q_fn, kq, vp, o_ref):
    scale = MLA_QK ** -0.5
    outs = []
    for h in range(MLA_HEADS):
        sl = slice(h * LANE, (h + 1) * LANE)
        s = _nt(q_fn(sl), kq[:, sl]) * scale
        e = jnp.exp(s - s.max(axis=-1, keepdims=True))
        acc = _mm(e.astype(BF16), vp[:, sl])
        outs.append(acc[:, :MLA_VD] / acc[:, MLA_VD:MLA_VD + 1])
    o_ref[...] = jnp.concatenate(outs, axis=-1)


def _mla_keys_values(ckvn16, kr_tile, wuk_ref, wuv_ref):
    kq = _mm(ckvn16, wuk_ref[...]) + jnp.concatenate([kr_tile] * MLA_HEADS, axis=-1)
    vp = _mm(ckvn16, wuv_ref[...]) + jnp.concatenate([_ones_lane()] * MLA_HEADS, axis=-1)
    return kq.astype(BF16), vp.astype(BF16)


def _prompt_attn_kernel(qkv_ref, cq_ref, ckv_ref, kr_ref, gq_ref, gkv_ref, wq_ref, wuk_ref, wuv_ref,
                        ona_ref, omla_ref, ckvn_ref, k_ref, v_ref, kro_ref):
    qkv = qkv_ref[...]
    k_ref[...] = qkv[:, NA_W:2 * NA_W]
    v_ref[...] = qkv[:, 2 * NA_W:3 * NA_W]
    kro_ref[...] = kr_ref[:, MLA_NOPE:MLA_QK]
    scale = NA_HD ** -0.5
    outs = []
    for h in range(NA_HEADS):
        qh = qkv[:, h * NA_HD:(h + 1) * NA_HD].astype(BF16)
        kh = qkv[:, NA_W + h * NA_HD:NA_W + (h + 1) * NA_HD].astype(BF16)
        vh = qkv[:, 2 * NA_W + h * NA_HD:2 * NA_W + (h + 1) * NA_HD].astype(BF16)
        acc, den = _softmax_pv([_nt(qh, kh) * scale], [vh])
        outs.append(acc / den)
    ona_ref[...] = jnp.concatenate(outs, axis=-1)

    cqn = _rms(cq_ref[...], gq_ref[...]).astype(BF16)
    q = _mm(cqn, wq_ref[...]).astype(BF16)
    ckvn = _rms(ckv_ref[...], gkv_ref[...])
    ckvn_ref[...] = ckvn
    kq, vp = _mla_keys_values(ckvn.astype(BF16), kr_ref[...], wuk_ref, wuv_ref)
    _mla_heads(lambda sl: q[:, sl], kq, vp, omla_ref)


def _prompt_attn(z, n_batch, g_q, g_kv, wq, wuk, wuv):
    m = z.shape[0]
    t = ROW_BLK
    hw = MLA_HEADS * LANE
    return pl.pallas_call(
        _prompt_attn_kernel,
        out_shape=(jax.ShapeDtypeStruct((m, NA_W), F32), jax.ShapeDtypeStruct((m, MLA_W), F32),
                   jax.ShapeDtypeStruct((m, MLA_KV_LORA), F32), jax.ShapeDtypeStruct((m, NA_W), F32),
                   jax.ShapeDtypeStruct((m, NA_W), F32), jax.ShapeDtypeStruct((m, MLA_ROPE), F32)),
        grid=(n_batch,),
        in_specs=[pl.BlockSpec((t, 3 * NA_W), lambda b: (b, C_NA_Q // (3 * NA_W))),
                  pl.BlockSpec((t, MLA_Q_LORA), lambda b: (b, C_MLA_CQ // MLA_Q_LORA)),
                  pl.BlockSpec((t, MLA_KV_LORA), lambda b: (b, C_MLA_CKV // MLA_KV_LORA)),
                  pl.BlockSpec((t, LANE), lambda b: (b, C_MLA_KR // LANE)),
                  _const_spec((1, MLA_Q_LORA)), _const_spec((1, MLA_KV_LORA)),
                  _const_spec((MLA_Q_LORA, hw)), _const_spec((MLA_KV_LORA, hw)), _const_spec((MLA_KV_LORA, hw))],
        out_specs=(pl.BlockSpec((t, NA_W), lambda b: (b, 0)), pl.BlockSpec((t, MLA_W), lambda b: (b, 0)),
                   pl.BlockSpec((t, MLA_KV_LORA), lambda b: (b, 0)), pl.BlockSpec((t, NA_W), lambda b: (b, 0)),
                   pl.BlockSpec((t, NA_W), lambda b: (b, 0)), pl.BlockSpec((t, MLA_ROPE), lambda b: (b, 0))),
        compiler_params=_params(("arbitrary",)),
        name="prompt_attn",
    )(z, z, z, z, g_q, g_kv, wq, wuk, wuv)


def _sample_na_kernel(q_ref, k0_ref, k1_ref, k2_ref, v0_ref, v1_ref, v2_ref, kc_ref, vc_ref, bias_ref, o_ref):
    scale = NA_HD ** -0.5
    q = q_ref[...]
    kl = [r[...] for r in (k0_ref, k1_ref, k2_ref)]
    vl = [r[...] for r in (v0_ref, v1_ref, v2_ref)]
    kc, vc = kc_ref[0, 0], vc_ref[0, 0]
    outs = []
    for h in range(NA_HEADS):
        sl = slice(h * NA_HD, (h + 1) * NA_HD)
        qh = q[:, sl].astype(BF16)
        s_list = [_nt(qh, kl[i][:, sl].astype(BF16)) * scale + bias_ref[0, h, :, i * ROW_BLK:(i + 1) * ROW_BLK]
                  for i in range(3)]
        s_list.append(_nt(qh, kc[:, sl].astype(BF16)) * scale)
        v_list = [vl[i][:, sl].astype(BF16) for i in range(3)] + [vc[:, sl].astype(BF16)]
        acc, den = _softmax_pv(s_list, v_list)
        outs.append(acc / den)
    o_ref[...] = jnp.concatenate(outs, axis=-1)


def _sample_na(z, cache_k, cache_v, bias_tab, layer, n_batch):
    m = z.shape[0]
    t = ROW_BLK
    nblk = m // n_batch // t
    past = cache_k.shape[2]

    def kv_map(col, i):
        return lambda b, rb: (b * nblk + jnp.clip(rb - 1, 0, nblk - 3) + i, col // NA_W)

    def variant(b, rb):
        return (jnp.where(rb == 0, 0, jnp.where(rb == nblk - 1, 2, 1)), 0, 0, 0)

    cache_spec = pl.BlockSpec((1, 1, past, NA_W), lambda b, rb: (b, layer, 0, 0))
    return pl.pallas_call(
        _sample_na_kernel,
        out_shape=jax.ShapeDtypeStruct((m, NA_W), F32),
        grid=(n_batch, nblk),
        in_specs=[pl.BlockSpec((t, NA_W), lambda b, rb: (b * nblk + rb, C_NA_Q // NA_W))]
                 + [pl.BlockSpec((t, NA_W), kv_map(C_NA_K, i)) for i in range(3)]
                 + [pl.BlockSpec((t, NA_W), kv_map(C_NA_V, i)) for i in range(3)]
                 + [cache_spec, cache_spec,
                    pl.BlockSpec((1, NA_HEADS, t, 3 * t), variant)],
        out_specs=pl.BlockSpec((t, NA_W), lambda b, rb: (b * nblk + rb, 0)),
        compiler_params=_params(("arbitrary", "arbitrary")),
        name="sample_na",
    )(z, z, z, z, z, z, z, cache_k, cache_v, bias_tab)


def _sample_mla_kv_kernel(cckv_ref, ckr_ref, ckv_ref, kr_ref, krp_ref, cos_ref, sin_ref, gkv_ref, wuk_ref, wuv_ref,
                          kq_ref, vp_ref):
    j = pl.program_id(1)

    @pl.when(j == 0)
    def _():
        kq_ref[0], vp_ref[0] = _mla_keys_values(cckv_ref[0, 0].astype(BF16), ckr_ref[0, 0], wuk_ref, wuv_ref)

    @pl.when(j > 0)
    def _():
        ckvn = _rms(ckv_ref[...], gkv_ref[...])
        kr_roped = kr_ref[...] * cos_ref[...] + krp_ref[...] * sin_ref[...]
        kq_ref[0], vp_ref[0] = _mla_keys_values(ckvn.astype(BF16), kr_roped, wuk_ref, wuv_ref)


def _sample_mla_kv(z, cache_ckv, cache_kr_pad, cos_t, sin_t, g_kv, wuk, wuv, layer, n_batch):
    m = z.shape[0]
    n = m // n_batch
    past = cache_ckv.shape[2]
    t = past
    nb = n // t
    hw = MLA_HEADS * LANE

    def zrow(b, j):
        return b * nb + jnp.maximum(j - 1, 0)

    return pl.pallas_call(
        _sample_mla_kv_kernel,
        out_shape=(jax.ShapeDtypeStruct((n_batch, past + n, hw), BF16),
                   jax.ShapeDtypeStruct((n_batch, past + n, hw), BF16)),
        grid=(n_batch, nb + 1),
        in_specs=[pl.BlockSpec((1, 1, past, MLA_KV_LORA), lambda b, j: (b, layer, 0, 0)),
                  pl.BlockSpec((1, 1, past, LANE), lambda b, j: (b, layer, 0, 0)),
                  pl.BlockSpec((t, MLA_KV_LORA), lambda b, j: (zrow(b, j), C_MLA_CKV // MLA_KV_LORA)),
                  pl.BlockSpec((t, LANE), lambda b, j: (zrow(b, j), C_MLA_KR // LANE)),
                  pl.BlockSpec((t, LANE), lambda b, j: (zrow(b, j), C_MLA_KRP // LANE)),
                  pl.BlockSpec((t, LANE), lambda b, j: (jnp.maximum(j - 1, 0), 0)),
                  pl.BlockSpec((t, LANE), lambda b, j: (jnp.maximum(j - 1, 0), 0)),
                  _const_spec((1, MLA_KV_LORA)), _const_spec((MLA_KV_LORA, hw)), _const_spec((MLA_KV_LORA, hw))],
        out_specs=(pl.BlockSpec((1, t, hw), lambda b, j: (b, j, 0)),
                   pl.BlockSpec((1, t, hw), lambda b, j: (b, j, 0))),
        compiler_params=_params(("arbitrary", "arbitrary")),
        name="sample_mla_kv",
    )(cache_ckv, cache_kr_pad, z, z, z, cos_t, sin_t, g_kv, wuk, wuv)


def _sample_mla_attn_kernel(cq_ref, cos_ref, sin_ref, gq_ref, wq_ref, wqp_ref, kq_ref, vp_ref, o_ref):
    cqn = _rms(cq_ref[...], gq_ref[...]).astype(BF16)
    q = _mm(cqn, wq_ref[...])
    q_rot = _mm(cqn, wqp_ref[...])
    cos, sin = cos_ref[...], sin_ref[...]
    _mla_heads(lambda sl: (q[:, sl] * cos + q_rot[:, sl] * sin).astype(BF16), kq_ref[0], vp_ref[0], o_ref)


def _sample_mla_attn(z, cos_t, sin_t, g_q, wq, wqp, kq_all, vp_all, n_batch):
    m = z.shape[0]
    t = ROW_BLK
    nblk = m // n_batch // t
    nk = kq_all.shape[1]
    hw = MLA_HEADS * LANE
    return pl.pallas_call(
        _sample_mla_attn_kernel,
        out_shape=jax.ShapeDtypeStruct((m, MLA_W), F32),
        grid=(n_batch, nblk),
        in_specs=[pl.BlockSpec((t, MLA_Q_LORA), lambda b, i: (b * nblk + i, C_MLA_CQ // MLA_Q_LORA)),
                  pl.BlockSpec((t, LANE), lambda b, i: (i, 0)),
                  pl.BlockSpec((t, LANE), lambda b, i: (i, 0)),
                  _const_spec((1, MLA_Q_LORA)),
                  _const_spec((MLA_Q_LORA, hw)), _const_spec((MLA_Q_LORA, hw)),
                  pl.BlockSpec((1, nk, hw), lambda b, i: (b, 0, 0)),
                  pl.BlockSpec((1, nk, hw), lambda b, i: (b, 0, 0))],
        out_specs=pl.BlockSpec((t, MLA_W), lambda b, i: (b * nblk + i, 0)),
        compiler_params=_params(("arbitrary", "arbitrary")),
        name="sample_mla_attn",
    )(z, cos_t, sin_t, g_q, wq, wqp, kq_all, vp_all)


def _hg_direction(fwd, q_ref, zf_ref, v_ref, loglb_ref, l1m_ref, s_ref, o_ref, qs, ks, bs, vs):
    t = ROW_BLK
    nchunk = t // HG_CHUNK
    q = _silu(q_ref[...].astype(F32))
    zf = zf_ref[...].astype(F32)
    v = v_ref[...].astype(F32)
    soft = jnp.log1p(jnp.exp(-jnp.abs(zf)))
    a1 = jnp.broadcast_to(loglb_ref[...], zf.shape)
    a2 = l1m_ref[...] + (jnp.minimum(zf, 0.0) - soft)
    logf = jnp.maximum(a1, a2) + jnp.log1p(jnp.exp(-jnp.abs(a1 - a2)))
    k = jnp.exp(l1m_ref[...] - jnp.maximum(zf, 0.0) - soft)

    def chunk_masks(n):
        ri = lax.broadcasted_iota(jnp.int32, (n, n), 0)
        ci = lax.broadcasted_iota(jnp.int32, (n, n), 1)
        causal = (ri >= ci) if fwd else (ri <= ci)
        same_sub = (ri // HG_SUB) == (ci // HG_SUB)
        same = (ri // HG_CHUNK) == (ci // HG_CHUNK)
        return same_sub & causal, same & jnp.logical_not(same_sub) & causal, same_sub, same

    tri_sub, _, same_sub, same = chunk_masks(t)
    hi = logf.astype(BF16)
    lo = (logf - hi.astype(F32)).astype(BF16)

    def seg_sum(mask):
        m16 = jnp.where(mask, 1.0, 0.0).astype(BF16)
        return _mm(m16, hi) + _mm(m16, lo)

    b_sub = seg_sum(tri_sub)
    tot_sub = seg_sum(same_sub)
    tot = seg_sum(same)
    row = lax.broadcasted_iota(jnp.int32, (t, 1), 0) % HG_CHUNK
    later = (row >= HG_SUB) if fwd else (row < HG_SUB)
    b = b_sub + jnp.where(later, tot - tot_sub, 0.0)

    qt_sub = (q * jnp.exp(b_sub)).astype(BF16)
    kt_sub = (k * jnp.exp(jnp.minimum(-b_sub, HG_EXP_CLAMP))).astype(BF16)
    kh_sub = (k * jnp.exp(tot_sub - b_sub)).astype(BF16)
    qt16 = (q * jnp.exp(b)).astype(BF16)
    kh16 = (k * jnp.exp(tot - b)).astype(BF16)
    v16 = v.astype(BF16)

    hb = t // 2
    tri_hb, cross_hb, _, _ = chunk_masks(hb)
    outs = []
    for h in range(HG_HEADS):
        sl = slice(h * HG_DK, (h + 1) * HG_DK)
        parts = []
        for r0 in (0, hb):
            rs = slice(r0, r0 + hb)
            p = _nt(qt_sub[rs, sl], jnp.concatenate([kt_sub[rs, sl], kh_sub[rs, sl]], axis=0))
            a = jnp.where(tri_hb, p[:, :hb], 0.0) + jnp.where(cross_hb, p[:, hb:], 0.0)
            parts.append(_mm(a.astype(BF16), v16[rs, sl]))
        outs.append(jnp.concatenate(parts, axis=0))
    o_ref[...] = jnp.concatenate(outs, axis=-1)

    qs[...] = q
    ks[...] = k
    bs[...] = b
    vs[...] = v
    lane = lax.broadcasted_iota(jnp.int32, (HG_KW, LANE), 0) // HG_DK
    col = lax.broadcasted_iota(jnp.int32, (HG_KW, LANE), 1)
    head_sum = (lane == col).astype(F32)
    srow = lax.broadcasted_iota(jnp.int32, (HG_CHUNK, 1), 0)
    for c in range(nchunk):
        r0 = c * HG_CHUNK
        worst = jnp.max(-jnp.minimum(tot_sub[r0:r0 + 1, :], tot_sub[r0 + HG_SUB:r0 + HG_SUB + 1, :]))

        @pl.when(worst > HG_SAFE_DECAY)
        def _():
            kc = ks[r0:r0 + HG_CHUNK, :]
            bc = bs[r0:r0 + HG_CHUNK, :]
            vc = vs[r0:r0 + HG_CHUNK, :]

            def body(i, carry):
                qrow = qs[pl.ds(r0 + i, 1), :]
                brow = bs[pl.ds(r0 + i, 1), :]
                p = qrow * kc * jnp.exp(jnp.minimum(brow - bc, 0.0))
                keep = (srow <= i) if fwd else (srow >= i)
                p = jnp.where(keep, p, 0.0)
                a = _mm(p, head_sum, HI)
                a_full = _nt(a, head_sum, HI)
                o_ref[pl.ds(r0 + i, 1), :] = jnp.sum(a_full * vc, axis=0, keepdims=True)
                return carry

            lax.fori_loop(0, HG_CHUNK, body, 0)

    gw = 4 * HG_DK
    bd_r = lax.broadcasted_iota(jnp.int32, (gw, gw), 0) // HG_DV
    bd_c = lax.broadcasted_iota(jnp.int32, (gw, gw), 1) // HG_DK
    bd = bd_r == bd_c
    order = range(nchunk) if fwd else range(nchunk - 1, -1, -1)
    for g in range(2):
        ls = slice(g * gw, (g + 1) * gw)
        s = s_ref[g]
        for c in order:
            rs = slice(c * HG_CHUNK, (c + 1) * HG_CHUNK)
            o_ref[rs, ls] += _nt(qt16[rs, ls], s.astype(BF16))
            u = _tn(v16[rs, ls], kh16[rs, ls])
            s = s * jnp.exp(tot[c * HG_CHUNK:c * HG_CHUNK + 1, ls]) + jnp.where(bd, u, 0.0)
        s_ref[g] = s


def _head_block(h):
    g, hh = divmod(h, 4)
    return g, slice(hh * HG_DV, (hh + 1) * HG_DV), slice(hh * HG_DK, (hh + 1) * HG_DK)


def _hgrn_kernel(*refs, has_state):
    (qf_ref, ff_ref, vf_ref, qb_ref, fb_ref, vb_ref, lbf_ref, l1f_ref, lbb_ref, l1b_ref) = refs[:10]
    s0_refs = refs[10:12] if has_state else (None, None)
    of_ref, ob_ref, sf_out, sb_out, sf, sb, qs, ks, bs, vs = refs[12:] if has_state else refs[10:]
    i = pl.program_id(1)

    @pl.when(i == 0)
    def _():
        for scr, s0_ref in zip((sf, sb), s0_refs):
            scr[...] = jnp.zeros(scr.shape, F32)
            if s0_ref is not None:
                for h in range(HG_HEADS):
                    g, rv, ck = _head_block(h)
                    scr[g, rv, ck] = s0_ref[0, 0, h].T

    _hg_direction(True, qf_ref, ff_ref, vf_ref, lbf_ref, l1f_ref, sf, of_ref, qs, ks, bs, vs)
    _hg_direction(False, qb_ref, fb_ref, vb_ref, lbb_ref, l1b_ref, sb, ob_ref, qs, ks, bs, vs)

    @pl.when(i == pl.num_programs(1) - 1)
    def _():
        for scr, out in ((sf, sf_out), (sb, sb_out)):
            for h in range(HG_HEADS):
                g, rv, ck = _head_block(h)
                out[0, h] = scr[g, rv, ck].T


def _hgrn(z, n_batch, lb_f, lb_b, s0f=None, s0b=None, layer=0):
    m = z.shape[0]
    t = ROW_BLK
    nblk = m // n_batch // t
    has_state = s0f is not None
    loglb_f, l1m_f = _lb_logs(lb_f)
    loglb_b, l1m_b = _lb_logs(lb_b)
    row = lambda a: a.reshape(1, HG_KW)

    def fmap(col):
        return lambda b, i: (b * nblk + i, col // HG_KW)

    def bmap(col):
        return lambda b, i: (b * nblk + nblk - 1 - i, col // HG_KW)

    blk = lambda imap: pl.BlockSpec((t, HG_KW), imap)
    st_out = pl.BlockSpec((1, HG_HEADS, HG_DK, HG_DV), lambda b, i: (b, 0, 0, 0))
    st_shape = jax.ShapeDtypeStruct((n_batch, HG_HEADS, HG_DK, HG_DV), F32)
    st_in = pl.BlockSpec((1, 1, HG_HEADS, HG_DK, HG_DV), lambda b, i: (b, layer, 0, 0, 0))
    bd_scratch = pltpu.VMEM((2, 4 * HG_DV, 4 * HG_DK), F32)
    return pl.pallas_call(
        functools.partial(_hgrn_kernel, has_state=has_state),
        out_shape=(jax.ShapeDtypeStruct((m, HG_W), F32), jax.ShapeDtypeStruct((m, HG_W), F32), st_shape, st_shape),
        grid=(n_batch, nblk),
        in_specs=[blk(fmap(M_HG_Q)), blk(fmap(M_HG_FF)), blk(fmap(M_HG_I)),
                  blk(bmap(M_HG_Q)), blk(bmap(M_HG_FB)), blk(bmap(M_HG_I)),
                  _const_spec((1, HG_KW)), _const_spec((1, HG_KW)), _const_spec((1, HG_KW)), _const_spec((1, HG_KW))]
                 + ([st_in, st_in] if has_state else []),
        out_specs=(pl.BlockSpec((t, HG_W), lambda b, i: (b * nblk + i, 0)),
                   pl.BlockSpec((t, HG_W), lambda b, i: (b * nblk + nblk - 1 - i, 0)),
                   st_out, st_out),
        scratch_shapes=[bd_scratch, bd_scratch] + [pltpu.VMEM((t, HG_KW), F32)] * 4,
        compiler_params=_params(("arbitrary", "arbitrary")),
        name="hgrn_scan",
    )(z, z, z, z, z, z, row(loglb_f), row(l1m_f), row(loglb_b), row(l1m_b), *((s0f, s0b) if has_state else ()))


def _out_kernel(*refs, has_next):
    (x_ref, mod_ref, gpost_ref, ghg_ref, ona_ref, omla_ref, of_ref, ob_ref,
     gna_ref, gmla_ref, ghgate_ref, mg_ref, wna_ref, wmla_ref, whg_ref, wout_ref) = refs[:16]
    y_ref = refs[-2] if has_next else refs[-1]
    o = of_ref[...] + ob_ref[...]
    hr = lax.broadcasted_iota(jnp.int32, (HG_W, HG_W), 0) // HG_DV
    hc = lax.broadcasted_iota(jnp.int32, (HG_W, HG_W), 1) // HG_DV
    head_mean = jnp.where(hr == hc, 1.0 / HG_DV, 0.0).astype(BF16)
    sq = o * o
    sq_hi = sq.astype(BF16)
    sq_lo = (sq - sq_hi.astype(F32)).astype(BF16)
    ms = _mm(sq_hi, head_mean) + _mm(sq_lo, head_mean)
    o_hg = o * lax.rsqrt(ms + NORM_EPS) * ghg_ref[...]

    def branch(o_b, gate_ref, w_ref):
        return _mm((o_b * _silu(gate_ref[...].astype(F32))).astype(BF16), w_ref[...])

    def merge_gate(i):
        return jax.nn.sigmoid(mg_ref[:, i * D_MODEL:(i + 1) * D_MODEL].astype(F32))

    merged = (merge_gate(0) * branch(ona_ref[...], gna_ref, wna_ref)
              + merge_gate(1) * branch(omla_ref[...], gmla_ref, wmla_ref)
              + merge_gate(2) * branch(o_hg, ghgate_ref, whg_ref))
    out = _mm(merged.astype(BF16), wout_ref[...])
    y = x_ref[...] + mod_ref[0, 2:3, :] * _rms(out, gpost_ref[...])
    y_ref[...] = y
    if has_next:
        modn_ref, gpren_ref, hn_ref = refs[16], refs[17], refs[-1]
        hn_ref[...] = _modulated_norm(y, gpren_ref[...], modn_ref).astype(BF16)


def _out_proj(x, z_head, z_mid, z_mg, mod3, g_post, g_hg, o_na, o_mla, o_f, o_b, w_na, w_mla, w_hg, w_out,
              rows_per_mod, mod_base, next_mod3=None, next_g_pre=None):
    m = x.shape[0]
    t = ROW_BLK
    tiles_per_mod = rows_per_mod // t
    has_next = next_mod3 is not None
    w512 = lambda: pl.BlockSpec((t, NA_W), lambda i: (i, 0))
    zcol = lambda col, w: pl.BlockSpec((t, w), lambda i: (i, col // w))
    mod_spec = pl.BlockSpec((1, 3, D_MODEL), lambda i: (mod_base + i // tiles_per_mod, 0, 0))
    row_spec = pl.BlockSpec((t, D_MODEL), lambda i: (i, 0))
    one = pl.Buffered(1)
    out = pl.pallas_call(
        functools.partial(_out_kernel, has_next=has_next),
        out_shape=(jax.ShapeDtypeStruct((m, D_MODEL), F32),)
                  + ((jax.ShapeDtypeStruct((m, D_MODEL), BF16),) if has_next else ()),
        grid=(m // t,),
        in_specs=[row_spec, mod_spec,
                  _const_spec((1, D_MODEL)), _const_spec((1, HG_W)),
                  w512(), w512(), w512(), w512(),
                  zcol(C_NA_G, NA_W), zcol(M_MLA_G, MLA_W), zcol(M_HG_G, HG_W),
                  zcol(0, MG_W),
                  pl.BlockSpec((NA_W, D_MODEL), lambda i: (0, 0), pipeline_mode=one),
                  pl.BlockSpec((MLA_W, D_MODEL), lambda i: (0, 0), pipeline_mode=one),
                  pl.BlockSpec((HG_W, D_MODEL), lambda i: (0, 0), pipeline_mode=one),
                  pl.BlockSpec((D_MODEL, D_MODEL), lambda i: (0, 0), pipeline_mode=one)]
                 + ([mod_spec, _const_spec((1, D_MODEL))] if has_next else []),
        out_specs=(row_spec,) + ((row_spec,) if has_next else ()),
        compiler_params=_params(("arbitrary",)),
        name="out_proj",
    )(x, mod3, g_post, g_hg, o_na, o_mla, o_f, o_b, z_head, z_mid, z_mid, z_mg, w_na, w_mla, w_hg, w_out,
      *((next_mod3, next_g_pre) if has_next else ()))
    return out if has_next else (out[0], None)


def kernel(x_prompt, x_sample, cache_na_k, cache_na_v, cache_mla_ckv, cache_mla_krope, state_hgrn_fwd, state_hgrn_bwd, c, c_ctx, w_ada, b_ada, g_pre, g_post, w_in, na_rpb, g_mla_q, w_mla_uq, g_mla_kv, w_mla_ukv, hg_lb_fwd, hg_lb_bwd, g_hg_out, w_br_na, w_br_mla, w_br_hg, w_out):
    bp, sp, _ = x_prompt.shape
    bs, ss, _ = x_sample.shape
    past = cache_na_k.shape[2]

    cvec = jnp.concatenate([c_ctx[None, :], c, jnp.zeros((8 - 1 - bs, D_MODEL), F32)], axis=0)
    mod = _modulation(cvec, w_ada, b_ada).reshape(DEPTH, 8, 3, D_MODEL)

    lb_f_all = _lower_bounds(hg_lb_fwd)
    lb_b_all = _lower_bounds(hg_lb_bwd)
    cos_t, sin_t = _rope_tables(ss)
    cache_k = cache_na_k.reshape(bs, DEPTH, past, NA_W)
    cache_v = cache_na_v.reshape(bs, DEPTH, past, NA_W)
    cache_kr_pad = _rope_lanes(cache_mla_krope)

    yp = x_prompt.reshape(bp * sp, D_MODEL)
    ys = x_sample.reshape(bs * ss, D_MODEL)
    new_k, new_v, new_ckv, new_kr, new_sf, new_sb = [], [], [], [], [], []
    w_head, w_tail = _pack_w_in(w_in)
    hp = _prenorm(yp, mod[0], g_pre[0][None, :], bp * sp, 0)
    hs = _prenorm(ys, mod[0], g_pre[0][None, :], ss, 1)
    for l in range(DEPTH):
        wq, wqp = _pack_w_uq(w_mla_uq[l])
        wuk, wuv = _pack_w_ukv(w_mla_ukv[l])
        w_na, w_mla, w_hg, w_o = (w.astype(BF16) for w in (w_br_na[l], w_br_mla[l], w_br_hg[l], w_out[l]))
        g_q, g_kv = g_mla_q[l][None, :], g_mla_kv[l][None, :]
        gpost, ghg = g_post[l][None, :], g_hg_out[l][None, :]
        mod3 = mod[l]
        nxt = (mod[l + 1], g_pre[l + 1][None, :]) if l + 1 < DEPTH else (None, None)

        zp_head, zp_mid, zp_mg = _in_proj(hp, w_head, w_tail, l)
        o_na, o_mla, ckvn, k_new, v_new, kr_new = _prompt_attn(zp_head, bp, g_q, g_kv, wq, wuk, wuv)
        o_f, o_b, sf, sb = _hgrn(zp_mid, bp, lb_f_all[l], lb_b_all[l])
        yp, hp = _out_proj(yp, zp_head, zp_mid, zp_mg, mod3, gpost, ghg, o_na, o_mla, o_f, o_b,
                           w_na, w_mla, w_hg, w_o, bp * sp, 0, *nxt)
        new_k.append(k_new.reshape(bp, sp, NA_HEADS, NA_HD))
        new_v.append(v_new.reshape(bp, sp, NA_HEADS, NA_HD))
        new_ckv.append(ckvn.reshape(bp, sp, MLA_KV_LORA))
        new_kr.append(kr_new.reshape(bp, sp, MLA_ROPE))
        new_sf.append(sf)
        new_sb.append(sb)

        zs_head, zs_mid, zs_mg = _in_proj(hs, w_head, w_tail, l)
        o_na = _sample_na(zs_head, cache_k, cache_v, _na_bias_table(na_rpb[l]), l, bs)
        kq_all, vp_all = _sample_mla_kv(zs_head, cache_mla_ckv, cache_kr_pad, cos_t, sin_t, g_kv, wuk, wuv, l, bs)
        o_mla = _sample_mla_attn(zs_head, cos_t, sin_t, g_q, wq, wqp, kq_all, vp_all, bs)
        o_f, o_b, _, _ = _hgrn(zs_mid, bs, lb_f_all[l], lb_b_all[l], state_hgrn_fwd, state_hgrn_bwd, l)
        ys, hs = _out_proj(ys, zs_head, zs_mid, zs_mg, mod3, gpost, ghg, o_na, o_mla, o_f, o_b,
                           w_na, w_mla, w_hg, w_o, ss, 1, *nxt)

    return (yp.reshape(bp, sp, D_MODEL), ys.reshape(bs, ss, D_MODEL),
            jnp.stack(new_k, axis=1), jnp.stack(new_v, axis=1), jnp.stack(new_ckv, axis=1),
            jnp.stack(new_kr, axis=1), jnp.stack(new_sf, axis=1), jnp.stack(new_sb, axis=1))
```

```python
import functools

import numpy as np
import jax
import jax.numpy as jnp
from jax import lax
from jax.experimental import pallas as pl
from jax.experimental.pallas import tpu as pltpu

D_MODEL = 2048
DEPTH = 2
GRID_W = 64
NORM_EPS = 1e-6
NEG_BIG = -1e30
NA_HEADS = 8
NA_HD = 64
NA_W = NA_HEADS * NA_HD
NA_KR = 8
NA_KW = 16
MLA_HEADS = 8
MLA_NOPE = 64
MLA_ROPE = 32
MLA_VD = 64
MLA_QK = MLA_NOPE + MLA_ROPE
MLA_W = MLA_HEADS * MLA_VD
MLA_Q_LORA = 512
MLA_KV_LORA = 256
ROPE_BASE = 10000.0
HG_HEADS = 8
HG_DK = 64
HG_DV = 64
HG_KW = HG_HEADS * HG_DK
HG_W = HG_HEADS * HG_DV
N_BRANCH = 3
PROJ_WIDTHS = (NA_W, NA_W, NA_W, NA_W, MLA_Q_LORA, MLA_KV_LORA, MLA_ROPE, MLA_W,
               HG_KW, HG_KW, HG_KW, HG_W, HG_W, N_BRANCH * D_MODEL)

F32 = jnp.float32
BF16 = jnp.bfloat16
HI = lax.Precision.HIGHEST

LANE = 128
ROW_BLK = 256
HG_CHUNK = 32
HG_SUB = 16
HG_SAFE_DECAY = 72.0
HG_EXP_CLAMP = 80.0
VMEM_LIMIT = 56 * 1024 * 1024

HEAD_W = 3072
MID_W = 3072
MG_W = N_BRANCH * D_MODEL
SPLIT_COL = 2848
C_NA_Q, C_NA_K, C_NA_V, C_NA_G = 0, 512, 1024, 1536
C_MLA_CQ, C_MLA_CKV, C_MLA_KR, C_MLA_KRP = 2048, 2560, 2816, 2944
M_MLA_G, M_HG_Q, M_HG_FF, M_HG_FB, M_HG_I, M_HG_G = 0, 512, 1024, 1536, 2048, 2560


def _nt(a, b, precision=None):
    return lax.dot_general(a, b, (((1,), (1,)), ((), ())), preferred_element_type=F32, precision=precision)


def _tn(a, b, precision=None):
    return lax.dot_general(a, b, (((0,), (0,)), ((), ())), preferred_element_type=F32, precision=precision)


def _mm(a, b, precision=None):
    return jnp.dot(a, b, preferred_element_type=F32, precision=precision)


def _rms(x, g):
    return x * lax.rsqrt(jnp.mean(x * x, axis=-1, keepdims=True) + NORM_EPS) * g


def _silu(x):
    return x * jax.nn.sigmoid(x)


def _params(sem):
    return pltpu.CompilerParams(dimension_semantics=sem, vmem_limit_bytes=VMEM_LIMIT)


def _const_spec(shape):
    nd = len(shape)
    return pl.BlockSpec(shape, lambda *_: (0,) * nd)


def _rot_half(a):
    parts = []
    for ax in range(2):
        lo_, hi_ = a[..., ax * 16:ax * 16 + 8], a[..., ax * 16 + 8:ax * 16 + 16]
        parts += [-hi_, lo_]
    return jnp.concatenate(parts, axis=-1)


def _pack_w_in(w):
    assert sum(PROJ_WIDTHS[:7]) == SPLIT_COL and sum(PROJ_WIDTHS[7:13]) == MID_W and PROJ_WIDTHS[13] == MG_W
    head = w[..., :SPLIT_COL].astype(BF16)
    kr = head[..., C_MLA_KR:SPLIT_COL]
    return jnp.concatenate([head[..., :C_MLA_KR], _rope_lanes(kr), _rope_lanes(_rot_half(kr))], axis=-1), \
        w[..., SPLIT_COL:].astype(BF16)


def _rope_lanes(a):
    z = lambda n: jnp.zeros(a.shape[:-1] + (n,), a.dtype)
    return jnp.concatenate([z(MLA_NOPE), a, z(LANE - MLA_QK)], axis=-1)


def _pack_w_uq(w):
    w3 = w.reshape(MLA_Q_LORA, MLA_HEADS, MLA_QK)
    nope, rope = w3[..., :MLA_NOPE], w3[..., MLA_NOPE:]
    pad = jnp.zeros(nope.shape[:-1] + (LANE - MLA_QK,), w.dtype)
    flat = lambda a: a.reshape(MLA_Q_LORA, MLA_HEADS * LANE).astype(BF16)
    return flat(jnp.concatenate([nope, rope, pad], axis=-1)), flat(_rope_lanes(_rot_half(rope)))


def _pack_w_ukv(w):
    w3 = w.reshape(MLA_KV_LORA, MLA_HEADS, MLA_NOPE + MLA_VD)
    pad = jnp.zeros((MLA_KV_LORA, MLA_HEADS, LANE - MLA_NOPE), w.dtype)
    flat = lambda a: a.reshape(MLA_KV_LORA, MLA_HEADS * LANE).astype(BF16)
    return flat(jnp.concatenate([w3[..., :MLA_NOPE], pad], -1)), flat(jnp.concatenate([w3[..., MLA_NOPE:], pad], -1))


def _rope_tables(n):
    t = np.arange(n)
    pos = np.stack([t // GRID_W, t % GRID_W]).astype(np.float32)
    axis_dim = MLA_ROPE // 2
    inv = ROPE_BASE ** (-jnp.arange(0, axis_dim, 2, dtype=F32) / axis_dim)
    ang = jnp.asarray(pos)[:, :, None] * inv
    ang = jnp.concatenate([ang, ang], axis=-1)
    ang = jnp.concatenate([ang[0], ang[1]], axis=-1)
    cos_t = jnp.concatenate([jnp.ones((n, MLA_NOPE), F32), jnp.cos(ang), jnp.zeros((n, LANE - MLA_QK), F32)], -1)
    return cos_t, _rope_lanes(jnp.sin(ang))


def _na_bias_table(rpb):
    rows = 32
    c = np.arange(GRID_W)
    win0 = np.clip(c - NA_KW // 2, 0, GRID_W - NA_KW)
    kc = np.arange(GRID_W)
    col_ok = (kc[None, :] >= win0[:, None]) & (kc[None, :] < win0[:, None] + NA_KW)
    rpb = rpb.astype(F32)
    edge = GRID_W - NA_KW
    ext = jnp.concatenate([jnp.broadcast_to(rpb[..., :1], rpb.shape[:-1] + (edge,)), rpb,
                           jnp.broadcast_to(rpb[..., -1:], rpb.shape[:-1] + (edge,))], axis=-1)
    toep = jnp.stack([ext[..., GRID_W - 1 - ci:2 * GRID_W - 1 - ci] for ci in range(GRID_W)], axis=2)
    toep = jnp.where(jnp.asarray(col_ok), toep, NEG_BIG)
    masked = jnp.full((NA_HEADS, GRID_W, GRID_W), NEG_BIG, F32)
    out = []
    for start, r0 in ((0, 0), (4, 8), (20, 28)):
        per_q = []
        for qr in range(4):
            r = r0 + qr
            kr0 = min(max(r - NA_KR // 2, 0), rows - NA_KR)
            tiles = []
            for j in range(12):
                kabs = start + j
                ok = kr0 <= kabs < kr0 + NA_KR
                tiles.append(toep[:, kabs - r + NA_KR - 1] if ok else masked)
            per_q.append(jnp.concatenate(tiles, axis=-1))
        out.append(jnp.concatenate(per_q, axis=1))
    return jnp.stack(out)


def _lower_bounds(p):
    s = jax.nn.softmax(p.astype(F32), axis=0)
    return jnp.cumsum(s, axis=0) - s[0]


def _lb_logs(lb):
    pos = lb > 0
    log_lb = jnp.where(pos, jnp.log(jnp.where(pos, lb, 1.0)), NEG_BIG)
    return log_lb, jnp.log1p(-lb)


def _mod_kernel(c_ref, w_ref, b_ref, o_ref):
    s = _silu(c_ref[...])
    o_ref[0] = _mm(s, w_ref[0], HI) + b_ref[0]


def _modulation(cvec, w_ada, b_ada):
    tn = 1024
    n3 = 3 * D_MODEL
    return pl.pallas_call(
        _mod_kernel,
        out_shape=jax.ShapeDtypeStruct((DEPTH, 8, n3), F32),
        grid=(DEPTH, n3 // tn),
        in_specs=[pl.BlockSpec((8, D_MODEL), lambda l, j: (0, 0)),
                  pl.BlockSpec((1, D_MODEL, tn), lambda l, j: (l, 0, j)),
                  pl.BlockSpec((1, 1, tn), lambda l, j: (l, 0, j))],
        out_specs=pl.BlockSpec((1, 8, tn), lambda l, j: (l, 0, j)),
        compiler_params=_params(("arbitrary", "arbitrary")),
        name="adaln_mod",
    )(cvec, w_ada, b_ada.reshape(DEPTH, 1, n3))


IN_TM, IN_TN = 2048, 1024


def _modulated_norm(x, g, mod_ref):
    return _rms(x, g) * (1.0 + mod_ref[0, 1:2, :]) + mod_ref[0, 0:1, :]


def _prenorm_kernel(x_ref, mod_ref, g_ref, h_ref):
    h_ref[...] = _modulated_norm(x_ref[...], g_ref[...], mod_ref).astype(BF16)


def _prenorm(x, mod3, g_pre, rows_per_mod, mod_base):
    m = x.shape[0]
    t = ROW_BLK
    tiles_per_mod = rows_per_mod // t
    return pl.pallas_call(
        _prenorm_kernel,
        out_shape=jax.ShapeDtypeStruct((m, D_MODEL), BF16),
        grid=(m // t,),
        in_specs=[pl.BlockSpec((t, D_MODEL), lambda i: (i, 0)),
                  pl.BlockSpec((1, 3, D_MODEL), lambda i: (mod_base + i // tiles_per_mod, 0, 0)),
                  _const_spec((1, D_MODEL))],
        out_specs=pl.BlockSpec((t, D_MODEL), lambda i: (i, 0)),
        compiler_params=_params(("arbitrary",)),
        name="prenorm",
    )(x, mod3, g_pre)


def _proj_kernel(h_ref, w_ref, z_ref):
    z_ref[...] = _mm(h_ref[...], w_ref[0]).astype(z_ref.dtype)


def _proj(h, w, layer, tile0, width, out_dtype):
    m = h.shape[0]
    tm, tn = IN_TM, IN_TN
    return pl.pallas_call(
        _proj_kernel,
        out_shape=jax.ShapeDtypeStruct((m, width), out_dtype),
        grid=(m // tm, width // tn),
        in_specs=[pl.BlockSpec((tm, D_MODEL), lambda i, j: (i, 0)),
                  pl.BlockSpec((1, D_MODEL, tn), lambda i, j: (layer, 0, tile0 + j))],
        out_specs=pl.BlockSpec((tm, tn), lambda i, j: (i, j)),
        compiler_params=_params(("arbitrary", "arbitrary")),
        name="in_proj",
    )(h, w)


def _in_proj(h, w_head, w_tail, layer):
    z_head = _proj(h, w_head, layer, 0, HEAD_W, F32)
    z_mid = _proj(h, w_tail, layer, 0, MID_W, BF16)
    z_mg = _proj(h, w_tail, layer, MID_W // IN_TN, MG_W, BF16)
    return z_head, z_mid, z_mg


def _softmax_pv(s_list, v_list):
    m = s_list[0].max(axis=-1, keepdims=True)
    for s in s_list[1:]:
        m = jnp.maximum(m, s.max(axis=-1, keepdims=True))
    acc, den = None, None
    for s, v in zip(s_list, v_list):
        e = jnp.exp(s - m)
        d = e.sum(axis=-1, keepdims=True)
        o = _mm(e.astype(BF16), v)
        acc = o if acc is None else acc + o
        den = d if den is None else den + d
    return acc, den


def _ones_lane():
    return (lax.broadcasted_iota(jnp.int32, (1, LANE), 1) == MLA_VD).astype(F32)


def _mla_heads(q_fn, kq, vp, o_ref):
    scale = MLA_QK ** -0.5
    outs = []
    for h in range(MLA_HEADS):
        sl = slice(h * LANE, (h + 1) * LANE)
        s = _nt(q_fn(sl), kq[:, sl]) * scale
        e = jnp.exp(s - s.max(axis=-1, keepdims=True))
        acc = _mm(e.astype(BF16), vp[:, sl])
        outs.append(acc[:, :MLA_VD] / acc[:, MLA_VD:MLA_VD + 1])
    o_ref[...] = jnp.concatenate(outs, axis=-1)


def _mla_keys_values(ckvn16, kr_tile, wuk_ref, wuv_ref):
    kq = _mm(ckvn16, wuk_ref[...]) + jnp.concatenate([kr_tile] * MLA_HEADS, axis=-1)
    vp = _mm(ckvn16, wuv_ref[...]) + jnp.concatenate([_ones_lane()] * MLA_HEADS, axis=-1)
    return kq.astype(BF16), vp.astype(BF16)


def _prompt_attn_kernel(qkv_ref, cq_ref, ckv_ref, kr_ref, gq_ref, gkv_ref, wq_ref, wuk_ref, wuv_ref,
                        ona_ref, omla_ref, ckvn_ref, k_ref, v_ref, kro_ref):
    qkv = qkv_ref[...]
    k_ref[...] = qkv[:, NA_W:2 * NA_W]
    v_ref[...] = qkv[:, 2 * NA_W:3 * NA_W]
    kro_ref[...] = kr_ref[:, MLA_NOPE:MLA_QK]
    scale = NA_HD ** -0.5
    outs = []
    for h in range(NA_HEADS):
        qh = qkv[:, h * NA_HD:(h + 1) * NA_HD].astype(BF16)
        kh = qkv[:, NA_W + h * NA_HD:NA_W + (h + 1) * NA_HD].astype(BF16)
        vh = qkv[:, 2 * NA_W + h * NA_HD:2 * NA_W + (h + 1) * NA_HD].astype(BF16)
        acc, den = _softmax_pv([_nt(qh, kh) * scale], [vh])
        outs.append(acc / den)
    ona_ref[...] = jnp.concatenate(outs, axis=-1)

    cqn = _rms(cq_ref[...], gq_ref[...]).astype(BF16)
    q = _mm(cqn, wq_ref[...]).astype(BF16)
    ckvn = _rms(ckv_ref[...], gkv_ref[...])
    ckvn_ref[...] = ckvn
    kq, vp = _mla_keys_values(ckvn.astype(BF16), kr_ref[...], wuk_ref, wuv_ref)
    _mla_heads(lambda sl: q[:, sl], kq, vp, omla_ref)


def _prompt_attn(z, n_batch, g_q, g_kv, wq, wuk, wuv):
    m = z.shape[0]
    t = ROW_BLK
    hw = MLA_HEADS * LANE
    return pl.pallas_call(
        _prompt_attn_kernel,
        out_shape=(jax.ShapeDtypeStruct((m, NA_W), F32), jax.ShapeDtypeStruct((m, MLA_W), F32),
                   jax.ShapeDtypeStruct((m, MLA_KV_LORA), F32), jax.ShapeDtypeStruct((m, NA_W), F32),
                   jax.ShapeDtypeStruct((m, NA_W), F32), jax.ShapeDtypeStruct((m, MLA_ROPE), F32)),
        grid=(n_batch,),
        in_specs=[pl.BlockSpec((t, 3 * NA_W), lambda b: (b, C_NA_Q // (3 * NA_W))),
                  pl.BlockSpec((t, MLA_Q_LORA), lambda b: (b, C_MLA_CQ // MLA_Q_LORA)),
                  pl.BlockSpec((t, MLA_KV_LORA), lambda b: (b, C_MLA_CKV // MLA_KV_LORA)),
                  pl.BlockSpec((t, LANE), lambda b: (b, C_MLA_KR // LANE)),
                  _const_spec((1, MLA_Q_LORA)), _const_spec((1, MLA_KV_LORA)),
                  _const_spec((MLA_Q_LORA, hw)), _const_spec((MLA_KV_LORA, hw)), _const_spec((MLA_KV_LORA, hw))],
        out_specs=(pl.BlockSpec((t, NA_W), lambda b: (b, 0)), pl.BlockSpec((t, MLA_W), lambda b: (b, 0)),
                   pl.BlockSpec((t, MLA_KV_LORA), lambda b: (b, 0)), pl.BlockSpec((t, NA_W), lambda b: (b, 0)),
                   pl.BlockSpec((t, NA_W), lambda b: (b, 0)), pl.BlockSpec((t, MLA_ROPE), lambda b: (b, 0))),
        compiler_params=_params(("arbitrary",)),
        name="prompt_attn",
    )(z, z, z, z, g_q, g_kv, wq, wuk, wuv)


def _sample_na_kernel(q_ref, k0_ref, k1_ref, k2_ref, v0_ref, v1_ref, v2_ref, kc_ref, vc_ref, bias_ref, o_ref):
    scale = NA_HD ** -0.5
    q = q_ref[...]
    kl = [r[...] for r in (k0_ref, k1_ref, k2_ref)]
    vl = [r[...] for r in (v0_ref, v1_ref, v2_ref)]
    kc, vc = kc_ref[0, 0], vc_ref[0, 0]
    outs = []
    for h in range(NA_HEADS):
        sl = slice(h * NA_HD, (h + 1) * NA_HD)
        qh = q[:, sl].astype(BF16)
        s_list = [_nt(qh, kl[i][:, sl].astype(BF16)) * scale + bias_ref[0, h, :, i * ROW_BLK:(i + 1) * ROW_BLK]
                  for i in range(3)]
        s_list.append(_nt(qh, kc[:, sl].astype(BF16)) * scale)
        v_list = [vl[i][:, sl].astype(BF16) for i in range(3)] + [vc[:, sl].astype(BF16)]
        acc, den = _softmax_pv(s_list, v_list)
        outs.append(acc / den)
    o_ref[...] = jnp.concatenate(outs, axis=-1)


def _sample_na(z, cache_k, cache_v, bias_tab, layer, n_batch):
    m = z.shape[0]
    t = ROW_BLK
    nblk = m // n_batch // t
    past = cache_k.shape[2]

    def kv_map(col, i):
        return lambda b, rb: (b * nblk + jnp.clip(rb - 1, 0, nblk - 3) + i, col // NA_W)

    def variant(b, rb):
        return (jnp.where(rb == 0, 0, jnp.where(rb == nblk - 1, 2, 1)), 0, 0, 0)

    cache_spec = pl.BlockSpec((1, 1, past, NA_W), lambda b, rb: (b, layer, 0, 0))
    return pl.pallas_call(
        _sample_na_kernel,
        out_shape=jax.ShapeDtypeStruct((m, NA_W), F32),
        grid=(n_batch, nblk),
        in_specs=[pl.BlockSpec((t, NA_W), lambda b, rb: (b * nblk + rb, C_NA_Q // NA_W))]
                 + [pl.BlockSpec((t, NA_W), kv_map(C_NA_K, i)) for i in range(3)]
                 + [pl.BlockSpec((t, NA_W), kv_map(C_NA_V, i)) for i in range(3)]
                 + [cache_spec, cache_spec,
                    pl.BlockSpec((1, NA_HEADS, t, 3 * t), variant)],
        out_specs=pl.BlockSpec((t, NA_W), lambda b, rb: (b * nblk + rb, 0)),
        compiler_params=_params(("arbitrary", "arbitrary")),
        name="sample_na",
    )(z, z, z, z, z, z, z, cache_k, cache_v, bias_tab)


def _sample_mla_kv_kernel(cckv_ref, ckr_ref, ckv_ref, kr_ref, krp_ref, cos_ref, sin_ref, gkv_ref, wuk_ref, wuv_ref,
                          kq_ref, vp_ref):
    j = pl.program_id(1)

    @pl.when(j == 0)
    def _():
        kq_ref[0], vp_ref[0] = _mla_keys_values(cckv_ref[0, 0].astype(BF16), ckr_ref[0, 0], wuk_ref, wuv_ref)

    @pl.when(j > 0)
    def _():
        ckvn = _rms(ckv_ref[...], gkv_ref[...])
        kr_roped = kr_ref[...] * cos_ref[...] + krp_ref[...] * sin_ref[...]
        kq_ref[0], vp_ref[0] = _mla_keys_values(ckvn.astype(BF16), kr_roped, wuk_ref, wuv_ref)


def _sample_mla_kv(z, cache_ckv, cache_kr_pad, cos_t, sin_t, g_kv, wuk, wuv, layer, n_batch):
    m = z.shape[0]
    n = m // n_batch
    past = cache_ckv.shape[2]
    t = past
    nb = n // t
    hw = MLA_HEADS * LANE

    def zrow(b, j):
        return b * nb + jnp.maximum(j - 1, 0)

    return pl.pallas_call(
        _sample_mla_kv_kernel,
        out_shape=(jax.ShapeDtypeStruct((n_batch, past + n, hw), BF16),
                   jax.ShapeDtypeStruct((n_batch, past + n, hw), BF16)),
        grid=(n_batch, nb + 1),
        in_specs=[pl.BlockSpec((1, 1, past, MLA_KV_LORA), lambda b, j: (b, layer, 0, 0)),
                  pl.BlockSpec((1, 1, past, LANE), lambda b, j: (b, layer, 0, 0)),
                  pl.BlockSpec((t, MLA_KV_LORA), lambda b, j: (zrow(b, j), C_MLA_CKV // MLA_KV_LORA)),
                  pl.BlockSpec((t, LANE), lambda b, j: (zrow(b, j), C_MLA_KR // LANE)),
                  pl.BlockSpec((t, LANE), lambda b, j: (zrow(b, j), C_MLA_KRP // LANE)),
                  pl.BlockSpec((t, LANE), lambda b, j: (jnp.maximum(j - 1, 0), 0)),
                  pl.BlockSpec((t, LANE), lambda b, j: (jnp.maximum(j - 1, 0), 0)),
                  _const_spec((1, MLA_KV_LORA)), _const_spec((MLA_KV_LORA, hw)), _const_spec((MLA_KV_LORA, hw))],
        out_specs=(pl.BlockSpec((1, t, hw), lambda b, j: (b, j, 0)),
                   pl.BlockSpec((1, t, hw), lambda b, j: (b, j, 0))),
        compiler_params=_params(("arbitrary", "arbitrary")),
        name="sample_mla_kv",
    )(cache_ckv, cache_kr_pad, z, z, z, cos_t, sin_t, g_kv, wuk, wuv)


def _sample_mla_attn_kernel(cq_ref, cos_ref, sin_ref, gq_ref, wq_ref, wqp_ref, kq_ref, vp_ref, o_ref):
    cqn = _rms(cq_ref[...], gq_ref[...]).astype(BF16)
    q = _mm(cqn, wq_ref[...])
    q_rot = _mm(cqn, wqp_ref[...])
    cos, sin = cos_ref[...], sin_ref[...]
    _mla_heads(lambda sl: (q[:, sl] * cos + q_rot[:, sl] * sin).astype(BF16), kq_ref[0], vp_ref[0], o_ref)


def _sample_mla_attn(z, cos_t, sin_t, g_q, wq, wqp, kq_all, vp_all, n_batch):
    m = z.shape[0]
    t = ROW_BLK
    nblk = m // n_batch // t
    nk = kq_all.shape[1]
    hw = MLA_HEADS * LANE
    return pl.pallas_call(
        _sample_mla_attn_kernel,
        out_shape=jax.ShapeDtypeStruct((m, MLA_W), F32),
        grid=(n_batch, nblk),
        in_specs=[pl.BlockSpec((t, MLA_Q_LORA), lambda b, i: (b * nblk + i, C_MLA_CQ // MLA_Q_LORA)),
                  pl.BlockSpec((t, LANE), lambda b, i: (i, 0)),
                  pl.BlockSpec((t, LANE), lambda b, i: (i, 0)),
                  _const_spec((1, MLA_Q_LORA)),
                  _const_spec((MLA_Q_LORA, hw)), _const_spec((MLA_Q_LORA, hw)),
                  pl.BlockSpec((1, nk, hw), lambda b, i: (b, 0, 0)),
                  pl.BlockSpec((1, nk, hw), lambda b, i: (b, 0, 0))],
        out_specs=pl.BlockSpec((t, MLA_W), lambda b, i: (b * nblk + i, 0)),
        compiler_params=_params(("arbitrary", "arbitrary")),
        name="sample_mla_attn",
    )(z, cos_t, sin_t, g_q, wq, wqp, kq_all, vp_all)


def _hg_direction(fwd, q_ref, zf_ref, v_ref, loglb_ref, l1m_ref, s_ref, o_ref, qs, ks, bs, vs, os):
    t = ROW_BLK
    nchunk = t // HG_CHUNK
    q = _silu(q_ref[...].astype(F32))
    zf = zf_ref[...].astype(F32)
    v = v_ref[...].astype(F32)
    soft = jnp.log1p(jnp.exp(-jnp.abs(zf)))
    a1 = jnp.broadcast_to(loglb_ref[...], zf.shape)
    a2 = l1m_ref[...] + (jnp.minimum(zf, 0.0) - soft)
    logf = jnp.maximum(a1, a2) + jnp.log1p(jnp.exp(-jnp.abs(a1 - a2)))
    k = jnp.exp(l1m_ref[...] - jnp.maximum(zf, 0.0) - soft)

    def chunk_masks(n):
        ri = lax.broadcasted_iota(jnp.int32, (n, n), 0)
        ci = lax.broadcasted_iota(jnp.int32, (n, n), 1)
        causal = (ri >= ci) if fwd else (ri <= ci)
        same_sub = (ri // HG_SUB) == (ci // HG_SUB)
        same = (ri // HG_CHUNK) == (ci // HG_CHUNK)
        return same_sub & causal, same & jnp.logical_not(same_sub) & causal, same_sub, same

    tri_sub, _, same_sub, same = chunk_masks(t)
    hi = logf.astype(BF16)
    lo = (logf - hi.astype(F32)).astype(BF16)

    def seg_sum(mask):
        m16 = jnp.where(mask, 1.0, 0.0).astype(BF16)
        return _mm(m16, hi) + _mm(m16, lo)

    b_sub = seg_sum(tri_sub)
    tot_sub = seg_sum(same_sub)
    tot = seg_sum(same)
    row = lax.broadcasted_iota(jnp.int32, (t, 1), 0) % HG_CHUNK
    later = (row >= HG_SUB) if fwd else (row < HG_SUB)
    b = b_sub + jnp.where(later, tot - tot_sub, 0.0)

    qt_sub = (q * jnp.exp(b_sub)).astype(BF16)
    kt_sub = (k * jnp.exp(jnp.minimum(-b_sub, HG_EXP_CLAMP))).astype(BF16)
    kh_sub = (k * jnp.exp(tot_sub - b_sub)).astype(BF16)
    qt16 = (q * jnp.exp(b)).astype(BF16)
    kh16 = (k * jnp.exp(tot - b)).astype(BF16)
    v16 = v.astype(BF16)

    hb = t // 2
    tri_hb, cross_hb, _, _ = chunk_masks(hb)
    outs = []
    for h in range(HG_HEADS):
        sl = slice(h * HG_DK, (h + 1) * HG_DK)
        parts = []
        for r0 in (0, hb):
            rs = slice(r0, r0 + hb)
            p = _nt(qt_sub[rs, sl], jnp.concatenate([kt_sub[rs, sl], kh_sub[rs, sl]], axis=0))
            a = jnp.where(tri_hb, p[:, :hb], 0.0) + jnp.where(cross_hb, p[:, hb:], 0.0)
            parts.append(_mm(a.astype(BF16), v16[rs, sl]))
        outs.append(jnp.concatenate(parts, axis=0))
    o_ref[...] = jnp.concatenate(outs, axis=-1)

    gw = 4 * HG_DK
    bd_r = lax.broadcasted_iota(jnp.int32, (gw, gw), 0) // HG_DV
    bd_c = lax.broadcasted_iota(jnp.int32, (gw, gw), 1) // HG_DK
    bd = bd_r == bd_c
    order = range(nchunk) if fwd else range(nchunk - 1, -1, -1)
    for g in range(2):
        ls = slice(g * gw, (g + 1) * gw)
        s = s_ref[g]
        for c in order:
            rs = slice(c * HG_CHUNK, (c + 1) * HG_CHUNK)
            inter = _nt(qt16[rs, ls], s.astype(BF16))
            o_ref[rs, ls] += inter
            os[rs, ls] = inter
            u = _tn(v16[rs, ls], kh16[rs, ls])
            s = s * jnp.exp(tot[c * HG_CHUNK:c * HG_CHUNK + 1, ls]) + jnp.where(bd, u, 0.0)
        s_ref[g] = s

    qs[...] = q
    ks[...] = k
    bs[...] = b
    vs[...] = v
    return [jnp.max(-jnp.minimum(tot_sub[c * HG_CHUNK:c * HG_CHUNK + 1, :],
                                 tot_sub[c * HG_CHUNK + HG_SUB:c * HG_CHUNK + HG_SUB + 1, :]))
            for c in range(nchunk)]


def _hg_pairwise(fwd, worst, o_ref, qs, ks, bs, vs, os):
    lane = lax.broadcasted_iota(jnp.int32, (HG_KW, LANE), 0) // HG_DK
    col = lax.broadcasted_iota(jnp.int32, (HG_KW, LANE), 1)
    head_sum = (lane == col).astype(F32)
    srow = lax.broadcasted_iota(jnp.int32, (HG_CHUNK, 1), 0)
    for c, worst_c in enumerate(worst):
        r0 = c * HG_CHUNK

        @pl.when(worst_c > HG_SAFE_DECAY)
        def _():
            kc = ks[r0:r0 + HG_CHUNK, :]
            bc = bs[r0:r0 + HG_CHUNK, :]
            vc = vs[r0:r0 + HG_CHUNK, :]

            def body(i, carry):
                qrow = qs[pl.ds(r0 + i, 1), :]
                brow = bs[pl.ds(r0 + i, 1), :]
                p = qrow * kc * jnp.exp(jnp.minimum(brow - bc, 0.0))
                keep = (srow <= i) if fwd else (srow >= i)
                p = jnp.where(keep, p, 0.0)
                a = _mm(p, head_sum, HI)
                a_full = _nt(a, head_sum, HI)
                o_ref[pl.ds(r0 + i, 1), :] = (jnp.sum(a_full * vc, axis=0, keepdims=True)
                                              + os[pl.ds(r0 + i, 1), :])
                return carry

            lax.fori_loop(0, HG_CHUNK, body, 0)


def _head_block(h):
    g, hh = divmod(h, 4)
    return g, slice(hh * HG_DV, (hh + 1) * HG_DV), slice(hh * HG_DK, (hh + 1) * HG_DK)


def _hgrn_kernel(*refs, has_state):
    (qf_ref, ff_ref, vf_ref, qb_ref, fb_ref, vb_ref, lbf_ref, l1f_ref, lbb_ref, l1b_ref) = refs[:10]
    s0_refs = refs[10:12] if has_state else (None, None)
    rest = refs[12:] if has_state else refs[10:]
    of_ref, ob_ref, sf_out, sb_out, sf, sb = rest[:6]
    scr_f, scr_b = rest[6:11], rest[11:16]
    i = pl.program_id(1)

    @pl.when(i == 0)
    def _():
        for scr, s0_ref in zip((sf, sb), s0_refs):
            scr[...] = jnp.zeros(scr.shape, F32)
            if s0_ref is not None:
                for h in range(HG_HEADS):
                    g, rv, ck = _head_block(h)
                    scr[g, rv, ck] = s0_ref[0, 0, h].T

    worst_f = _hg_direction(True, qf_ref, ff_ref, vf_ref, lbf_ref, l1f_ref, sf, of_ref, *scr_f)
    worst_b = _hg_direction(False, qb_ref, fb_ref, vb_ref, lbb_ref, l1b_ref, sb, ob_ref, *scr_b)

    @pl.when(functools.reduce(jnp.maximum, worst_f + worst_b) > HG_SAFE_DECAY)
    def _():
        _hg_pairwise(True, worst_f, of_ref, *scr_f)
        _hg_pairwise(False, worst_b, ob_ref, *scr_b)

    @pl.when(i == pl.num_programs(1) - 1)
    def _():
        for scr, out in ((sf, sf_out), (sb, sb_out)):
            for h in range(HG_HEADS):
                g, rv, ck = _head_block(h)
                out[0, h] = scr[g, rv, ck].T


def _hgrn(z, n_batch, lb_f, lb_b, s0f=None, s0b=None, layer=0):
    m = z.shape[0]
    t = ROW_BLK
    nblk = m // n_batch // t
    has_state = s0f is not None
    loglb_f, l1m_f = _lb_logs(lb_f)
    loglb_b, l1m_b = _lb_logs(lb_b)
    row = lambda a: a.reshape(1, HG_KW)

    def fmap(col):
        return lambda b, i: (b * nblk + i, col // HG_KW)

    def bmap(col):
        return lambda b, i: (b * nblk + nblk - 1 - i, col // HG_KW)

    blk = lambda imap: pl.BlockSpec((t, HG_KW), imap)
    st_out = pl.BlockSpec((1, HG_HEADS, HG_DK, HG_DV), lambda b, i: (b, 0, 0, 0))
    st_shape = jax.ShapeDtypeStruct((n_batch, HG_HEADS, HG_DK, HG_DV), F32)
    st_in = pl.BlockSpec((1, 1, HG_HEADS, HG_DK, HG_DV), lambda b, i: (b, layer, 0, 0, 0))
    bd_scratch = pltpu.VMEM((2, 4 * HG_DV, 4 * HG_DK), F32)
    return pl.pallas_call(
        functools.partial(_hgrn_kernel, has_state=has_state),
        out_shape=(jax.ShapeDtypeStruct((m, HG_W), F32), jax.ShapeDtypeStruct((m, HG_W), F32), st_shape, st_shape),
        grid=(n_batch, nblk),
        in_specs=[blk(fmap(M_HG_Q)), blk(fmap(M_HG_FF)), blk(fmap(M_HG_I)),
                  blk(bmap(M_HG_Q)), blk(bmap(M_HG_FB)), blk(bmap(M_HG_I)),
                  _const_spec((1, HG_KW)), _const_spec((1, HG_KW)), _const_spec((1, HG_KW)), _const_spec((1, HG_KW))]
                 + ([st_in, st_in] if has_state else []),
        out_specs=(pl.BlockSpec((t, HG_W), lambda b, i: (b * nblk + i, 0)),
                   pl.BlockSpec((t, HG_W), lambda b, i: (b * nblk + nblk - 1 - i, 0)),
                   st_out, st_out),
        scratch_shapes=[bd_scratch, bd_scratch] + [pltpu.VMEM((t, HG_KW), F32)] * 10,
        compiler_params=_params(("arbitrary", "arbitrary")),
        name="hgrn_scan",
    )(z, z, z, z, z, z, row(loglb_f), row(l1m_f), row(loglb_b), row(l1m_b), *((s0f, s0b) if has_state else ()))


def _out_kernel(*refs, has_next):
    (x_ref, mod_ref, gpost_ref, ghg_ref, ona_ref, omla_ref, of_ref, ob_ref,
     gna_ref, gmla_ref, ghgate_ref, mg_ref, wna_ref, wmla_ref, whg_ref, wout_ref) = refs[:16]
    y_ref = refs[-2] if has_next else refs[-1]
    o = of_ref[...] + ob_ref[...]
    hr = lax.broadcasted_iota(jnp.int32, (HG_W, HG_W), 0) // HG_DV
    hc = lax.broadcasted_iota(jnp.int32, (HG_W, HG_W), 1) // HG_DV
    head_mean = jnp.where(hr == hc, 1.0 / HG_DV, 0.0).astype(BF16)
    sq = o * o
    sq_hi = sq.astype(BF16)
    sq_lo = (sq - sq_hi.astype(F32)).astype(BF16)
    ms = _mm(sq_hi, head_mean) + _mm(sq_lo, head_mean)
    o_hg = o * lax.rsqrt(ms + NORM_EPS) * ghg_ref[...]

    def branch(o_b, gate_ref, w_ref):
        return _mm((o_b * _silu(gate_ref[...].astype(F32))).astype(BF16), w_ref[...])

    def merge_gate(i):
        return jax.nn.sigmoid(mg_ref[:, i * D_MODEL:(i + 1) * D_MODEL].astype(F32))

    merged = (merge_gate(0) * branch(ona_ref[...], gna_ref, wna_ref)
              + merge_gate(1) * branch(omla_ref[...], gmla_ref, wmla_ref)
              + merge_gate(2) * branch(o_hg, ghgate_ref, whg_ref))
    out = _mm(merged.astype(BF16), wout_ref[...])
    y = x_ref[...] + mod_ref[0, 2:3, :] * _rms(out, gpost_ref[...])
    y_ref[...] = y
    if has_next:
        modn_ref, gpren_ref, hn_ref = refs[16], refs[17], refs[-1]
        hn_ref[...] = _modulated_norm(y, gpren_ref[...], modn_ref).astype(BF16)


def _out_proj(x, z_head, z_mid, z_mg, mod3, g_post, g_hg, o_na, o_mla, o_f, o_b, w_na, w_mla, w_hg, w_out,
              rows_per_mod, mod_base, next_mod3=None, next_g_pre=None):
    m = x.shape[0]
    t = ROW_BLK
    tiles_per_mod = rows_per_mod // t
    has_next = next_mod3 is not None
    w512 = lambda: pl.BlockSpec((t, NA_W), lambda i: (i, 0))
    zcol = lambda col, w: pl.BlockSpec((t, w), lambda i: (i, col // w))
    mod_spec = pl.BlockSpec((1, 3, D_MODEL), lambda i: (mod_base + i // tiles_per_mod, 0, 0))
    row_spec = pl.BlockSpec((t, D_MODEL), lambda i: (i, 0))
    one = pl.Buffered(1)
    out = pl.pallas_call(
        functools.partial(_out_kernel, has_next=has_next),
        out_shape=(jax.ShapeDtypeStruct((m, D_MODEL), F32),)
                  + ((jax.ShapeDtypeStruct((m, D_MODEL), BF16),) if has_next else ()),
        grid=(m // t,),
        in_specs=[row_spec, mod_spec,
                  _const_spec((1, D_MODEL)), _const_spec((1, HG_W)),
                  w512(), w512(), w512(), w512(),
                  zcol(C_NA_G, NA_W), zcol(M_MLA_G, MLA_W), zcol(M_HG_G, HG_W),
                  zcol(0, MG_W),
                  pl.BlockSpec((NA_W, D_MODEL), lambda i: (0, 0), pipeline_mode=one),
                  pl.BlockSpec((MLA_W, D_MODEL), lambda i: (0, 0), pipeline_mode=one),
                  pl.BlockSpec((HG_W, D_MODEL), lambda i: (0, 0), pipeline_mode=one),
                  pl.BlockSpec((D_MODEL, D_MODEL), lambda i: (0, 0), pipeline_mode=one)]
                 + ([mod_spec, _const_spec((1, D_MODEL))] if has_next else []),
        out_specs=(row_spec,) + ((row_spec,) if has_next else ()),
        compiler_params=_params(("arbitrary",)),
        name="out_proj",
    )(x, mod3, g_post, g_hg, o_na, o_mla, o_f, o_b, z_head, z_mid, z_mid, z_mg, w_na, w_mla, w_hg, w_out,
      *((next_mod3, next_g_pre) if has_next else ()))
    return out if has_next else (out[0], None)


def kernel(x_prompt, x_sample, cache_na_k, cache_na_v, cache_mla_ckv, cache_mla_krope, state_hgrn_fwd, state_hgrn_bwd, c, c_ctx, w_ada, b_ada, g_pre, g_post, w_in, na_rpb, g_mla_q, w_mla_uq, g_mla_kv, w_mla_ukv, hg_lb_fwd, hg_lb_bwd, g_hg_out, w_br_na, w_br_mla, w_br_hg, w_out):
    bp, sp, _ = x_prompt.shape
    bs, ss, _ = x_sample.shape
    past = cache_na_k.shape[2]

    cvec = jnp.concatenate([c_ctx[None, :], c, jnp.zeros((8 - 1 - bs, D_MODEL), F32)], axis=0)
    mod = _modulation(cvec, w_ada, b_ada).reshape(DEPTH, 8, 3, D_MODEL)

    lb_f_all = _lower_bounds(hg_lb_fwd)
    lb_b_all = _lower_bounds(hg_lb_bwd)
    cos_t, sin_t = _rope_tables(ss)
    cache_k = cache_na_k.reshape(bs, DEPTH, past, NA_W)
    cache_v = cache_na_v.reshape(bs, DEPTH, past, NA_W)
    cache_kr_pad = _rope_lanes(cache_mla_krope)

    yp = x_prompt.reshape(bp * sp, D_MODEL)
    ys = x_sample.reshape(bs * ss, D_MODEL)
    new_k, new_v, new_ckv, new_kr, new_sf, new_sb = [], [], [], [], [], []
    w_head, w_tail = _pack_w_in(w_in)
    hp = _prenorm(yp, mod[0], g_pre[0][None, :], bp * sp, 0)
    hs = _prenorm(ys, mod[0], g_pre[0][None, :], ss, 1)
    for l in range(DEPTH):
        wq, wqp = _pack_w_uq(w_mla_uq[l])
        wuk, wuv = _pack_w_ukv(w_mla_ukv[l])
        w_na, w_mla, w_hg, w_o = (w.astype(BF16) for w in (w_br_na[l], w_br_mla[l], w_br_hg[l], w_out[l]))
        g_q, g_kv = g_mla_q[l][None, :], g_mla_kv[l][None, :]
        gpost, ghg = g_post[l][None, :], g_hg_out[l][None, :]
        mod3 = mod[l]
        nxt = (mod[l + 1], g_pre[l + 1][None, :]) if l + 1 < DEPTH else (None, None)

        zp_head, zp_mid, zp_mg = _in_proj(hp, w_head, w_tail, l)
        o_na, o_mla, ckvn, k_new, v_new, kr_new = _prompt_attn(zp_head, bp, g_q, g_kv, wq, wuk, wuv)
        o_f, o_b, sf, sb = _hgrn(zp_mid, bp, lb_f_all[l], lb_b_all[l])
        yp, hp = _out_proj(yp, zp_head, zp_mid, zp_mg, mod3, gpost, ghg, o_na, o_mla, o_f, o_b,
                           w_na, w_mla, w_hg, w_o, bp * sp, 0, *nxt)
        new_k.append(k_new.reshape(bp, sp, NA_HEADS, NA_HD))
        new_v.append(v_new.reshape(bp, sp, NA_HEADS, NA_HD))
        new_ckv.append(ckvn.reshape(bp, sp, MLA_KV_LORA))
        new_kr.append(kr_new.reshape(bp, sp, MLA_ROPE))
        new_sf.append(sf)
        new_sb.append(sb)

        zs_head, zs_mid, zs_mg = _in_proj(hs, w_head, w_tail, l)
        o_na = _sample_na(zs_head, cache_k, cache_v, _na_bias_table(na_rpb[l]), l, bs)
        kq_all, vp_all = _sample_mla_kv(zs_head, cache_mla_ckv, cache_kr_pad, cos_t, sin_t, g_kv, wuk, wuv, l, bs)
        o_mla = _sample_mla_attn(zs_head, cos_t, sin_t, g_q, wq, wqp, kq_all, vp_all, bs)
        o_f, o_b, _, _ = _hgrn(zs_mid, bs, lb_f_all[l], lb_b_all[l], state_hgrn_fwd, state_hgrn_bwd, l)
        ys, hs = _out_proj(ys, zs_head, zs_mid, zs_mg, mod3, gpost, ghg, o_na, o_mla, o_f, o_b,
                           w_na, w_mla, w_hg, w_o, ss, 1, *nxt)

    return (yp.reshape(bp, sp, D_MODEL), ys.reshape(bs, ss, D_MODEL),
            jnp.stack(new_k, axis=1), jnp.stack(new_v, axis=1), jnp.stack(new_ckv, axis=1),
            jnp.stack(new_kr, axis=1), jnp.stack(new_sf, axis=1), jnp.stack(new_sb, axis=1))
```

```python
import functools

import numpy as np
import jax
import jax.numpy as jnp
from jax import lax
from jax.experimental import pallas as pl
from jax.experimental.pallas import tpu as pltpu

D_MODEL = 2048
DEPTH = 2
GRID_W = 64
NORM_EPS = 1e-6
NEG_BIG = -1e30
NA_HEADS = 8
NA_HD = 64
NA_W = NA_HEADS * NA_HD
NA_KR = 8
NA_KW = 16
MLA_HEADS = 8
MLA_NOPE = 64
MLA_ROPE = 32
MLA_VD = 64
MLA_QK = MLA_NOPE + MLA_ROPE
MLA_W = MLA_HEADS * MLA_VD
MLA_Q_LORA = 512
MLA_KV_LORA = 256
ROPE_BASE = 10000.0
HG_HEADS = 8
HG_DK = 64
HG_DV = 64
HG_KW = HG_HEADS * HG_DK
HG_W = HG_HEADS * HG_DV
N_BRANCH = 3
PROJ_WIDTHS = (NA_W, NA_W, NA_W, NA_W, MLA_Q_LORA, MLA_KV_LORA, MLA_ROPE, MLA_W,
               HG_KW, HG_KW, HG_KW, HG_W, HG_W, N_BRANCH * D_MODEL)

F32 = jnp.float32
BF16 = jnp.bfloat16
HI = lax.Precision.HIGHEST

LANE = 128
ROW_BLK = 256
HG_CHUNK = 32
HG_SUB = 16
HG_SAFE_DECAY = 72.0
HG_EXP_CLAMP = 80.0
VMEM_LIMIT = 56 * 1024 * 1024

HEAD_W = 3072
MID_W = 3072
MG_W = N_BRANCH * D_MODEL
SPLIT_COL = 2848
C_NA_Q, C_NA_K, C_NA_V, C_NA_G = 0, 512, 1024, 1536
C_MLA_CQ, C_MLA_CKV, C_MLA_KR, C_MLA_KRP = 2048, 2560, 2816, 2944
M_MLA_G, M_HG_Q, M_HG_FF, M_HG_FB, M_HG_I, M_HG_G = 0, 512, 1024, 1536, 2048, 2560


def _nt(a, b, precision=None):
    return lax.dot_general(a, b, (((1,), (1,)), ((), ())), preferred_element_type=F32, precision=precision)


def _tn(a, b, precision=None):
    return lax.dot_general(a, b, (((0,), (0,)), ((), ())), preferred_element_type=F32, precision=precision)


def _mm(a, b, precision=None):
    return jnp.dot(a, b, preferred_element_type=F32, precision=precision)


def _rms(x, g):
    return x * lax.rsqrt(jnp.mean(x * x, axis=-1, keepdims=True) + NORM_EPS) * g


def _silu(x):
    return x * jax.nn.sigmoid(x)


def _params(sem):
    return pltpu.CompilerParams(dimension_semantics=sem, vmem_limit_bytes=VMEM_LIMIT)


def _const_spec(shape):
    nd = len(shape)
    return pl.BlockSpec(shape, lambda *_: (0,) * nd)


def _rot_half(a):
    parts = []
    for ax in range(2):
        lo_, hi_ = a[..., ax * 16:ax * 16 + 8], a[..., ax * 16 + 8:ax * 16 + 16]
        parts += [-hi_, lo_]
    return jnp.concatenate(parts, axis=-1)


def _pack_w_in(w):
    assert sum(PROJ_WIDTHS[:7]) == SPLIT_COL and sum(PROJ_WIDTHS[7:13]) == MID_W and PROJ_WIDTHS[13] == MG_W
    head = w[..., :SPLIT_COL].astype(BF16)
    kr = head[..., C_MLA_KR:SPLIT_COL]
    return jnp.concatenate([head[..., :C_MLA_KR], _rope_lanes(kr), _rope_lanes(_rot_half(kr))], axis=-1), \
        _cast_tail(w)


TAIL_STRIP = 256
TAIL_SHIFT = SPLIT_COL - C_MLA_KR
TAIL_TN = 1024
TAIL_NSTRIP = TAIL_TN // TAIL_STRIP + 1


def _cast_tail_kernel(*refs):
    out_ref = refs[-1]
    x = jnp.concatenate([r[0] for r in refs[:-1]], axis=-1)
    out_ref[0] = x[:, TAIL_SHIFT:TAIL_SHIFT + TAIL_TN].astype(BF16)


def _cast_tail(w):
    depth, d, n = w.shape
    rows = 512
    width = n - SPLIT_COL
    assert C_MLA_KR % TAIL_STRIP == 0 and width % TAIL_TN == 0
    strip0 = C_MLA_KR // TAIL_STRIP

    def strip(m):
        return pl.BlockSpec((1, rows, TAIL_STRIP), lambda l, i, j: (l, i, strip0 + j * (TAIL_TN // TAIL_STRIP) + m))

    return pl.pallas_call(
        _cast_tail_kernel,
        out_shape=jax.ShapeDtypeStruct((depth, d, width), BF16),
        grid=(depth, d // rows, width // TAIL_TN),
        in_specs=[strip(m) for m in range(TAIL_NSTRIP)],
        out_specs=pl.BlockSpec((1, rows, TAIL_TN), lambda l, i, j: (l, i, j)),
        compiler_params=_params(("arbitrary", "arbitrary", "arbitrary")),
        name="cast_tail",
    )(*([w] * TAIL_NSTRIP))


def _rope_lanes(a):
    z = lambda n: jnp.zeros(a.shape[:-1] + (n,), a.dtype)
    return jnp.concatenate([z(MLA_NOPE), a, z(LANE - MLA_QK)], axis=-1)


def _pack_w_uq(w):
    w3 = w.reshape(MLA_Q_LORA, MLA_HEADS, MLA_QK)
    nope, rope = w3[..., :MLA_NOPE], w3[..., MLA_NOPE:]
    pad = jnp.zeros(nope.shape[:-1] + (LANE - MLA_QK,), w.dtype)
    flat = lambda a: a.reshape(MLA_Q_LORA, MLA_HEADS * LANE).astype(BF16)
    return flat(jnp.concatenate([nope, rope, pad], axis=-1)), flat(_rope_lanes(_rot_half(rope)))


def _pack_w_ukv(w):
    w3 = w.reshape(MLA_KV_LORA, MLA_HEADS, MLA_NOPE + MLA_VD)
    pad = jnp.zeros((MLA_KV_LORA, MLA_HEADS, LANE - MLA_NOPE), w.dtype)
    flat = lambda a: a.reshape(MLA_KV_LORA, MLA_HEADS * LANE).astype(BF16)
    return flat(jnp.concatenate([w3[..., :MLA_NOPE], pad], -1)), flat(jnp.concatenate([w3[..., MLA_NOPE:], pad], -1))


def _rope_tables(n):
    t = np.arange(n)
    pos = np.stack([t // GRID_W, t % GRID_W]).astype(np.float32)
    axis_dim = MLA_ROPE // 2
    inv = ROPE_BASE ** (-jnp.arange(0, axis_dim, 2, dtype=F32) / axis_dim)
    ang = jnp.asarray(pos)[:, :, None] * inv
    ang = jnp.concatenate([ang, ang], axis=-1)
    ang = jnp.concatenate([ang[0], ang[1]], axis=-1)
    cos_t = jnp.concatenate([jnp.ones((n, MLA_NOPE), F32), jnp.cos(ang), jnp.zeros((n, LANE - MLA_QK), F32)], -1)
    return cos_t, _rope_lanes(jnp.sin(ang))


def _na_bias_table(rpb):
    rows = 32
    c = np.arange(GRID_W)
    win0 = np.clip(c - NA_KW // 2, 0, GRID_W - NA_KW)
    kc = np.arange(GRID_W)
    col_ok = (kc[None, :] >= win0[:, None]) & (kc[None, :] < win0[:, None] + NA_KW)
    rpb = rpb.astype(F32)
    edge = GRID_W - NA_KW
    ext = jnp.concatenate([jnp.broadcast_to(rpb[..., :1], rpb.shape[:-1] + (edge,)), rpb,
                           jnp.broadcast_to(rpb[..., -1:], rpb.shape[:-1] + (edge,))], axis=-1)
    toep = jnp.stack([ext[..., GRID_W - 1 - ci:2 * GRID_W - 1 - ci] for ci in range(GRID_W)], axis=2)
    toep = jnp.where(jnp.asarray(col_ok), toep, NEG_BIG)
    masked = jnp.full((NA_HEADS, GRID_W, GRID_W), NEG_BIG, F32)
    out = []
    for start, r0 in ((0, 0), (4, 8), (20, 28)):
        per_q = []
        for qr in range(4):
            r = r0 + qr
            kr0 = min(max(r - NA_KR // 2, 0), rows - NA_KR)
            tiles = []
            for j in range(12):
                kabs = start + j
                ok = kr0 <= kabs < kr0 + NA_KR
                tiles.append(toep[:, kabs - r + NA_KR - 1] if ok else masked)
            per_q.append(jnp.concatenate(tiles, axis=-1))
        out.append(jnp.concatenate(per_q, axis=1))
    return jnp.stack(out)


def _lower_bounds(p):
    s = jax.nn.softmax(p.astype(F32), axis=0)
    return jnp.cumsum(s, axis=0) - s[0]


def _lb_logs(lb):
    pos = lb > 0
    log_lb = jnp.where(pos, jnp.log(jnp.where(pos, lb, 1.0)), NEG_BIG)
    return log_lb, jnp.log1p(-lb)


def _mod_kernel(c_ref, w_ref, b_ref, o_ref):
    s = _silu(c_ref[...])
    o_ref[0] = _mm(s, w_ref[0], HI) + b_ref[0]


def _modulation(cvec, w_ada, b_ada):
    tn = 1024
    n3 = 3 * D_MODEL
    return pl.pallas_call(
        _mod_kernel,
        out_shape=jax.ShapeDtypeStruct((DEPTH, 8, n3), F32),
        grid=(DEPTH, n3 // tn),
        in_specs=[pl.BlockSpec((8, D_MODEL), lambda l, j: (0, 0)),
                  pl.BlockSpec((1, D_MODEL, tn), lambda l, j: (l, 0, j)),
                  pl.BlockSpec((1, 1, tn), lambda l, j: (l, 0, j))],
        out_specs=pl.BlockSpec((1, 8, tn), lambda l, j: (l, 0, j)),
        compiler_params=_params(("arbitrary", "arbitrary")),
        name="adaln_mod",
    )(cvec, w_ada, b_ada.reshape(DEPTH, 1, n3))


IN_TM, IN_TN = 2048, 1024


def _modulated_norm(x, g, mod_ref):
    return _rms(x, g) * (1.0 + mod_ref[0, 1:2, :]) + mod_ref[0, 0:1, :]


def _prenorm_kernel(x_ref, mod_ref, g_ref, h_ref):
    h_ref[...] = _modulated_norm(x_ref[...], g_ref[...], mod_ref).astype(BF16)


def _prenorm(x, mod3, g_pre, rows_per_mod, mod_base):
    m = x.shape[0]
    t = ROW_BLK
    tiles_per_mod = rows_per_mod // t
    return pl.pallas_call(
        _prenorm_kernel,
        out_shape=jax.ShapeDtypeStruct((m, D_MODEL), BF16),
        grid=(m // t,),
        in_specs=[pl.BlockSpec((t, D_MODEL), lambda i: (i, 0)),
                  pl.BlockSpec((1, 3, D_MODEL), lambda i: (mod_base + i // tiles_per_mod, 0, 0)),
                  _const_spec((1, D_MODEL))],
        out_specs=pl.BlockSpec((t, D_MODEL), lambda i: (i, 0)),
        compiler_params=_params(("arbitrary",)),
        name="prenorm",
    )(x, mod3, g_pre)


def _proj_kernel(h_ref, w_ref, z_ref):
    z_ref[...] = _mm(h_ref[...], w_ref[0]).astype(z_ref.dtype)


def _proj(h, w, layer, tile0, width, out_dtype):
    m = h.shape[0]
    tm, tn = IN_TM, IN_TN
    return pl.pallas_call(
        _proj_kernel,
        out_shape=jax.ShapeDtypeStruct((m, width), out_dtype),
        grid=(m // tm, width // tn),
        in_specs=[pl.BlockSpec((tm, D_MODEL), lambda i, j: (i, 0)),
                  pl.BlockSpec((1, D_MODEL, tn), lambda i, j: (layer, 0, tile0 + j))],
        out_specs=pl.BlockSpec((tm, tn), lambda i, j: (i, j)),
        compiler_params=_params(("arbitrary", "arbitrary")),
        name="in_proj",
    )(h, w)


def _in_proj(h, w_head, w_tail, layer):
    z_head = _proj(h, w_head, layer, 0, HEAD_W, F32)
    z_mid = _proj(h, w_tail, layer, 0, MID_W, BF16)
    z_mg = _proj(h, w_tail, layer, MID_W // IN_TN, MG_W, BF16)
    return z_head, z_mid, z_mg


def _softmax_pv(s_list, v_list):
    m = s_list[0].max(axis=-1, keepdims=True)
    for s in s_list[1:]:
        m = jnp.maximum(m, s.max(axis=-1, keepdims=True))
    acc, den = None, None
    for s, v in zip(s_list, v_list):
        e = jnp.exp(s - m)
        d = e.sum(axis=-1, keepdims=True)
        o = _mm(e.astype(BF16), v)
        acc = o if acc is None else acc + o
        den = d if den is None else den + d
    return acc, den


def _ones_lane():
    return (lax.broadcasted_iota(jnp.int32, (1, LANE), 1) == MLA_VD).astype(F32)


def _mla_heads(q_fn, kq, vp, o_ref):
    scale = MLA_QK ** -0.5
    outs = []
    for h in range(MLA_HEADS):
        sl = slice(h * LANE, (h + 1) * LANE)
        s = _nt(q_fn(sl), kq[:, sl]) * scale
        e = jnp.exp(s - s.max(axis=-1, keepdims=True))
        acc = _mm(e.astype(BF16), vp[:, sl])
        outs.append(acc[:, :MLA_VD] / acc[:, MLA_VD:MLA_VD + 1])
    o_ref[...] = jnp.concatenate(outs, axis=-1)


def _mla_keys_values(ckvn16, kr_tile, wuk_ref, wuv_ref):
    kq = _mm(ckvn16, wuk_ref[...]) + jnp.concatenate([kr_tile] * MLA_HEADS, axis=-1)
    vp = _mm(ckvn16, wuv_ref[...]) + jnp.concatenate([_ones_lane()] * MLA_HEADS, axis=-1)
    return kq.astype(BF16), vp.astype(BF16)


def _prompt_attn_kernel(*refs, n_prev):
    (qkv_ref, cq_ref, ckv_ref, kr_ref, gq_ref, gkv_ref, wq_ref, wuk_ref, wuv_ref) = refs[:9]
    prev_refs = refs[9:13] if n_prev else (None,) * 4
    ona_ref, omla_ref = refs[-6:-4]
    t = ROW_BLK

    def emit(out_ref, prev_ref, new):
        if prev_ref is not None:
            out_ref[0:n_prev * t, :] = prev_ref[...]
        out_ref[n_prev * t:(n_prev + 1) * t, :] = new

    qkv = qkv_ref[...]
    ckvn = _rms(ckv_ref[...], gkv_ref[...])
    for out_ref, prev_ref, new in zip(refs[-4:], prev_refs,
                                      (ckvn, qkv[:, NA_W:2 * NA_W], qkv[:, 2 * NA_W:3 * NA_W],
                                       kr_ref[:, MLA_NOPE:MLA_QK])):
        emit(out_ref, prev_ref, new)
    scale = NA_HD ** -0.5
    outs = []
    for h in range(NA_HEADS):
        qh = qkv[:, h * NA_HD:(h + 1) * NA_HD].astype(BF16)
        kh = qkv[:, NA_W + h * NA_HD:NA_W + (h + 1) * NA_HD].astype(BF16)
        vh = qkv[:, 2 * NA_W + h * NA_HD:2 * NA_W + (h + 1) * NA_HD].astype(BF16)
        acc, den = _softmax_pv([_nt(qh, kh) * scale], [vh])
        outs.append(acc / den)
    ona_ref[...] = jnp.concatenate(outs, axis=-1)

    cqn = _rms(cq_ref[...], gq_ref[...]).astype(BF16)
    q = _mm(cqn, wq_ref[...]).astype(BF16)
    kq, vp = _mla_keys_values(ckvn.astype(BF16), kr_ref[...], wuk_ref, wuv_ref)
    _mla_heads(lambda sl: q[:, sl], kq, vp, omla_ref)


CACHE_WIDTHS = (MLA_KV_LORA, NA_W, NA_W, MLA_ROPE)


def _prompt_attn(z, n_batch, g_q, g_kv, wq, wuk, wuv, prev=()):
    m = z.shape[0]
    t = ROW_BLK
    hw = MLA_HEADS * LANE
    n_prev = prev[0].shape[0] // m if prev else 0
    stack_rows = (n_prev + 1) * t
    return pl.pallas_call(
        functools.partial(_prompt_attn_kernel, n_prev=n_prev),
        out_shape=(jax.ShapeDtypeStruct((m, NA_W), F32), jax.ShapeDtypeStruct((m, MLA_W), F32))
                  + tuple(jax.ShapeDtypeStruct((n_batch * stack_rows, w), F32) for w in CACHE_WIDTHS),
        grid=(n_batch,),
        in_specs=[pl.BlockSpec((t, 3 * NA_W), lambda b: (b, C_NA_Q // (3 * NA_W))),
                  pl.BlockSpec((t, MLA_Q_LORA), lambda b: (b, C_MLA_CQ // MLA_Q_LORA)),
                  pl.BlockSpec((t, MLA_KV_LORA), lambda b: (b, C_MLA_CKV // MLA_KV_LORA)),
                  pl.BlockSpec((t, LANE), lambda b: (b, C_MLA_KR // LANE)),
                  _const_spec((1, MLA_Q_LORA)), _const_spec((1, MLA_KV_LORA)),
                  _const_spec((MLA_Q_LORA, hw)), _const_spec((MLA_KV_LORA, hw)), _const_spec((MLA_KV_LORA, hw))]
                 + [pl.BlockSpec((n_prev * t, w), lambda b: (b, 0)) for w in CACHE_WIDTHS[:4 if n_prev else 0]],
        out_specs=(pl.BlockSpec((t, NA_W), lambda b: (b, 0)), pl.BlockSpec((t, MLA_W), lambda b: (b, 0)))
                  + tuple(pl.BlockSpec((stack_rows, w), lambda b: (b, 0)) for w in CACHE_WIDTHS),
        compiler_params=_params(("arbitrary",)),
        name="prompt_attn",
    )(z, z, z, z, g_q, g_kv, wq, wuk, wuv, *prev)


def _sample_na_kernel(q_ref, k0_ref, k1_ref, k2_ref, v0_ref, v1_ref, v2_ref, kc_ref, vc_ref, bias_ref, o_ref):
    scale = NA_HD ** -0.5
    q = q_ref[...]
    kl = [r[...] for r in (k0_ref, k1_ref, k2_ref)]
    vl = [r[...] for r in (v0_ref, v1_ref, v2_ref)]
    kc, vc = kc_ref[0, 0], vc_ref[0, 0]
    outs = []
    for h in range(NA_HEADS):
        sl = slice(h * NA_HD, (h + 1) * NA_HD)
        qh = q[:, sl].astype(BF16)
        s_list = [_nt(qh, kl[i][:, sl].astype(BF16)) * scale + bias_ref[0, h, :, i * ROW_BLK:(i + 1) * ROW_BLK]
                  for i in range(3)]
        s_list.append(_nt(qh, kc[:, sl].astype(BF16)) * scale)
        v_list = [vl[i][:, sl].astype(BF16) for i in range(3)] + [vc[:, sl].astype(BF16)]
        acc, den = _softmax_pv(s_list, v_list)
        outs.append(acc / den)
    o_ref[...] = jnp.concatenate(outs, axis=-1)


def _sample_na(z, cache_k, cache_v, bias_tab, layer, n_batch):
    m = z.shape[0]
    t = ROW_BLK
    nblk = m // n_batch // t
    past = cache_k.shape[2]

    def kv_map(col, i):
        return lambda b, rb: (b * nblk + jnp.clip(rb - 1, 0, nblk - 3) + i, col // NA_W)

    def variant(b, rb):
        return (jnp.where(rb == 0, 0, jnp.where(rb == nblk - 1, 2, 1)), 0, 0, 0)

    cache_spec = pl.BlockSpec((1, 1, past, NA_W), lambda b, rb: (b, layer, 0, 0))
    return pl.pallas_call(
        _sample_na_kernel,
        out_shape=jax.ShapeDtypeStruct((m, NA_W), F32),
        grid=(n_batch, nblk),
        in_specs=[pl.BlockSpec((t, NA_W), lambda b, rb: (b * nblk + rb, C_NA_Q // NA_W))]
                 + [pl.BlockSpec((t, NA_W), kv_map(C_NA_K, i)) for i in range(3)]
                 + [pl.BlockSpec((t, NA_W), kv_map(C_NA_V, i)) for i in range(3)]
                 + [cache_spec, cache_spec,
                    pl.BlockSpec((1, NA_HEADS, t, 3 * t), variant)],
        out_specs=pl.BlockSpec((t, NA_W), lambda b, rb: (b * nblk + rb, 0)),
        compiler_params=_params(("arbitrary", "arbitrary")),
        name="sample_na",
    )(z, z, z, z, z, z, z, cache_k, cache_v, bias_tab)


def _sample_mla_kv_kernel(cckv_ref, ckr_ref, ckv_ref, kr_ref, krp_ref, cos_ref, sin_ref, gkv_ref, wuk_ref, wuv_ref,
                          kq_ref, vp_ref):
    j = pl.program_id(1)

    @pl.when(j == 0)
    def _():
        kq_ref[0], vp_ref[0] = _mla_keys_values(cckv_ref[0, 0].astype(BF16), ckr_ref[0, 0], wuk_ref, wuv_ref)

    @pl.when(j > 0)
    def _():
        ckvn = _rms(ckv_ref[...], gkv_ref[...])
        kr_roped = kr_ref[...] * cos_ref[...] + krp_ref[...] * sin_ref[...]
        kq_ref[0], vp_ref[0] = _mla_keys_values(ckvn.astype(BF16), kr_roped, wuk_ref, wuv_ref)


def _sample_mla_kv(z, cache_ckv, cache_kr_pad, cos_t, sin_t, g_kv, wuk, wuv, layer, n_batch):
    m = z.shape[0]
    n = m // n_batch
    past = cache_ckv.shape[2]
    t = past
    nb = n // t
    hw = MLA_HEADS * LANE

    def zrow(b, j):
        return b * nb + jnp.maximum(j - 1, 0)

    return pl.pallas_call(
        _sample_mla_kv_kernel,
        out_shape=(jax.ShapeDtypeStruct((n_batch, past + n, hw), BF16),
                   jax.ShapeDtypeStruct((n_batch, past + n, hw), BF16)),
        grid=(n_batch, nb + 1),
        in_specs=[pl.BlockSpec((1, 1, past, MLA_KV_LORA), lambda b, j: (b, layer, 0, 0)),
                  pl.BlockSpec((1, 1, past, LANE), lambda b, j: (b, layer, 0, 0)),
                  pl.BlockSpec((t, MLA_KV_LORA), lambda b, j: (zrow(b, j), C_MLA_CKV // MLA_KV_LORA)),
                  pl.BlockSpec((t, LANE), lambda b, j: (zrow(b, j), C_MLA_KR // LANE)),
                  pl.BlockSpec((t, LANE), lambda b, j: (zrow(b, j), C_MLA_KRP // LANE)),
                  pl.BlockSpec((t, LANE), lambda b, j: (jnp.maximum(j - 1, 0), 0)),
                  pl.BlockSpec((t, LANE), lambda b, j: (jnp.maximum(j - 1, 0), 0)),
                  _const_spec((1, MLA_KV_LORA)), _const_spec((MLA_KV_LORA, hw)), _const_spec((MLA_KV_LORA, hw))],
        out_specs=(pl.BlockSpec((1, t, hw), lambda b, j: (b, j, 0)),
                   pl.BlockSpec((1, t, hw), lambda b, j: (b, j, 0))),
        compiler_params=_params(("arbitrary", "arbitrary")),
        name="sample_mla_kv",
    )(cache_ckv, cache_kr_pad, z, z, z, cos_t, sin_t, g_kv, wuk, wuv)


def _sample_mla_attn_kernel(cq_ref, cos_ref, sin_ref, gq_ref, wq_ref, wqp_ref, kq_ref, vp_ref, o_ref):
    cqn = _rms(cq_ref[...], gq_ref[...]).astype(BF16)
    q = _mm(cqn, wq_ref[...])
    q_rot = _mm(cqn, wqp_ref[...])
    cos, sin = cos_ref[...], sin_ref[...]
    _mla_heads(lambda sl: (q[:, sl] * cos + q_rot[:, sl] * sin).astype(BF16), kq_ref[0], vp_ref[0], o_ref)


def _sample_mla_attn(z, cos_t, sin_t, g_q, wq, wqp, kq_all, vp_all, n_batch):
    m = z.shape[0]
    t = ROW_BLK
    nblk = m // n_batch // t
    nk = kq_all.shape[1]
    hw = MLA_HEADS * LANE
    return pl.pallas_call(
        _sample_mla_attn_kernel,
        out_shape=jax.ShapeDtypeStruct((m, MLA_W), F32),
        grid=(n_batch, nblk),
        in_specs=[pl.BlockSpec((t, MLA_Q_LORA), lambda b, i: (b * nblk + i, C_MLA_CQ // MLA_Q_LORA)),
                  pl.BlockSpec((t, LANE), lambda b, i: (i, 0)),
                  pl.BlockSpec((t, LANE), lambda b, i: (i, 0)),
                  _const_spec((1, MLA_Q_LORA)),
                  _const_spec((MLA_Q_LORA, hw)), _const_spec((MLA_Q_LORA, hw)),
                  pl.BlockSpec((1, nk, hw), lambda b, i: (b, 0, 0)),
                  pl.BlockSpec((1, nk, hw), lambda b, i: (b, 0, 0))],
        out_specs=pl.BlockSpec((t, MLA_W), lambda b, i: (b * nblk + i, 0)),
        compiler_params=_params(("arbitrary", "arbitrary")),
        name="sample_mla_attn",
    )(z, cos_t, sin_t, g_q, wq, wqp, kq_all, vp_all)


def _hg_direction(fwd, q_ref, zf_ref, v_ref, loglb_ref, l1m_ref, s_ref, o_ref, qs, ks, bs, vs, os):
    t = ROW_BLK
    nchunk = t // HG_CHUNK
    q = _silu(q_ref[...].astype(F32))
    zf = zf_ref[...].astype(F32)
    v = v_ref[...].astype(F32)
    soft = jnp.log1p(jnp.exp(-jnp.abs(zf)))
    a1 = jnp.broadcast_to(loglb_ref[...], zf.shape)
    a2 = l1m_ref[...] + (jnp.minimum(zf, 0.0) - soft)
    logf = jnp.maximum(a1, a2) + jnp.log1p(jnp.exp(-jnp.abs(a1 - a2)))
    k = jnp.exp(l1m_ref[...] - jnp.maximum(zf, 0.0) - soft)

    def chunk_masks(n):
        ri = lax.broadcasted_iota(jnp.int32, (n, n), 0)
        ci = lax.broadcasted_iota(jnp.int32, (n, n), 1)
        causal = (ri >= ci) if fwd else (ri <= ci)
        same_sub = (ri // HG_SUB) == (ci // HG_SUB)
        same = (ri // HG_CHUNK) == (ci // HG_CHUNK)
        return same_sub & causal, same & jnp.logical_not(same_sub) & causal, same_sub, same

    tri_sub, _, same_sub, same = chunk_masks(t)
    hi = logf.astype(BF16)
    lo = (logf - hi.astype(F32)).astype(BF16)

    def seg_sum(mask):
        m16 = jnp.where(mask, 1.0, 0.0).astype(BF16)
        return _mm(m16, hi) + _mm(m16, lo)

    b_sub = seg_sum(tri_sub)
    tot_sub = seg_sum(same_sub)
    tot = seg_sum(same)
    row = lax.broadcasted_iota(jnp.int32, (t, 1), 0) % HG_CHUNK
    later = (row >= HG_SUB) if fwd else (row < HG_SUB)
    b = b_sub + jnp.where(later, tot - tot_sub, 0.0)

    qt_sub = (q * jnp.exp(b_sub)).astype(BF16)
    kt_sub = (k * jnp.exp(jnp.minimum(-b_sub, HG_EXP_CLAMP))).astype(BF16)
    kh_sub = (k * jnp.exp(tot_sub - b_sub)).astype(BF16)
    qt16 = (q * jnp.exp(b)).astype(BF16)
    kh16 = (k * jnp.exp(tot - b)).astype(BF16)
    v16 = v.astype(BF16)

    hb = t // 2
    tri_hb, cross_hb, _, _ = chunk_masks(hb)
    outs = []
    for h in range(HG_HEADS):
        sl = slice(h * HG_DK, (h + 1) * HG_DK)
        parts = []
        for r0 in (0, hb):
            rs = slice(r0, r0 + hb)
            p = _nt(qt_sub[rs, sl], jnp.concatenate([kt_sub[rs, sl], kh_sub[rs, sl]], axis=0))
            a = jnp.where(tri_hb, p[:, :hb], 0.0) + jnp.where(cross_hb, p[:, hb:], 0.0)
            parts.append(_mm(a.astype(BF16), v16[rs, sl]))
        outs.append(jnp.concatenate(parts, axis=0))
    o_ref[...] = jnp.concatenate(outs, axis=-1)

    gw = 4 * HG_DK
    bd_r = lax.broadcasted_iota(jnp.int32, (gw, gw), 0) // HG_DV
    bd_c = lax.broadcasted_iota(jnp.int32, (gw, gw), 1) // HG_DK
    bd = bd_r == bd_c
    order = range(nchunk) if fwd else range(nchunk - 1, -1, -1)
    for g in range(2):
        ls = slice(g * gw, (g + 1) * gw)
        s = s_ref[g]
        for c in order:
            rs = slice(c * HG_CHUNK, (c + 1) * HG_CHUNK)
            inter = _nt(qt16[rs, ls], s.astype(BF16))
            o_ref[rs, ls] += inter
            os[rs, ls] = inter
            u = _tn(v16[rs, ls], kh16[rs, ls])
            s = s * jnp.exp(tot[c * HG_CHUNK:c * HG_CHUNK + 1, ls]) + jnp.where(bd, u, 0.0)
        s_ref[g] = s

    qs[...] = q
    ks[...] = k
    bs[...] = b
    vs[...] = v
    return [jnp.max(-jnp.minimum(tot_sub[c * HG_CHUNK:c * HG_CHUNK + 1, :],
                                 tot_sub[c * HG_CHUNK + HG_SUB:c * HG_CHUNK + HG_SUB + 1, :]))
            for c in range(nchunk)]


def _hg_pairwise(fwd, worst, o_ref, qs, ks, bs, vs, os):
    lane = lax.broadcasted_iota(jnp.int32, (HG_KW, LANE), 0) // HG_DK
    col = lax.broadcasted_iota(jnp.int32, (HG_KW, LANE), 1)
    head_sum = (lane == col).astype(F32)
    srow = lax.broadcasted_iota(jnp.int32, (HG_CHUNK, 1), 0)
    for c, worst_c in enumerate(worst):
        r0 = c * HG_CHUNK

        @pl.when(worst_c > HG_SAFE_DECAY)
        def _():
            kc = ks[r0:r0 + HG_CHUNK, :]
            bc = bs[r0:r0 + HG_CHUNK, :]
            vc = vs[r0:r0 + HG_CHUNK, :]

            def body(i, carry):
                qrow = qs[pl.ds(r0 + i, 1), :]
                brow = bs[pl.ds(r0 + i, 1), :]
                p = qrow * kc * jnp.exp(jnp.minimum(brow - bc, 0.0))
                keep = (srow <= i) if fwd else (srow >= i)
                p = jnp.where(keep, p, 0.0)
                a = _mm(p, head_sum, HI)
                a_full = _nt(a, head_sum, HI)
                o_ref[pl.ds(r0 + i, 1), :] = (jnp.sum(a_full * vc, axis=0, keepdims=True)
                                              + os[pl.ds(r0 + i, 1), :])
                return carry

            lax.fori_loop(0, HG_CHUNK, body, 0)


def _head_block(h):
    g, hh = divmod(h, 4)
    return g, slice(hh * HG_DV, (hh + 1) * HG_DV), slice(hh * HG_DK, (hh + 1) * HG_DK)


def _hgrn_kernel(*refs, has_state):
    (qf_ref, ff_ref, vf_ref, qb_ref, fb_ref, vb_ref, lbf_ref, l1f_ref, lbb_ref, l1b_ref) = refs[:10]
    s0_refs = refs[10:12] if has_state else (None, None)
    rest = refs[12:] if has_state else refs[10:]
    of_ref, ob_ref, sf_out, sb_out, sf, sb = rest[:6]
    scr_f, scr_b = rest[6:11], rest[11:16]
    i = pl.program_id(1)

    @pl.when(i == 0)
    def _():
        for scr, s0_ref in zip((sf, sb), s0_refs):
            scr[...] = jnp.zeros(scr.shape, F32)
            if s0_ref is not None:
                for h in range(HG_HEADS):
                    g, rv, ck = _head_block(h)
                    scr[g, rv, ck] = s0_ref[0, 0, h].T

    worst_f = _hg_direction(True, qf_ref, ff_ref, vf_ref, lbf_ref, l1f_ref, sf, of_ref, *scr_f)
    worst_b = _hg_direction(False, qb_ref, fb_ref, vb_ref, lbb_ref, l1b_ref, sb, ob_ref, *scr_b)

    @pl.when(functools.reduce(jnp.maximum, worst_f + worst_b) > HG_SAFE_DECAY)
    def _():
        _hg_pairwise(True, worst_f, of_ref, *scr_f)
        _hg_pairwise(False, worst_b, ob_ref, *scr_b)

    @pl.when(i == pl.num_programs(1) - 1)
    def _():
        for scr, out in ((sf, sf_out), (sb, sb_out)):
            for h in range(HG_HEADS):
                g, rv, ck = _head_block(h)
                out[0, h] = scr[g, rv, ck].T


def _hgrn(z, n_batch, lb_f, lb_b, s0f=None, s0b=None, layer=0):
    m = z.shape[0]
    t = ROW_BLK
    nblk = m // n_batch // t
    has_state = s0f is not None
    loglb_f, l1m_f = _lb_logs(lb_f)
    loglb_b, l1m_b = _lb_logs(lb_b)
    row = lambda a: a.reshape(1, HG_KW)

    def fmap(col):
        return lambda b, i: (b * nblk + i, col // HG_KW)

    def bmap(col):
        return lambda b, i: (b * nblk + nblk - 1 - i, col // HG_KW)

    blk = lambda imap: pl.BlockSpec((t, HG_KW), imap)
    st_out = pl.BlockSpec((1, HG_HEADS, HG_DK, HG_DV), lambda b, i: (b, 0, 0, 0))
    st_shape = jax.ShapeDtypeStruct((n_batch, HG_HEADS, HG_DK, HG_DV), F32)
    st_in = pl.BlockSpec((1, 1, HG_HEADS, HG_DK, HG_DV), lambda b, i: (b, layer, 0, 0, 0))
    bd_scratch = pltpu.VMEM((2, 4 * HG_DV, 4 * HG_DK), F32)
    return pl.pallas_call(
        functools.partial(_hgrn_kernel, has_state=has_state),
        out_shape=(jax.ShapeDtypeStruct((m, HG_W), F32), jax.ShapeDtypeStruct((m, HG_W), F32), st_shape, st_shape),
        grid=(n_batch, nblk),
        in_specs=[blk(fmap(M_HG_Q)), blk(fmap(M_HG_FF)), blk(fmap(M_HG_I)),
                  blk(bmap(M_HG_Q)), blk(bmap(M_HG_FB)), blk(bmap(M_HG_I)),
                  _const_spec((1, HG_KW)), _const_spec((1, HG_KW)), _const_spec((1, HG_KW)), _const_spec((1, HG_KW))]
                 + ([st_in, st_in] if has_state else []),
        out_specs=(pl.BlockSpec((t, HG_W), lambda b, i: (b * nblk + i, 0)),
                   pl.BlockSpec((t, HG_W), lambda b, i: (b * nblk + nblk - 1 - i, 0)),
                   st_out, st_out),
        scratch_shapes=[bd_scratch, bd_scratch] + [pltpu.VMEM((t, HG_KW), F32)] * 10,
        compiler_params=_params(("arbitrary", "arbitrary")),
        name="hgrn_scan",
    )(z, z, z, z, z, z, row(loglb_f), row(l1m_f), row(loglb_b), row(l1m_b), *((s0f, s0b) if has_state else ()))


def _out_kernel(*refs, has_next):
    (x_ref, mod_ref, gpost_ref, ghg_ref, ona_ref, omla_ref, of_ref, ob_ref,
     gna_ref, gmla_ref, ghgate_ref, mg_ref, wna_ref, wmla_ref, whg_ref, wout_ref) = refs[:16]
    y_ref = refs[-2] if has_next else refs[-1]
    o = of_ref[...] + ob_ref[...]
    hr = lax.broadcasted_iota(jnp.int32, (HG_W, HG_W), 0) // HG_DV
    hc = lax.broadcasted_iota(jnp.int32, (HG_W, HG_W), 1) // HG_DV
    head_mean = jnp.where(hr == hc, 1.0 / HG_DV, 0.0).astype(BF16)
    sq = o * o
    sq_hi = sq.astype(BF16)
    sq_lo = (sq - sq_hi.astype(F32)).astype(BF16)
    ms = _mm(sq_hi, head_mean) + _mm(sq_lo, head_mean)
    o_hg = o * lax.rsqrt(ms + NORM_EPS) * ghg_ref[...]

    def branch(o_b, gate_ref, w_ref):
        return _mm((o_b * _silu(gate_ref[...].astype(F32))).astype(BF16), w_ref[...])

    def merge_gate(i):
        return jax.nn.sigmoid(mg_ref[:, i * D_MODEL:(i + 1) * D_MODEL].astype(F32))

    merged = (merge_gate(0) * branch(ona_ref[...], gna_ref, wna_ref)
              + merge_gate(1) * branch(omla_ref[...], gmla_ref, wmla_ref)
              + merge_gate(2) * branch(o_hg, ghgate_ref, whg_ref))
    out = _mm(merged.astype(BF16), wout_ref[...])
    y = x_ref[...] + mod_ref[0, 2:3, :] * _rms(out, gpost_ref[...])
    y_ref[...] = y
    if has_next:
        modn_ref, gpren_ref, hn_ref = refs[16], refs[17], refs[-1]
        hn_ref[...] = _modulated_norm(y, gpren_ref[...], modn_ref).astype(BF16)


def _out_proj(x, z_head, z_mid, z_mg, mod3, g_post, g_hg, o_na, o_mla, o_f, o_b, w_na, w_mla, w_hg, w_out,
              rows_per_mod, mod_base, next_mod3=None, next_g_pre=None):
    m = x.shape[0]
    t = ROW_BLK
    tiles_per_mod = rows_per_mod // t
    has_next = next_mod3 is not None
    w512 = lambda: pl.BlockSpec((t, NA_W), lambda i: (i, 0))
    zcol = lambda col, w: pl.BlockSpec((t, w), lambda i: (i, col // w))
    mod_spec = pl.BlockSpec((1, 3, D_MODEL), lambda i: (mod_base + i // tiles_per_mod, 0, 0))
    row_spec = pl.BlockSpec((t, D_MODEL), lambda i: (i, 0))
    one = pl.Buffered(1)
    out = pl.pallas_call(
        functools.partial(_out_kernel, has_next=has_next),
        out_shape=(jax.ShapeDtypeStruct((m, D_MODEL), F32),)
                  + ((jax.ShapeDtypeStruct((m, D_MODEL), BF16),) if has_next else ()),
        grid=(m // t,),
        in_specs=[row_spec, mod_spec,
                  _const_spec((1, D_MODEL)), _const_spec((1, HG_W)),
                  w512(), w512(), w512(), w512(),
                  zcol(C_NA_G, NA_W), zcol(M_MLA_G, MLA_W), zcol(M_HG_G, HG_W),
                  zcol(0, MG_W),
                  pl.BlockSpec((NA_W, D_MODEL), lambda i: (0, 0), pipeline_mode=one),
                  pl.BlockSpec((MLA_W, D_MODEL), lambda i: (0, 0), pipeline_mode=one),
                  pl.BlockSpec((HG_W, D_MODEL), lambda i: (0, 0), pipeline_mode=one),
                  pl.BlockSpec((D_MODEL, D_MODEL), lambda i: (0, 0), pipeline_mode=one)]
                 + ([mod_spec, _const_spec((1, D_MODEL))] if has_next else []),
        out_specs=(row_spec,) + ((row_spec,) if has_next else ()),
        compiler_params=_params(("arbitrary",)),
        name="out_proj",
    )(x, mod3, g_post, g_hg, o_na, o_mla, o_f, o_b, z_head, z_mid, z_mid, z_mg, w_na, w_mla, w_hg, w_out,
      *((next_mod3, next_g_pre) if has_next else ()))
    return out if has_next else (out[0], None)


def kernel(x_prompt, x_sample, cache_na_k, cache_na_v, cache_mla_ckv, cache_mla_krope, state_hgrn_fwd, state_hgrn_bwd, c, c_ctx, w_ada, b_ada, g_pre, g_post, w_in, na_rpb, g_mla_q, w_mla_uq, g_mla_kv, w_mla_ukv, hg_lb_fwd, hg_lb_bwd, g_hg_out, w_br_na, w_br_mla, w_br_hg, w_out):
    bp, sp, _ = x_prompt.shape
    bs, ss, _ = x_sample.shape
    past = cache_na_k.shape[2]

    cvec = jnp.concatenate([c_ctx[None, :], c, jnp.zeros((8 - 1 - bs, D_MODEL), F32)], axis=0)
    mod = _modulation(cvec, w_ada, b_ada).reshape(DEPTH, 8, 3, D_MODEL)

    lb_f_all = _lower_bounds(hg_lb_fwd)
    lb_b_all = _lower_bounds(hg_lb_bwd)
    cos_t, sin_t = _rope_tables(ss)
    cache_k = cache_na_k.reshape(bs, DEPTH, past, NA_W)
    cache_v = cache_na_v.reshape(bs, DEPTH, past, NA_W)
    cache_kr_pad = _rope_lanes(cache_mla_krope)

    yp = x_prompt.reshape(bp * sp, D_MODEL)
    ys = x_sample.reshape(bs * ss, D_MODEL)
    caches, new_sf, new_sb = (), [], []
    w_head, w_tail = _pack_w_in(w_in)
    hp = _prenorm(yp, mod[0], g_pre[0][None, :], bp * sp, 0)
    hs = _prenorm(ys, mod[0], g_pre[0][None, :], ss, 1)
    for l in range(DEPTH):
        wq, wqp = _pack_w_uq(w_mla_uq[l])
        wuk, wuv = _pack_w_ukv(w_mla_ukv[l])
        w_na, w_mla, w_hg, w_o = (w.astype(BF16) for w in (w_br_na[l], w_br_mla[l], w_br_hg[l], w_out[l]))
        g_q, g_kv = g_mla_q[l][None, :], g_mla_kv[l][None, :]
        gpost, ghg = g_post[l][None, :], g_hg_out[l][None, :]
        mod3 = mod[l]
        nxt = (mod[l + 1], g_pre[l + 1][None, :]) if l + 1 < DEPTH else (None, None)

        zp_head, zp_mid, zp_mg = _in_proj(hp, w_head, w_tail, l)
        o_na, o_mla, *caches = _prompt_attn(zp_head, bp, g_q, g_kv, wq, wuk, wuv, tuple(caches))
        o_f, o_b, sf, sb = _hgrn(zp_mid, bp, lb_f_all[l], lb_b_all[l])
        yp, hp = _out_proj(yp, zp_head, zp_mid, zp_mg, mod3, gpost, ghg, o_na, o_mla, o_f, o_b,
                           w_na, w_mla, w_hg, w_o, bp * sp, 0, *nxt)
        new_sf.append(sf)
        new_sb.append(sb)

        zs_head, zs_mid, zs_mg = _in_proj(hs, w_head, w_tail, l)
        o_na = _sample_na(zs_head, cache_k, cache_v, _na_bias_table(na_rpb[l]), l, bs)
        kq_all, vp_all = _sample_mla_kv(zs_head, cache_mla_ckv, cache_kr_pad, cos_t, sin_t, g_kv, wuk, wuv, l, bs)
        o_mla = _sample_mla_attn(zs_head, cos_t, sin_t, g_q, wq, wqp, kq_all, vp_all, bs)
        o_f, o_b, _, _ = _hgrn(zs_mid, bs, lb_f_all[l], lb_b_all[l], state_hgrn_fwd, state_hgrn_bwd, l)
        ys, hs = _out_proj(ys, zs_head, zs_mid, zs_mg, mod3, gpost, ghg, o_na, o_mla, o_f, o_b,
                           w_na, w_mla, w_hg, w_o, ss, 1, *nxt)

    new_ckv, new_k, new_v, new_kr = caches
    return (yp.reshape(bp, sp, D_MODEL), ys.reshape(bs, ss, D_MODEL),
            new_k.reshape(bp, DEPTH, sp, NA_HEADS, NA_HD), new_v.reshape(bp, DEPTH, sp, NA_HEADS, NA_HD),
            new_ckv.reshape(bp, DEPTH, sp, MLA_KV_LORA), new_kr.reshape(bp, DEPTH, sp, MLA_ROPE),
            jnp.stack(new_sf, axis=1), jnp.stack(new_sb, axis=1))
```

```python
import functools

import numpy as np
import jax
import jax.numpy as jnp
from jax import lax
from jax.experimental import pallas as pl
from jax.experimental.pallas import tpu as pltpu

D_MODEL = 2048
DEPTH = 2
GRID_W = 64
NORM_EPS = 1e-6
NEG_BIG = -1e30
NA_HEADS = 8
NA_HD = 64
NA_W = NA_HEADS * NA_HD
NA_KR = 8
NA_KW = 16
MLA_HEADS = 8
MLA_NOPE = 64
MLA_ROPE = 32
MLA_VD = 64
MLA_QK = MLA_NOPE + MLA_ROPE
MLA_W = MLA_HEADS * MLA_VD
MLA_Q_LORA = 512
MLA_KV_LORA = 256
ROPE_BASE = 10000.0
HG_HEADS = 8
HG_DK = 64
HG_DV = 64
HG_KW = HG_HEADS * HG_DK
HG_W = HG_HEADS * HG_DV
N_BRANCH = 3
PROJ_WIDTHS = (NA_W, NA_W, NA_W, NA_W, MLA_Q_LORA, MLA_KV_LORA, MLA_ROPE, MLA_W,
               HG_KW, HG_KW, HG_KW, HG_W, HG_W, N_BRANCH * D_MODEL)

F32 = jnp.float32
BF16 = jnp.bfloat16
HI = lax.Precision.HIGHEST

LANE = 128
ROW_BLK = 256
HG_CHUNK = 32
HG_SUB = 16
HG_SAFE_DECAY = 72.0
HG_EXP_CLAMP = 80.0
VMEM_LIMIT = 56 * 1024 * 1024

HEAD_W = 3072
MID_W = 3072
MG_W = N_BRANCH * D_MODEL
SPLIT_COL = 2848
C_NA_Q, C_NA_K, C_NA_V, C_NA_G = 0, 512, 1024, 1536
C_MLA_CQ, C_MLA_CKV, C_MLA_KR, C_MLA_KRP = 2048, 2560, 2816, 2944
M_MLA_G, M_HG_Q, M_HG_FF, M_HG_FB, M_HG_I, M_HG_G = 0, 512, 1024, 1536, 2048, 2560


def _nt(a, b, precision=None):
    return lax.dot_general(a, b, (((1,), (1,)), ((), ())), preferred_element_type=F32, precision=precision)


def _tn(a, b, precision=None):
    return lax.dot_general(a, b, (((0,), (0,)), ((), ())), preferred_element_type=F32, precision=precision)


def _mm(a, b, precision=None):
    return jnp.dot(a, b, preferred_element_type=F32, precision=precision)


def _rms(x, g):
    return x * lax.rsqrt(jnp.mean(x * x, axis=-1, keepdims=True) + NORM_EPS) * g


def _silu(x):
    return x * jax.nn.sigmoid(x)


def _params(sem):
    return pltpu.CompilerParams(dimension_semantics=sem, vmem_limit_bytes=VMEM_LIMIT)


def _const_spec(shape):
    nd = len(shape)
    return pl.BlockSpec(shape, lambda *_: (0,) * nd)


def _rot_half(a):
    parts = []
    for ax in range(2):
        lo_, hi_ = a[..., ax * 16:ax * 16 + 8], a[..., ax * 16 + 8:ax * 16 + 16]
        parts += [-hi_, lo_]
    return jnp.concatenate(parts, axis=-1)


def _pack_w_in(w):
    assert sum(PROJ_WIDTHS[:7]) == SPLIT_COL and sum(PROJ_WIDTHS[7:13]) == MID_W and PROJ_WIDTHS[13] == MG_W
    head = w[..., :SPLIT_COL].astype(BF16)
    kr = head[..., C_MLA_KR:SPLIT_COL]
    return jnp.concatenate([head[..., :C_MLA_KR], _rope_lanes(kr), _rope_lanes(_rot_half(kr))], axis=-1), \
        w[..., SPLIT_COL:].astype(BF16)


def _rope_lanes(a):
    z = lambda n: jnp.zeros(a.shape[:-1] + (n,), a.dtype)
    return jnp.concatenate([z(MLA_NOPE), a, z(LANE - MLA_QK)], axis=-1)


def _pack_w_uq(w):
    w3 = w.reshape(MLA_Q_LORA, MLA_HEADS, MLA_QK)
    nope, rope = w3[..., :MLA_NOPE], w3[..., MLA_NOPE:]
    pad = jnp.zeros(nope.shape[:-1] + (LANE - MLA_QK,), w.dtype)
    flat = lambda a: a.reshape(MLA_Q_LORA, MLA_HEADS * LANE).astype(BF16)
    return flat(jnp.concatenate([nope, rope, pad], axis=-1)), flat(_rope_lanes(_rot_half(rope)))


def _pack_w_ukv(w):
    w3 = w.reshape(MLA_KV_LORA, MLA_HEADS, MLA_NOPE + MLA_VD)
    pad = jnp.zeros((MLA_KV_LORA, MLA_HEADS, LANE - MLA_NOPE), w.dtype)
    flat = lambda a: a.reshape(MLA_KV_LORA, MLA_HEADS * LANE).astype(BF16)
    return flat(jnp.concatenate([w3[..., :MLA_NOPE], pad], -1)), flat(jnp.concatenate([w3[..., MLA_NOPE:], pad], -1))


def _rope_tables(n):
    t = np.arange(n)
    pos = np.stack([t // GRID_W, t % GRID_W]).astype(np.float32)
    axis_dim = MLA_ROPE // 2
    inv = ROPE_BASE ** (-jnp.arange(0, axis_dim, 2, dtype=F32) / axis_dim)
    ang = jnp.asarray(pos)[:, :, None] * inv
    ang = jnp.concatenate([ang, ang], axis=-1)
    ang = jnp.concatenate([ang[0], ang[1]], axis=-1)
    cos_t = jnp.concatenate([jnp.ones((n, MLA_NOPE), F32), jnp.cos(ang), jnp.zeros((n, LANE - MLA_QK), F32)], -1)
    return cos_t, _rope_lanes(jnp.sin(ang))


def _na_bias_table(rpb):
    rows = 32
    c = np.arange(GRID_W)
    win0 = np.clip(c - NA_KW // 2, 0, GRID_W - NA_KW)
    kc = np.arange(GRID_W)
    col_ok = (kc[None, :] >= win0[:, None]) & (kc[None, :] < win0[:, None] + NA_KW)
    rpb = rpb.astype(F32)
    edge = GRID_W - NA_KW
    ext = jnp.concatenate([jnp.broadcast_to(rpb[..., :1], rpb.shape[:-1] + (edge,)), rpb,
                           jnp.broadcast_to(rpb[..., -1:], rpb.shape[:-1] + (edge,))], axis=-1)
    toep = jnp.stack([ext[..., GRID_W - 1 - ci:2 * GRID_W - 1 - ci] for ci in range(GRID_W)], axis=2)
    toep = jnp.where(jnp.asarray(col_ok), toep, NEG_BIG)
    masked = jnp.full((NA_HEADS, GRID_W, GRID_W), NEG_BIG, F32)
    out = []
    for start, r0 in ((0, 0), (4, 8), (20, 28)):
        per_q = []
        for qr in range(4):
            r = r0 + qr
            kr0 = min(max(r - NA_KR // 2, 0), rows - NA_KR)
            tiles = []
            for j in range(12):
                kabs = start + j
                ok = kr0 <= kabs < kr0 + NA_KR
                tiles.append(toep[:, kabs - r + NA_KR - 1] if ok else masked)
            per_q.append(jnp.concatenate(tiles, axis=-1))
        out.append(jnp.concatenate(per_q, axis=1))
    return jnp.stack(out)


def _lower_bounds(p):
    s = jax.nn.softmax(p.astype(F32), axis=0)
    return jnp.cumsum(s, axis=0) - s[0]


def _lb_logs(lb):
    pos = lb > 0
    log_lb = jnp.where(pos, jnp.log(jnp.where(pos, lb, 1.0)), NEG_BIG)
    return log_lb, jnp.log1p(-lb)


def _mod_kernel(c_ref, w_ref, b_ref, o_ref):
    s = _silu(c_ref[...])
    o_ref[0] = _mm(s, w_ref[0], HI) + b_ref[0]


def _modulation(cvec, w_ada, b_ada):
    tn = 1024
    n3 = 3 * D_MODEL
    return pl.pallas_call(
        _mod_kernel,
        out_shape=jax.ShapeDtypeStruct((DEPTH, 8, n3), F32),
        grid=(DEPTH, n3 // tn),
        in_specs=[pl.BlockSpec((8, D_MODEL), lambda l, j: (0, 0)),
                  pl.BlockSpec((1, D_MODEL, tn), lambda l, j: (l, 0, j)),
                  pl.BlockSpec((1, 1, tn), lambda l, j: (l, 0, j))],
        out_specs=pl.BlockSpec((1, 8, tn), lambda l, j: (l, 0, j)),
        compiler_params=_params(("arbitrary", "arbitrary")),
        name="adaln_mod",
    )(cvec, w_ada, b_ada.reshape(DEPTH, 1, n3))


IN_TM, IN_TN = 2048, 1024


def _modulated_norm(x, g, mod_ref):
    return _rms(x, g) * (1.0 + mod_ref[0, 1:2, :]) + mod_ref[0, 0:1, :]


def _prenorm_kernel(x_ref, mod_ref, g_ref, h_ref):
    h_ref[...] = _modulated_norm(x_ref[...], g_ref[...], mod_ref).astype(BF16)


def _prenorm(x, mod3, g_pre, rows_per_mod, mod_base):
    m = x.shape[0]
    t = ROW_BLK
    tiles_per_mod = rows_per_mod // t
    return pl.pallas_call(
        _prenorm_kernel,
        out_shape=jax.ShapeDtypeStruct((m, D_MODEL), BF16),
        grid=(m // t,),
        in_specs=[pl.BlockSpec((t, D_MODEL), lambda i: (i, 0)),
                  pl.BlockSpec((1, 3, D_MODEL), lambda i: (mod_base + i // tiles_per_mod, 0, 0)),
                  _const_spec((1, D_MODEL))],
        out_specs=pl.BlockSpec((t, D_MODEL), lambda i: (i, 0)),
        compiler_params=_params(("arbitrary",)),
        name="prenorm",
    )(x, mod3, g_pre)


def _proj_kernel(h_ref, w_ref, z_ref):
    z_ref[...] = _mm(h_ref[...], w_ref[0]).astype(z_ref.dtype)


def _proj(h, w, layer, tile0, width, out_dtype):
    m = h.shape[0]
    tm, tn = IN_TM, IN_TN
    return pl.pallas_call(
        _proj_kernel,
        out_shape=jax.ShapeDtypeStruct((m, width), out_dtype),
        grid=(m // tm, width // tn),
        in_specs=[pl.BlockSpec((tm, D_MODEL), lambda i, j: (i, 0)),
                  pl.BlockSpec((1, D_MODEL, tn), lambda i, j: (layer, 0, tile0 + j))],
        out_specs=pl.BlockSpec((tm, tn), lambda i, j: (i, j)),
        compiler_params=_params(("arbitrary", "arbitrary")),
        name="in_proj",
    )(h, w)


def _in_proj(h, w_head, w_tail, layer):
    z_head = _proj(h, w_head, layer, 0, HEAD_W, F32)
    z_mid = _proj(h, w_tail, layer, 0, MID_W, BF16)
    z_mg = _proj(h, w_tail, layer, MID_W // IN_TN, MG_W, BF16)
    return z_head, z_mid, z_mg


def _softmax_pv(s_list, v_list):
    m = s_list[0].max(axis=-1, keepdims=True)
    for s in s_list[1:]:
        m = jnp.maximum(m, s.max(axis=-1, keepdims=True))
    acc, den = None, None
    for s, v in zip(s_list, v_list):
        e = jnp.exp(s - m)
        d = e.sum(axis=-1, keepdims=True)
        o = _mm(e.astype(BF16), v)
        acc = o if acc is None else acc + o
        den = d if den is None else den + d
    return acc, den


def _na_head_pairs(q, k_fn, v_fn, scores_fn):
    low = lax.broadcasted_iota(jnp.int32, (1, LANE), 1) < NA_HD
    outs = []
    for p in range(NA_HEADS // 2):
        sl = slice(p * LANE, (p + 1) * LANE)
        q2, k_pair, v_pair = q[:, sl], k_fn(sl), v_fn(sl)
        halves = []
        for half in range(2):
            qh = jnp.where(low if half == 0 else jnp.logical_not(low), q2, 0.0).astype(BF16)
            acc, den = _softmax_pv(scores_fn(qh, k_pair, 2 * p + half), v_pair)
            halves.append(acc / den)
        outs.append(jnp.where(low, halves[0], halves[1]))
    return jnp.concatenate(outs, axis=-1)


def _ones_lane():
    return (lax.broadcasted_iota(jnp.int32, (1, LANE), 1) == MLA_VD).astype(F32)


def _mla_heads(q_fn, kq, vp, o_ref):
    outs = []
    for h in range(MLA_HEADS):
        sl = slice(h * LANE, (h + 1) * LANE)
        s = _nt(q_fn(sl), kq[:, sl])
        e = jnp.exp(s - s.max(axis=-1, keepdims=True))
        acc = _mm(e.astype(BF16), vp[:, sl])
        outs.append(acc[:, :MLA_VD] / acc[:, MLA_VD:MLA_VD + 1])
    o_ref[...] = jnp.concatenate(outs, axis=-1)


def _mla_keys_values(ckvn16, kr_tile, wuk_ref, wuv_ref):
    kq = _mm(ckvn16, wuk_ref[...]) + jnp.concatenate([kr_tile] * MLA_HEADS, axis=-1)
    vp = _mm(ckvn16, wuv_ref[...]) + jnp.concatenate([_ones_lane()] * MLA_HEADS, axis=-1)
    return kq.astype(BF16), vp.astype(BF16)


def _prompt_attn_kernel(qkv_ref, cq_ref, ckv_ref, kr_ref, gq_ref, gkv_ref, wq_ref, wuk_ref, wuv_ref,
                        ona_ref, omla_ref, ckvn_ref, k_ref, v_ref, kro_ref):
    qkv = qkv_ref[...]
    k_ref[...] = qkv[:, NA_W:2 * NA_W]
    v_ref[...] = qkv[:, 2 * NA_W:3 * NA_W]
    kro_ref[...] = kr_ref[:, MLA_NOPE:MLA_QK]
    ckvn = _rms(ckv_ref[...], gkv_ref[...])
    ckvn_ref[...] = ckvn
    scale = NA_HD ** -0.5
    outs = []
    for h in range(NA_HEADS):
        qh = (qkv[:, h * NA_HD:(h + 1) * NA_HD] * scale).astype(BF16)
        kh = qkv[:, NA_W + h * NA_HD:NA_W + (h + 1) * NA_HD].astype(BF16)
        vh = qkv[:, 2 * NA_W + h * NA_HD:2 * NA_W + (h + 1) * NA_HD].astype(BF16)
        acc, den = _softmax_pv([_nt(qh, kh)], [vh])
        outs.append(acc / den)
    ona_ref[...] = jnp.concatenate(outs, axis=-1)

    cqn = _rms(cq_ref[...], gq_ref[...]).astype(BF16)
    q = _mm(cqn, wq_ref[...]).astype(BF16)
    kq, vp = _mla_keys_values(ckvn.astype(BF16), kr_ref[...], wuk_ref, wuv_ref)
    _mla_heads(lambda sl: q[:, sl], kq, vp, omla_ref)


def _prompt_attn(z, n_batch, g_q, g_kv, wq, wuk, wuv):
    m = z.shape[0]
    t = ROW_BLK
    hw = MLA_HEADS * LANE
    return pl.pallas_call(
        _prompt_attn_kernel,
        out_shape=(jax.ShapeDtypeStruct((m, NA_W), F32), jax.ShapeDtypeStruct((m, MLA_W), F32),
                   jax.ShapeDtypeStruct((m, MLA_KV_LORA), F32), jax.ShapeDtypeStruct((m, NA_W), F32),
                   jax.ShapeDtypeStruct((m, NA_W), F32), jax.ShapeDtypeStruct((m, MLA_ROPE), F32)),
        grid=(n_batch,),
        in_specs=[pl.BlockSpec((t, 3 * NA_W), lambda b: (b, C_NA_Q // (3 * NA_W))),
                  pl.BlockSpec((t, MLA_Q_LORA), lambda b: (b, C_MLA_CQ // MLA_Q_LORA)),
                  pl.BlockSpec((t, MLA_KV_LORA), lambda b: (b, C_MLA_CKV // MLA_KV_LORA)),
                  pl.BlockSpec((t, LANE), lambda b: (b, C_MLA_KR // LANE)),
                  _const_spec((1, MLA_Q_LORA)), _const_spec((1, MLA_KV_LORA)),
                  _const_spec((MLA_Q_LORA, hw)), _const_spec((MLA_KV_LORA, hw)), _const_spec((MLA_KV_LORA, hw))],
        out_specs=(pl.BlockSpec((t, NA_W), lambda b: (b, 0)), pl.BlockSpec((t, MLA_W), lambda b: (b, 0)),
                   pl.BlockSpec((t, MLA_KV_LORA), lambda b: (b, 0)), pl.BlockSpec((t, NA_W), lambda b: (b, 0)),
                   pl.BlockSpec((t, NA_W), lambda b: (b, 0)), pl.BlockSpec((t, MLA_ROPE), lambda b: (b, 0))),
        compiler_params=_params(("arbitrary",)),
        name="prompt_attn",
    )(z, z, z, z, g_q, g_kv, wq, wuk, wuv)


def _sample_na_kernel(q_ref, k0_ref, k1_ref, k2_ref, v0_ref, v1_ref, v2_ref, kc_ref, vc_ref, bias_ref, o_ref):
    q = q_ref[...] * NA_HD ** -0.5
    kl = [r[...] for r in (k0_ref, k1_ref, k2_ref)]
    vl = [r[...] for r in (v0_ref, v1_ref, v2_ref)]
    kc, vc = kc_ref[0, 0], vc_ref[0, 0]

    def scores(qh, k_pair, h):
        s_list = [_nt(qh, k_pair[i]) + bias_ref[0, h, :, i * ROW_BLK:(i + 1) * ROW_BLK] for i in range(3)]
        return s_list + [_nt(qh, k_pair[3])]

    o_ref[...] = _na_head_pairs(q, lambda sl: [a[:, sl].astype(BF16) for a in kl + [kc]],
                                lambda sl: [a[:, sl].astype(BF16) for a in vl + [vc]], scores)


def _sample_na(z, cache_k, cache_v, bias_tab, layer, n_batch):
    m = z.shape[0]
    t = ROW_BLK
    nblk = m // n_batch // t
    past = cache_k.shape[2]

    def kv_map(col, i):
        return lambda b, rb: (b * nblk + jnp.clip(rb - 1, 0, nblk - 3) + i, col // NA_W)

    def variant(b, rb):
        return (jnp.where(rb == 0, 0, jnp.where(rb == nblk - 1, 2, 1)), 0, 0, 0)

    cache_spec = pl.BlockSpec((1, 1, past, NA_W), lambda b, rb: (b, layer, 0, 0))
    return pl.pallas_call(
        _sample_na_kernel,
        out_shape=jax.ShapeDtypeStruct((m, NA_W), F32),
        grid=(n_batch, nblk),
        in_specs=[pl.BlockSpec((t, NA_W), lambda b, rb: (b * nblk + rb, C_NA_Q // NA_W))]
                 + [pl.BlockSpec((t, NA_W), kv_map(C_NA_K, i)) for i in range(3)]
                 + [pl.BlockSpec((t, NA_W), kv_map(C_NA_V, i)) for i in range(3)]
                 + [cache_spec, cache_spec,
                    pl.BlockSpec((1, NA_HEADS, t, 3 * t), variant)],
        out_specs=pl.BlockSpec((t, NA_W), lambda b, rb: (b * nblk + rb, 0)),
        compiler_params=_params(("arbitrary", "arbitrary")),
        name="sample_na",
    )(z, z, z, z, z, z, z, cache_k, cache_v, bias_tab)


def _sample_mla_kv_kernel(cckv_ref, ckr_ref, ckv_ref, kr_ref, krp_ref, cos_ref, sin_ref, gkv_ref, wuk_ref, wuv_ref,
                          kq_ref, vp_ref):
    j = pl.program_id(1)

    @pl.when(j == 0)
    def _():
        kq_ref[0], vp_ref[0] = _mla_keys_values(cckv_ref[0, 0].astype(BF16), ckr_ref[0, 0], wuk_ref, wuv_ref)

    @pl.when(j > 0)
    def _():
        ckvn = _rms(ckv_ref[...], gkv_ref[...])
        kr_roped = kr_ref[...] * cos_ref[...] + krp_ref[...] * sin_ref[...]
        kq_ref[0], vp_ref[0] = _mla_keys_values(ckvn.astype(BF16), kr_roped, wuk_ref, wuv_ref)


def _sample_mla_kv(z, cache_ckv, cache_kr_pad, cos_t, sin_t, g_kv, wuk, wuv, layer, n_batch):
    m = z.shape[0]
    n = m // n_batch
    past = cache_ckv.shape[2]
    t = past
    nb = n // t
    hw = MLA_HEADS * LANE

    def zrow(b, j):
        return b * nb + jnp.maximum(j - 1, 0)

    return pl.pallas_call(
        _sample_mla_kv_kernel,
        out_shape=(jax.ShapeDtypeStruct((n_batch, past + n, hw), BF16),
                   jax.ShapeDtypeStruct((n_batch, past + n, hw), BF16)),
        grid=(n_batch, nb + 1),
        in_specs=[pl.BlockSpec((1, 1, past, MLA_KV_LORA), lambda b, j: (b, layer, 0, 0)),
                  pl.BlockSpec((1, 1, past, LANE), lambda b, j: (b, layer, 0, 0)),
                  pl.BlockSpec((t, MLA_KV_LORA), lambda b, j: (zrow(b, j), C_MLA_CKV // MLA_KV_LORA)),
                  pl.BlockSpec((t, LANE), lambda b, j: (zrow(b, j), C_MLA_KR // LANE)),
                  pl.BlockSpec((t, LANE), lambda b, j: (zrow(b, j), C_MLA_KRP // LANE)),
                  pl.BlockSpec((t, LANE), lambda b, j: (jnp.maximum(j - 1, 0), 0)),
                  pl.BlockSpec((t, LANE), lambda b, j: (jnp.maximum(j - 1, 0), 0)),
                  _const_spec((1, MLA_KV_LORA)), _const_spec((MLA_KV_LORA, hw)), _const_spec((MLA_KV_LORA, hw))],
        out_specs=(pl.BlockSpec((1, t, hw), lambda b, j: (b, j, 0)),
                   pl.BlockSpec((1, t, hw), lambda b, j: (b, j, 0))),
        compiler_params=_params(("arbitrary", "arbitrary")),
        name="sample_mla_kv",
    )(cache_ckv, cache_kr_pad, z, z, z, cos_t, sin_t, g_kv, wuk, wuv)


def _sample_mla_attn_kernel(cq_ref, cos_ref, sin_ref, gq_ref, wq_ref, wqp_ref, kq_ref, vp_ref, o_ref):
    cqn = _rms(cq_ref[...], gq_ref[...]).astype(BF16)
    q = _mm(cqn, wq_ref[...])
    q_rot = _mm(cqn, wqp_ref[...])
    cos, sin = cos_ref[...], sin_ref[...]
    _mla_heads(lambda sl: (q[:, sl] * cos + q_rot[:, sl] * sin).astype(BF16), kq_ref[0], vp_ref[0], o_ref)


def _sample_mla_attn(z, cos_t, sin_t, g_q, wq, wqp, kq_all, vp_all, n_batch):
    m = z.shape[0]
    t = ROW_BLK
    nblk = m // n_batch // t
    nk = kq_all.shape[1]
    hw = MLA_HEADS * LANE
    return pl.pallas_call(
        _sample_mla_attn_kernel,
        out_shape=jax.ShapeDtypeStruct((m, MLA_W), F32),
        grid=(n_batch, nblk),
        in_specs=[pl.BlockSpec((t, MLA_Q_LORA), lambda b, i: (b * nblk + i, C_MLA_CQ // MLA_Q_LORA)),
                  pl.BlockSpec((t, LANE), lambda b, i: (i, 0)),
                  pl.BlockSpec((t, LANE), lambda b, i: (i, 0)),
                  _const_spec((1, MLA_Q_LORA)),
                  _const_spec((MLA_Q_LORA, hw)), _const_spec((MLA_Q_LORA, hw)),
                  pl.BlockSpec((1, nk, hw), lambda b, i: (b, 0, 0)),
                  pl.BlockSpec((1, nk, hw), lambda b, i: (b, 0, 0))],
        out_specs=pl.BlockSpec((t, MLA_W), lambda b, i: (b * nblk + i, 0)),
        compiler_params=_params(("arbitrary", "arbitrary")),
        name="sample_mla_attn",
    )(z, cos_t, sin_t, g_q, wq, wqp, kq_all, vp_all)


def _hg_direction(fwd, q_ref, zf_ref, v_ref, loglb_ref, l1m_ref, s_ref, o_ref, qs, ks, bs, vs, os):
    t = ROW_BLK
    nchunk = t // HG_CHUNK
    q = _silu(q_ref[...].astype(F32))
    zf = zf_ref[...].astype(F32)
    v = v_ref[...].astype(F32)
    soft = jnp.log1p(jnp.exp(-jnp.abs(zf)))
    a1 = jnp.broadcast_to(loglb_ref[...], zf.shape)
    a2 = l1m_ref[...] + (jnp.minimum(zf, 0.0) - soft)
    logf = jnp.maximum(a1, a2) + jnp.log1p(jnp.exp(-jnp.abs(a1 - a2)))
    k = jnp.exp(l1m_ref[...] - jnp.maximum(zf, 0.0) - soft)

    def chunk_masks(n):
        ri = lax.broadcasted_iota(jnp.int32, (n, n), 0)
        ci = lax.broadcasted_iota(jnp.int32, (n, n), 1)
        causal = (ri >= ci) if fwd else (ri <= ci)
        same_sub = (ri // HG_SUB) == (ci // HG_SUB)
        same = (ri // HG_CHUNK) == (ci // HG_CHUNK)
        return same_sub & causal, same & jnp.logical_not(same_sub) & causal, same_sub, same

    tri_sub, _, same_sub, same = chunk_masks(t)
    hi = logf.astype(BF16)
    lo = (logf - hi.astype(F32)).astype(BF16)

    def seg_sum(mask):
        m16 = jnp.where(mask, 1.0, 0.0).astype(BF16)
        return _mm(m16, hi) + _mm(m16, lo)

    b_sub = seg_sum(tri_sub)
    nsub = t // HG_SUB
    b3 = b_sub.reshape(nsub, HG_SUB, HG_KW)
    edge = b3[:, HG_SUB - 1:HG_SUB, :] if fwd else b3[:, 0:1, :]
    tot_sub = jnp.broadcast_to(edge, (nsub, HG_SUB, HG_KW)).reshape(t, HG_KW)
    pair = edge.reshape(nchunk, 2, 1, HG_KW)
    tot = jnp.broadcast_to(pair[:, 0:1] + pair[:, 1:2], (nchunk, 2, HG_SUB, HG_KW)).reshape(t, HG_KW)
    row = lax.broadcasted_iota(jnp.int32, (t, 1), 0) % HG_CHUNK
    later = (row >= HG_SUB) if fwd else (row < HG_SUB)
    b = b_sub + jnp.where(later, tot - tot_sub, 0.0)

    qt_sub = (q * jnp.exp(b_sub)).astype(BF16)
    kt_sub = (k * jnp.exp(jnp.minimum(-b_sub, HG_EXP_CLAMP))).astype(BF16)
    kh_sub = (k * jnp.exp(tot_sub - b_sub)).astype(BF16)
    qt16 = (q * jnp.exp(b)).astype(BF16)
    kh16 = (k * jnp.exp(tot - b)).astype(BF16)
    v16 = v.astype(BF16)

    hb = t // 2
    tri_hb, cross_hb, _, _ = chunk_masks(hb)
    outs = []
    for h in range(HG_HEADS):
        sl = slice(h * HG_DK, (h + 1) * HG_DK)
        parts = []
        for r0 in (0, hb):
            rs = slice(r0, r0 + hb)
            p = _nt(qt_sub[rs, sl], jnp.concatenate([kt_sub[rs, sl], kh_sub[rs, sl]], axis=0))
            a = jnp.where(tri_hb, p[:, :hb], 0.0) + jnp.where(cross_hb, p[:, hb:], 0.0)
            parts.append(_mm(a.astype(BF16), v16[rs, sl]))
        outs.append(jnp.concatenate(parts, axis=0))
    o_ref[...] = jnp.concatenate(outs, axis=-1)

    gw = 4 * HG_DK
    bd_r = lax.broadcasted_iota(jnp.int32, (gw, gw), 0) // HG_DV
    bd_c = lax.broadcasted_iota(jnp.int32, (gw, gw), 1) // HG_DK
    bd = bd_r == bd_c
    order = range(nchunk) if fwd else range(nchunk - 1, -1, -1)
    for g in range(2):
        ls = slice(g * gw, (g + 1) * gw)
        s = s_ref[g]
        for c in order:
            rs = slice(c * HG_CHUNK, (c + 1) * HG_CHUNK)
            inter = _nt(qt16[rs, ls], s.astype(BF16))
            o_ref[rs, ls] += inter
            os[rs, ls] = inter
            u = _tn(v16[rs, ls], kh16[rs, ls])
            s = s * jnp.exp(tot[c * HG_CHUNK:c * HG_CHUNK + 1, ls]) + jnp.where(bd, u, 0.0)
        s_ref[g] = s

    qs[...] = q
    ks[...] = k
    bs[...] = b
    vs[...] = v
    return [jnp.max(-jnp.minimum(tot_sub[c * HG_CHUNK:c * HG_CHUNK + 1, :],
                                 tot_sub[c * HG_CHUNK + HG_SUB:c * HG_CHUNK + HG_SUB + 1, :]))
            for c in range(nchunk)]


def _hg_pairwise(fwd, worst, o_ref, qs, ks, bs, vs, os):
    lane = lax.broadcasted_iota(jnp.int32, (HG_KW, LANE), 0) // HG_DK
    col = lax.broadcasted_iota(jnp.int32, (HG_KW, LANE), 1)
    head_sum = (lane == col).astype(F32)
    srow = lax.broadcasted_iota(jnp.int32, (HG_CHUNK, 1), 0)
    for c, worst_c in enumerate(worst):
        r0 = c * HG_CHUNK

        @pl.when(worst_c > HG_SAFE_DECAY)
        def _():
            kc = ks[r0:r0 + HG_CHUNK, :]
            bc = bs[r0:r0 + HG_CHUNK, :]
            vc = vs[r0:r0 + HG_CHUNK, :]

            def body(i, carry):
                qrow = qs[pl.ds(r0 + i, 1), :]
                brow = bs[pl.ds(r0 + i, 1), :]
                p = qrow * kc * jnp.exp(jnp.minimum(brow - bc, 0.0))
                keep = (srow <= i) if fwd else (srow >= i)
                p = jnp.where(keep, p, 0.0)
                a = _mm(p, head_sum, HI)
                a_full = _nt(a, head_sum, HI)
                o_ref[pl.ds(r0 + i, 1), :] = (jnp.sum(a_full * vc, axis=0, keepdims=True)
                                              + os[pl.ds(r0 + i, 1), :])
                return carry

            lax.fori_loop(0, HG_CHUNK, body, 0)


def _head_block(h):
    g, hh = divmod(h, 4)
    return g, slice(hh * HG_DV, (hh + 1) * HG_DV), slice(hh * HG_DK, (hh + 1) * HG_DK)


def _hgrn_kernel(*refs, has_state):
    (qf_ref, ff_ref, vf_ref, qb_ref, fb_ref, vb_ref, lbf_ref, l1f_ref, lbb_ref, l1b_ref) = refs[:10]
    s0_refs = refs[10:12] if has_state else (None, None)
    rest = refs[12:] if has_state else refs[10:]
    of_ref, ob_ref, sf_out, sb_out, sf, sb = rest[:6]
    scr_f, scr_b = rest[6:11], rest[11:16]
    i = pl.program_id(1)

    @pl.when(i == 0)
    def _():
        for scr, s0_ref in zip((sf, sb), s0_refs):
            scr[...] = jnp.zeros(scr.shape, F32)
            if s0_ref is not None:
                for h in range(HG_HEADS):
                    g, rv, ck = _head_block(h)
                    scr[g, rv, ck] = s0_ref[0, 0, h].T

    worst_f = _hg_direction(True, qf_ref, ff_ref, vf_ref, lbf_ref, l1f_ref, sf, of_ref, *scr_f)
    worst_b = _hg_direction(False, qb_ref, fb_ref, vb_ref, lbb_ref, l1b_ref, sb, ob_ref, *scr_b)

    @pl.when(functools.reduce(jnp.maximum, worst_f + worst_b) > HG_SAFE_DECAY)
    def _():
        _hg_pairwise(True, worst_f, of_ref, *scr_f)
        _hg_pairwise(False, worst_b, ob_ref, *scr_b)

    @pl.when(i == pl.num_programs(1) - 1)
    def _():
        for scr, out in ((sf, sf_out), (sb, sb_out)):
            for h in range(HG_HEADS):
                g, rv, ck = _head_block(h)
                out[0, h] = scr[g, rv, ck].T


def _hgrn(z, n_batch, lb_f, lb_b, s0f=None, s0b=None, layer=0):
    m = z.shape[0]
    t = ROW_BLK
    nblk = m // n_batch // t
    has_state = s0f is not None
    loglb_f, l1m_f = _lb_logs(lb_f)
    loglb_b, l1m_b = _lb_logs(lb_b)
    row = lambda a: a.reshape(1, HG_KW)

    def fmap(col):
        return lambda b, i: (b * nblk + i, col // HG_KW)

    def bmap(col):
        return lambda b, i: (b * nblk + nblk - 1 - i, col // HG_KW)

    blk = lambda imap: pl.BlockSpec((t, HG_KW), imap)
    st_out = pl.BlockSpec((1, HG_HEADS, HG_DK, HG_DV), lambda b, i: (b, 0, 0, 0))
    st_shape = jax.ShapeDtypeStruct((n_batch, HG_HEADS, HG_DK, HG_DV), F32)
    st_in = pl.BlockSpec((1, 1, HG_HEADS, HG_DK, HG_DV), lambda b, i: (b, layer, 0, 0, 0))
    bd_scratch = pltpu.VMEM((2, 4 * HG_DV, 4 * HG_DK), F32)
    return pl.pallas_call(
        functools.partial(_hgrn_kernel, has_state=has_state),
        out_shape=(jax.ShapeDtypeStruct((m, HG_W), F32), jax.ShapeDtypeStruct((m, HG_W), F32), st_shape, st_shape),
        grid=(n_batch, nblk),
        in_specs=[blk(fmap(M_HG_Q)), blk(fmap(M_HG_FF)), blk(fmap(M_HG_I)),
                  blk(bmap(M_HG_Q)), blk(bmap(M_HG_FB)), blk(bmap(M_HG_I)),
                  _const_spec((1, HG_KW)), _const_spec((1, HG_KW)), _const_spec((1, HG_KW)), _const_spec((1, HG_KW))]
                 + ([st_in, st_in] if has_state else []),
        out_specs=(pl.BlockSpec((t, HG_W), lambda b, i: (b * nblk + i, 0)),
                   pl.BlockSpec((t, HG_W), lambda b, i: (b * nblk + nblk - 1 - i, 0)),
                   st_out, st_out),
        scratch_shapes=[bd_scratch, bd_scratch] + [pltpu.VMEM((t, HG_KW), F32)] * 10,
        compiler_params=_params(("arbitrary", "arbitrary")),
        name="hgrn_scan",
    )(z, z, z, z, z, z, row(loglb_f), row(l1m_f), row(loglb_b), row(l1m_b), *((s0f, s0b) if has_state else ()))


def _out_kernel(*refs, has_next):
    (x_ref, mod_ref, gpost_ref, ghg_ref, ona_ref, omla_ref, of_ref, ob_ref,
     gna_ref, gmla_ref, ghgate_ref, mg_ref, wna_ref, wmla_ref, whg_ref, wout_ref) = refs[:16]
    y_ref = refs[-2] if has_next else refs[-1]
    o = of_ref[...] + ob_ref[...]
    hr = lax.broadcasted_iota(jnp.int32, (HG_W, HG_W), 0) // HG_DV
    hc = lax.broadcasted_iota(jnp.int32, (HG_W, HG_W), 1) // HG_DV
    head_mean = jnp.where(hr == hc, 1.0 / HG_DV, 0.0).astype(BF16)
    sq = o * o
    sq_hi = sq.astype(BF16)
    sq_lo = (sq - sq_hi.astype(F32)).astype(BF16)
    ms = _mm(sq_hi, head_mean) + _mm(sq_lo, head_mean)
    o_hg = o * lax.rsqrt(ms + NORM_EPS) * ghg_ref[...]

    def branch(o_b, gate_ref, w_ref):
        return _mm((o_b * _silu(gate_ref[...].astype(F32))).astype(BF16), w_ref[...])

    def merge_gate(i):
        return jax.nn.sigmoid(mg_ref[:, i * D_MODEL:(i + 1) * D_MODEL].astype(F32))

    merged = (merge_gate(0) * branch(ona_ref[...], gna_ref, wna_ref)
              + merge_gate(1) * branch(omla_ref[...], gmla_ref, wmla_ref)
              + merge_gate(2) * branch(o_hg, ghgate_ref, whg_ref))
    out = _mm(merged.astype(BF16), wout_ref[...])
    y = x_ref[...] + mod_ref[0, 2:3, :] * _rms(out, gpost_ref[...])
    y_ref[...] = y
    if has_next:
        modn_ref, gpren_ref, hn_ref = refs[16], refs[17], refs[-1]
        hn_ref[...] = _modulated_norm(y, gpren_ref[...], modn_ref).astype(BF16)


def _out_proj(x, z_head, z_mid, z_mg, mod3, g_post, g_hg, o_na, o_mla, o_f, o_b, w_na, w_mla, w_hg, w_out,
              rows_per_mod, mod_base, next_mod3=None, next_g_pre=None):
    m = x.shape[0]
    t = ROW_BLK
    tiles_per_mod = rows_per_mod // t
    has_next = next_mod3 is not None
    w512 = lambda: pl.BlockSpec((t, NA_W), lambda i: (i, 0))
    zcol = lambda col, w: pl.BlockSpec((t, w), lambda i: (i, col // w))
    mod_spec = pl.BlockSpec((1, 3, D_MODEL), lambda i: (mod_base + i // tiles_per_mod, 0, 0))
    row_spec = pl.BlockSpec((t, D_MODEL), lambda i: (i, 0))
    one = pl.Buffered(1)
    out = pl.pallas_call(
        functools.partial(_out_kernel, has_next=has_next),
        out_shape=(jax.ShapeDtypeStruct((m, D_MODEL), F32),)
                  + ((jax.ShapeDtypeStruct((m, D_MODEL), BF16),) if has_next else ()),
        grid=(m // t,),
        in_specs=[row_spec, mod_spec,
                  _const_spec((1, D_MODEL)), _const_spec((1, HG_W)),
                  w512(), w512(), w512(), w512(),
                  zcol(C_NA_G, NA_W), zcol(M_MLA_G, MLA_W), zcol(M_HG_G, HG_W),
                  zcol(0, MG_W),
                  pl.BlockSpec((NA_W, D_MODEL), lambda i: (0, 0), pipeline_mode=one),
                  pl.BlockSpec((MLA_W, D_MODEL), lambda i: (0, 0), pipeline_mode=one),
                  pl.BlockSpec((HG_W, D_MODEL), lambda i: (0, 0), pipeline_mode=one),
                  pl.BlockSpec((D_MODEL, D_MODEL), lambda i: (0, 0), pipeline_mode=one)]
                 + ([mod_spec, _const_spec((1, D_MODEL))] if has_next else []),
        out_specs=(row_spec,) + ((row_spec,) if has_next else ()),
        compiler_params=_params(("arbitrary",)),
        name="out_proj",
    )(x, mod3, g_post, g_hg, o_na, o_mla, o_f, o_b, z_head, z_mid, z_mid, z_mg, w_na, w_mla, w_hg, w_out,
      *((next_mod3, next_g_pre) if has_next else ()))
    return out if has_next else (out[0], None)


def kernel(x_prompt, x_sample, cache_na_k, cache_na_v, cache_mla_ckv, cache_mla_krope, state_hgrn_fwd, state_hgrn_bwd, c, c_ctx, w_ada, b_ada, g_pre, g_post, w_in, na_rpb, g_mla_q, w_mla_uq, g_mla_kv, w_mla_ukv, hg_lb_fwd, hg_lb_bwd, g_hg_out, w_br_na, w_br_mla, w_br_hg, w_out):
    bp, sp, _ = x_prompt.shape
    bs, ss, _ = x_sample.shape
    past = cache_na_k.shape[2]

    cvec = jnp.concatenate([c_ctx[None, :], c, jnp.zeros((8 - 1 - bs, D_MODEL), F32)], axis=0)
    mod = _modulation(cvec, w_ada, b_ada).reshape(DEPTH, 8, 3, D_MODEL)

    lb_f_all = _lower_bounds(hg_lb_fwd)
    lb_b_all = _lower_bounds(hg_lb_bwd)
    cos_t, sin_t = _rope_tables(ss)
    cache_k = cache_na_k.reshape(bs, DEPTH, past, NA_W)
    cache_v = cache_na_v.reshape(bs, DEPTH, past, NA_W)
    cache_kr_pad = _rope_lanes(cache_mla_krope)

    yp = x_prompt.reshape(bp * sp, D_MODEL)
    ys = x_sample.reshape(bs * ss, D_MODEL)
    new_k, new_v, new_ckv, new_kr, new_sf, new_sb = [], [], [], [], [], []
    w_head, w_tail = _pack_w_in(w_in)
    hp = _prenorm(yp, mod[0], g_pre[0][None, :], bp * sp, 0)
    hs = _prenorm(ys, mod[0], g_pre[0][None, :], ss, 1)
    for l in range(DEPTH):
        wq, wqp = _pack_w_uq(w_mla_uq[l])
        wuk, wuv = _pack_w_ukv(w_mla_ukv[l])
        w_na, w_mla, w_hg, w_o = (w.astype(BF16) for w in (w_br_na[l], w_br_mla[l], w_br_hg[l], w_out[l]))
        g_q, g_kv = (g_mla_q[l] * MLA_QK ** -0.5)[None, :], g_mla_kv[l][None, :]
        gpost, ghg = g_post[l][None, :], g_hg_out[l][None, :]
        mod3 = mod[l]
        nxt = (mod[l + 1], g_pre[l + 1][None, :]) if l + 1 < DEPTH else (None, None)

        zp_head, zp_mid, zp_mg = _in_proj(hp, w_head, w_tail, l)
        o_na, o_mla, ckvn, k_new, v_new, kr_new = _prompt_attn(zp_head, bp, g_q, g_kv, wq, wuk, wuv)
        o_f, o_b, sf, sb = _hgrn(zp_mid, bp, lb_f_all[l], lb_b_all[l])
        yp, hp = _out_proj(yp, zp_head, zp_mid, zp_mg, mod3, gpost, ghg, o_na, o_mla, o_f, o_b,
                           w_na, w_mla, w_hg, w_o, bp * sp, 0, *nxt)
        new_k.append(k_new.reshape(bp, sp, NA_HEADS, NA_HD))
        new_v.append(v_new.reshape(bp, sp, NA_HEADS, NA_HD))
        new_ckv.append(ckvn.reshape(bp, sp, MLA_KV_LORA))
        new_kr.append(kr_new.reshape(bp, sp, MLA_ROPE))
        new_sf.append(sf)
        new_sb.append(sb)

        zs_head, zs_mid, zs_mg = _in_proj(hs, w_head, w_tail, l)
        o_na = _sample_na(zs_head, cache_k, cache_v, _na_bias_table(na_rpb[l]), l, bs)
        kq_all, vp_all = _sample_mla_kv(zs_head, cache_mla_ckv, cache_kr_pad, cos_t, sin_t, g_kv, wuk, wuv, l, bs)
        o_mla = _sample_mla_attn(zs_head, cos_t, sin_t, g_q, wq, wqp, kq_all, vp_all, bs)
        o_f, o_b, _, _ = _hgrn(zs_mid, bs, lb_f_all[l], lb_b_all[l], state_hgrn_fwd, state_hgrn_bwd, l)
        ys, hs = _out_proj(ys, zs_head, zs_mid, zs_mg, mod3, gpost, ghg, o_na, o_mla, o_f, o_b,
                           w_na, w_mla, w_hg, w_o, ss, 1, *nxt)

    return (yp.reshape(bp, sp, D_MODEL), ys.reshape(bs, ss, D_MODEL),
            jnp.stack(new_k, axis=1), jnp.stack(new_v, axis=1), jnp.stack(new_ckv, axis=1),
            jnp.stack(new_kr, axis=1), jnp.stack(new_sf, axis=1), jnp.stack(new_sb, axis=1))
```

```python
import functools

import numpy as np
import jax
import jax.numpy as jnp
from jax import lax
from jax.experimental import pallas as pl
from jax.experimental.pallas import tpu as pltpu

D_MODEL = 2048
DEPTH = 2
GRID_W = 64
NORM_EPS = 1e-6
NEG_BIG = -1e30
NA_HEADS = 8
NA_HD = 64
NA_W = NA_HEADS * NA_HD
NA_KR = 8
NA_KW = 16
MLA_HEADS = 8
MLA_NOPE = 64
MLA_ROPE = 32
MLA_VD = 64
MLA_QK = MLA_NOPE + MLA_ROPE
MLA_W = MLA_HEADS * MLA_VD
MLA_Q_LORA = 512
MLA_KV_LORA = 256
ROPE_BASE = 10000.0
HG_HEADS = 8
HG_DK = 64
HG_DV = 64
HG_KW = HG_HEADS * HG_DK
HG_W = HG_HEADS * HG_DV
N_BRANCH = 3
PROJ_WIDTHS = (NA_W, NA_W, NA_W, NA_W, MLA_Q_LORA, MLA_KV_LORA, MLA_ROPE, MLA_W,
               HG_KW, HG_KW, HG_KW, HG_W, HG_W, N_BRANCH * D_MODEL)

F32 = jnp.float32
BF16 = jnp.bfloat16
HI = lax.Precision.HIGHEST

LANE = 128
ROW_BLK = 256
HG_CHUNK = 32
HG_SUB = 16
HG_SAFE_DECAY = 72.0
HG_EXP_CLAMP = 80.0
VMEM_LIMIT = 56 * 1024 * 1024

HEAD_W = 3072
MID_W = 3072
MG_W = N_BRANCH * D_MODEL
SPLIT_COL = 2848
C_NA_Q, C_NA_K, C_NA_V, C_NA_G = 0, 512, 1024, 1536
C_MLA_CQ, C_MLA_CKV, C_MLA_KR, C_MLA_KRP = 2048, 2560, 2816, 2944
M_MLA_G, M_HG_Q, M_HG_FF, M_HG_FB, M_HG_I, M_HG_G = 0, 512, 1024, 1536, 2048, 2560


def _nt(a, b, precision=None):
    return lax.dot_general(a, b, (((1,), (1,)), ((), ())), preferred_element_type=F32, precision=precision)


def _tn(a, b, precision=None):
    return lax.dot_general(a, b, (((0,), (0,)), ((), ())), preferred_element_type=F32, precision=precision)


def _mm(a, b, precision=None):
    return jnp.dot(a, b, preferred_element_type=F32, precision=precision)


def _rms(x, g):
    return x * lax.rsqrt(jnp.mean(x * x, axis=-1, keepdims=True) + NORM_EPS) * g


def _silu(x):
    return x * jax.nn.sigmoid(x)


def _params(sem):
    return pltpu.CompilerParams(dimension_semantics=sem, vmem_limit_bytes=VMEM_LIMIT)


def _const_spec(shape):
    nd = len(shape)
    return pl.BlockSpec(shape, lambda *_: (0,) * nd)


def _rot_half(a):
    parts = []
    for ax in range(2):
        lo_, hi_ = a[..., ax * 16:ax * 16 + 8], a[..., ax * 16 + 8:ax * 16 + 16]
        parts += [-hi_, lo_]
    return jnp.concatenate(parts, axis=-1)


def _pack_w_in(w):
    assert sum(PROJ_WIDTHS[:7]) == SPLIT_COL and sum(PROJ_WIDTHS[7:13]) == MID_W and PROJ_WIDTHS[13] == MG_W
    head = w[..., :SPLIT_COL].astype(BF16)
    kr = head[..., C_MLA_KR:SPLIT_COL]
    return jnp.concatenate([head[..., :C_MLA_KR], _rope_lanes(kr), _rope_lanes(_rot_half(kr))], axis=-1), \
        w[..., SPLIT_COL:].astype(BF16)


def _rope_lanes(a):
    z = lambda n: jnp.zeros(a.shape[:-1] + (n,), a.dtype)
    return jnp.concatenate([z(MLA_NOPE), a, z(LANE - MLA_QK)], axis=-1)


def _pack_w_uq(w):
    w3 = w.reshape(MLA_Q_LORA, MLA_HEADS, MLA_QK)
    nope, rope = w3[..., :MLA_NOPE], w3[..., MLA_NOPE:]
    pad = jnp.zeros(nope.shape[:-1] + (LANE - MLA_QK,), w.dtype)
    flat = lambda a: a.reshape(MLA_Q_LORA, MLA_HEADS * LANE).astype(BF16)
    return flat(jnp.concatenate([nope, rope, pad], axis=-1)), flat(_rope_lanes(_rot_half(rope)))


def _pack_w_ukv(w):
    w3 = w.reshape(MLA_KV_LORA, MLA_HEADS, MLA_NOPE + MLA_VD)
    pad = jnp.zeros((MLA_KV_LORA, MLA_HEADS, LANE - MLA_NOPE), w.dtype)
    flat = lambda a: a.reshape(MLA_KV_LORA, MLA_HEADS * LANE).astype(BF16)
    return flat(jnp.concatenate([w3[..., :MLA_NOPE], pad], -1)), flat(jnp.concatenate([w3[..., MLA_NOPE:], pad], -1))


def _rope_tables(n):
    t = np.arange(n)
    pos = np.stack([t // GRID_W, t % GRID_W]).astype(np.float32)
    axis_dim = MLA_ROPE // 2
    inv = ROPE_BASE ** (-jnp.arange(0, axis_dim, 2, dtype=F32) / axis_dim)
    ang = jnp.asarray(pos)[:, :, None] * inv
    ang = jnp.concatenate([ang, ang], axis=-1)
    ang = jnp.concatenate([ang[0], ang[1]], axis=-1)
    cos_t = jnp.concatenate([jnp.ones((n, MLA_NOPE), F32), jnp.cos(ang), jnp.zeros((n, LANE - MLA_QK), F32)], -1)
    return cos_t, _rope_lanes(jnp.sin(ang))


def _na_bias_table(rpb):
    rows = 32
    c = np.arange(GRID_W)
    win0 = np.clip(c - NA_KW // 2, 0, GRID_W - NA_KW)
    kc = np.arange(GRID_W)
    col_ok = (kc[None, :] >= win0[:, None]) & (kc[None, :] < win0[:, None] + NA_KW)
    rpb = rpb.astype(F32)
    edge = GRID_W - NA_KW
    ext = jnp.concatenate([jnp.broadcast_to(rpb[..., :1], rpb.shape[:-1] + (edge,)), rpb,
                           jnp.broadcast_to(rpb[..., -1:], rpb.shape[:-1] + (edge,))], axis=-1)
    toep = jnp.stack([ext[..., GRID_W - 1 - ci:2 * GRID_W - 1 - ci] for ci in range(GRID_W)], axis=2)
    toep = jnp.where(jnp.asarray(col_ok), toep, NEG_BIG)
    masked = jnp.full((NA_HEADS, GRID_W, GRID_W), NEG_BIG, F32)
    out = []
    for start, r0 in ((0, 0), (4, 8), (20, 28)):
        per_q = []
        for qr in range(4):
            r = r0 + qr
            kr0 = min(max(r - NA_KR // 2, 0), rows - NA_KR)
            tiles = []
            for j in range(12):
                kabs = start + j
                ok = kr0 <= kabs < kr0 + NA_KR
                tiles.append(toep[:, kabs - r + NA_KR - 1] if ok else masked)
            per_q.append(jnp.concatenate(tiles, axis=-1))
        out.append(jnp.concatenate(per_q, axis=1))
    return jnp.stack(out)


def _lower_bounds(p):
    s = jax.nn.softmax(p.astype(F32), axis=0)
    return jnp.cumsum(s, axis=0) - s[0]


def _lb_logs(lb):
    pos = lb > 0
    log_lb = jnp.where(pos, jnp.log(jnp.where(pos, lb, 1.0)), NEG_BIG)
    return log_lb, jnp.log1p(-lb)


def _mod_kernel(c_ref, w_ref, b_ref, o_ref):
    s = _silu(c_ref[...])
    o_ref[0] = _mm(s, w_ref[0], HI) + b_ref[0]


def _modulation(cvec, w_ada, b_ada):
    tn = 1024
    n3 = 3 * D_MODEL
    return pl.pallas_call(
        _mod_kernel,
        out_shape=jax.ShapeDtypeStruct((DEPTH, 8, n3), F32),
        grid=(DEPTH, n3 // tn),
        in_specs=[pl.BlockSpec((8, D_MODEL), lambda l, j: (0, 0)),
                  pl.BlockSpec((1, D_MODEL, tn), lambda l, j: (l, 0, j)),
                  pl.BlockSpec((1, 1, tn), lambda l, j: (l, 0, j))],
        out_specs=pl.BlockSpec((1, 8, tn), lambda l, j: (l, 0, j)),
        compiler_params=_params(("arbitrary", "arbitrary")),
        name="adaln_mod",
    )(cvec, w_ada, b_ada.reshape(DEPTH, 1, n3))


IN_TM, IN_TN = 2048, 1024


def _modulated_norm(x, g, mod_ref):
    return _rms(x, g) * (1.0 + mod_ref[0, 1:2, :]) + mod_ref[0, 0:1, :]


def _prenorm_kernel(x_ref, mod_ref, g_ref, h_ref):
    h_ref[...] = _modulated_norm(x_ref[...], g_ref[...], mod_ref).astype(BF16)


def _prenorm(x, mod3, g_pre, rows_per_mod, mod_base):
    m = x.shape[0]
    t = ROW_BLK
    tiles_per_mod = rows_per_mod // t
    return pl.pallas_call(
        _prenorm_kernel,
        out_shape=jax.ShapeDtypeStruct((m, D_MODEL), BF16),
        grid=(m // t,),
        in_specs=[pl.BlockSpec((t, D_MODEL), lambda i: (i, 0)),
                  pl.BlockSpec((1, 3, D_MODEL), lambda i: (mod_base + i // tiles_per_mod, 0, 0)),
                  _const_spec((1, D_MODEL))],
        out_specs=pl.BlockSpec((t, D_MODEL), lambda i: (i, 0)),
        compiler_params=_params(("arbitrary",)),
        name="prenorm",
    )(x, mod3, g_pre)


def _proj_kernel(h_ref, w_ref, z_ref):
    z_ref[...] = _mm(h_ref[...], w_ref[0]).astype(z_ref.dtype)


def _proj(h, w, layer, tile0, width, out_dtype):
    m = h.shape[0]
    tm, tn = IN_TM, IN_TN
    return pl.pallas_call(
        _proj_kernel,
        out_shape=jax.ShapeDtypeStruct((m, width), out_dtype),
        grid=(m // tm, width // tn),
        in_specs=[pl.BlockSpec((tm, D_MODEL), lambda i, j: (i, 0)),
                  pl.BlockSpec((1, D_MODEL, tn), lambda i, j: (layer, 0, tile0 + j))],
        out_specs=pl.BlockSpec((tm, tn), lambda i, j: (i, j)),
        compiler_params=_params(("arbitrary", "arbitrary")),
        name="in_proj",
    )(h, w)


def _in_proj(h, w_head, w_tail, layer):
    z_head = _proj(h, w_head, layer, 0, HEAD_W, F32)
    z_mid = _proj(h, w_tail, layer, 0, MID_W, BF16)
    z_mg = _proj(h, w_tail, layer, MID_W // IN_TN, MG_W, BF16)
    return z_head, z_mid, z_mg


def _softmax_pv(s_list, v_list):
    m = s_list[0].max(axis=-1, keepdims=True)
    for s in s_list[1:]:
        m = jnp.maximum(m, s.max(axis=-1, keepdims=True))
    acc, den = None, None
    for s, v in zip(s_list, v_list):
        e = jnp.exp(s - m)
        d = e.sum(axis=-1, keepdims=True)
        o = _mm(e.astype(BF16), v)
        acc = o if acc is None else acc + o
        den = d if den is None else den + d
    return acc, den


def _na_head_pairs(q, k_fn, v_fn, scores_fn):
    low = lax.broadcasted_iota(jnp.int32, (1, LANE), 1) < NA_HD
    outs = []
    for p in range(NA_HEADS // 2):
        sl = slice(p * LANE, (p + 1) * LANE)
        q2, k_pair, v_pair = q[:, sl], k_fn(sl), v_fn(sl)
        halves = []
        for half in range(2):
            qh = jnp.where(low if half == 0 else jnp.logical_not(low), q2, 0.0).astype(BF16)
            acc, den = _softmax_pv(scores_fn(qh, k_pair, 2 * p + half), v_pair)
            halves.append(acc / den)
        outs.append(jnp.where(low, halves[0], halves[1]))
    return jnp.concatenate(outs, axis=-1)


def _ones_lane():
    return (lax.broadcasted_iota(jnp.int32, (1, LANE), 1) == MLA_VD).astype(F32)


def _mla_heads(q_fn, kq, vp, o_ref):
    outs = []
    for h in range(MLA_HEADS):
        sl = slice(h * LANE, (h + 1) * LANE)
        s = _nt(q_fn(sl), kq[:, sl])
        e = jnp.exp(s - s.max(axis=-1, keepdims=True))
        acc = _mm(e.astype(BF16), vp[:, sl])
        outs.append(acc[:, :MLA_VD] / acc[:, MLA_VD:MLA_VD + 1])
    o_ref[...] = jnp.concatenate(outs, axis=-1)


def _mla_keys_values(ckvn16, kr_tile, wuk_ref, wuv_ref):
    kq = _mm(ckvn16, wuk_ref[...]) + jnp.concatenate([kr_tile] * MLA_HEADS, axis=-1)
    vp = _mm(ckvn16, wuv_ref[...]) + jnp.concatenate([_ones_lane()] * MLA_HEADS, axis=-1)
    return kq.astype(BF16), vp.astype(BF16)


def _prompt_attn_kernel(qkv_ref, cq_ref, ckv_ref, kr_ref, gq_ref, gkv_ref, wq_ref, wuk_ref, wuv_ref,
                        ona_ref, omla_ref, ckvn_ref, k_ref, v_ref, kro_ref):
    qkv = qkv_ref[...]
    k_ref[...] = qkv[:, NA_W:2 * NA_W]
    v_ref[...] = qkv[:, 2 * NA_W:3 * NA_W]
    kro_ref[...] = kr_ref[:, MLA_NOPE:MLA_QK]
    ckvn = _rms(ckv_ref[...], gkv_ref[...])
    ckvn_ref[...] = ckvn
    scale = NA_HD ** -0.5
    outs = []
    for h in range(NA_HEADS):
        qh = (qkv[:, h * NA_HD:(h + 1) * NA_HD] * scale).astype(BF16)
        kh = qkv[:, NA_W + h * NA_HD:NA_W + (h + 1) * NA_HD].astype(BF16)
        vh = qkv[:, 2 * NA_W + h * NA_HD:2 * NA_W + (h + 1) * NA_HD].astype(BF16)
        acc, den = _softmax_pv([_nt(qh, kh)], [vh])
        outs.append(acc / den)
    ona_ref[...] = jnp.concatenate(outs, axis=-1)

    cqn = _rms(cq_ref[...], gq_ref[...]).astype(BF16)
    q = _mm(cqn, wq_ref[...]).astype(BF16)
    kq, vp = _mla_keys_values(ckvn.astype(BF16), kr_ref[...], wuk_ref, wuv_ref)
    _mla_heads(lambda sl: q[:, sl], kq, vp, omla_ref)


def _prompt_attn(z, n_batch, g_q, g_kv, wq, wuk, wuv):
    m = z.shape[0]
    t = ROW_BLK
    hw = MLA_HEADS * LANE
    return pl.pallas_call(
        _prompt_attn_kernel,
        out_shape=(jax.ShapeDtypeStruct((m, NA_W), F32), jax.ShapeDtypeStruct((m, MLA_W), F32),
                   jax.ShapeDtypeStruct((m, MLA_KV_LORA), F32), jax.ShapeDtypeStruct((m, NA_W), F32),
                   jax.ShapeDtypeStruct((m, NA_W), F32), jax.ShapeDtypeStruct((m, MLA_ROPE), F32)),
        grid=(n_batch,),
        in_specs=[pl.BlockSpec((t, 3 * NA_W), lambda b: (b, C_NA_Q // (3 * NA_W))),
                  pl.BlockSpec((t, MLA_Q_LORA), lambda b: (b, C_MLA_CQ // MLA_Q_LORA)),
                  pl.BlockSpec((t, MLA_KV_LORA), lambda b: (b, C_MLA_CKV // MLA_KV_LORA)),
                  pl.BlockSpec((t, LANE), lambda b: (b, C_MLA_KR // LANE)),
                  _const_spec((1, MLA_Q_LORA)), _const_spec((1, MLA_KV_LORA)),
                  _const_spec((MLA_Q_LORA, hw)), _const_spec((MLA_KV_LORA, hw)), _const_spec((MLA_KV_LORA, hw))],
        out_specs=(pl.BlockSpec((t, NA_W), lambda b: (b, 0)), pl.BlockSpec((t, MLA_W), lambda b: (b, 0)),
                   pl.BlockSpec((t, MLA_KV_LORA), lambda b: (b, 0)), pl.BlockSpec((t, NA_W), lambda b: (b, 0)),
                   pl.BlockSpec((t, NA_W), lambda b: (b, 0)), pl.BlockSpec((t, MLA_ROPE), lambda b: (b, 0))),
        compiler_params=_params(("arbitrary",)),
        name="prompt_attn",
    )(z, z, z, z, g_q, g_kv, wq, wuk, wuv)


def _sample_na_kernel(q_ref, k0_ref, k1_ref, k2_ref, v0_ref, v1_ref, v2_ref, kc_ref, vc_ref, bias_ref, o_ref):
    q = q_ref[...] * NA_HD ** -0.5
    kl = [r[...] for r in (k0_ref, k1_ref, k2_ref)]
    vl = [r[...] for r in (v0_ref, v1_ref, v2_ref)]
    kc, vc = kc_ref[0, 0], vc_ref[0, 0]

    def scores(qh, k_pair, h):
        s_list = [_nt(qh, k_pair[i]) + bias_ref[0, h, :, i * ROW_BLK:(i + 1) * ROW_BLK] for i in range(3)]
        return s_list + [_nt(qh, k_pair[3])]

    o_ref[...] = _na_head_pairs(q, lambda sl: [a[:, sl].astype(BF16) for a in kl + [kc]],
                                lambda sl: [a[:, sl].astype(BF16) for a in vl + [vc]], scores)


def _sample_na(z, cache_k, cache_v, bias_tab, layer, n_batch):
    m = z.shape[0]
    t = ROW_BLK
    nblk = m // n_batch // t
    past = cache_k.shape[2]

    def kv_map(col, i):
        return lambda b, rb: (b * nblk + jnp.clip(rb - 1, 0, nblk - 3) + i, col // NA_W)

    def variant(b, rb):
        return (jnp.where(rb == 0, 0, jnp.where(rb == nblk - 1, 2, 1)), 0, 0, 0)

    cache_spec = pl.BlockSpec((1, 1, past, NA_W), lambda b, rb: (b, layer, 0, 0))
    return pl.pallas_call(
        _sample_na_kernel,
        out_shape=jax.ShapeDtypeStruct((m, NA_W), F32),
        grid=(n_batch, nblk),
        in_specs=[pl.BlockSpec((t, NA_W), lambda b, rb: (b * nblk + rb, C_NA_Q // NA_W))]
                 + [pl.BlockSpec((t, NA_W), kv_map(C_NA_K, i)) for i in range(3)]
                 + [pl.BlockSpec((t, NA_W), kv_map(C_NA_V, i)) for i in range(3)]
                 + [cache_spec, cache_spec,
                    pl.BlockSpec((1, NA_HEADS, t, 3 * t), variant)],
        out_specs=pl.BlockSpec((t, NA_W), lambda b, rb: (b * nblk + rb, 0)),
        compiler_params=_params(("arbitrary", "arbitrary")),
        name="sample_na",
    )(z, z, z, z, z, z, z, cache_k, cache_v, bias_tab)


def _sample_mla_kv_kernel(cckv_ref, ckr_ref, ckv_ref, kr_ref, krp_ref, cos_ref, sin_ref, gkv_ref, wuk_ref, wuv_ref,
                          kq_ref, vp_ref):
    j = pl.program_id(1)

    @pl.when(j == 0)
    def _():
        kq_ref[0], vp_ref[0] = _mla_keys_values(cckv_ref[0, 0].astype(BF16), ckr_ref[0, 0], wuk_ref, wuv_ref)

    @pl.when(j > 0)
    def _():
        ckvn = _rms(ckv_ref[...], gkv_ref[...])
        kr_roped = kr_ref[...] * cos_ref[...] + krp_ref[...] * sin_ref[...]
        kq_ref[0], vp_ref[0] = _mla_keys_values(ckvn.astype(BF16), kr_roped, wuk_ref, wuv_ref)


def _sample_mla_kv(z, cache_ckv, cache_kr_pad, cos_t, sin_t, g_kv, wuk, wuv, layer, n_batch):
    m = z.shape[0]
    n = m // n_batch
    past = cache_ckv.shape[2]
    t = past
    nb = n // t
    hw = MLA_HEADS * LANE

    def zrow(b, j):
        return b * nb + jnp.maximum(j - 1, 0)

    return pl.pallas_call(
        _sample_mla_kv_kernel,
        out_shape=(jax.ShapeDtypeStruct((n_batch, past + n, hw), BF16),
                   jax.ShapeDtypeStruct((n_batch, past + n, hw), BF16)),
        grid=(n_batch, nb + 1),
        in_specs=[pl.BlockSpec((1, 1, past, MLA_KV_LORA), lambda b, j: (b, layer, 0, 0)),
                  pl.BlockSpec((1, 1, past, LANE), lambda b, j: (b, layer, 0, 0)),
                  pl.BlockSpec((t, MLA_KV_LORA), lambda b, j: (zrow(b, j), C_MLA_CKV // MLA_KV_LORA)),
                  pl.BlockSpec((t, LANE), lambda b, j: (zrow(b, j), C_MLA_KR // LANE)),
                  pl.BlockSpec((t, LANE), lambda b, j: (zrow(b, j), C_MLA_KRP // LANE)),
                  pl.BlockSpec((t, LANE), lambda b, j: (jnp.maximum(j - 1, 0), 0)),
                  pl.BlockSpec((t, LANE), lambda b, j: (jnp.maximum(j - 1, 0), 0)),
                  _const_spec((1, MLA_KV_LORA)), _const_spec((MLA_KV_LORA, hw)), _const_spec((MLA_KV_LORA, hw))],
        out_specs=(pl.BlockSpec((1, t, hw), lambda b, j: (b, j, 0)),
                   pl.BlockSpec((1, t, hw), lambda b, j: (b, j, 0))),
        compiler_params=_params(("arbitrary", "arbitrary")),
        name="sample_mla_kv",
    )(cache_ckv, cache_kr_pad, z, z, z, cos_t, sin_t, g_kv, wuk, wuv)


def _sample_mla_attn_kernel(cq_ref, cos_ref, sin_ref, gq_ref, wq_ref, wqp_ref, kq_ref, vp_ref, o_ref):
    cqn = _rms(cq_ref[...], gq_ref[...]).astype(BF16)
    q = _mm(cqn, wq_ref[...])
    q_rot = _mm(cqn, wqp_ref[...])
    cos, sin = cos_ref[...], sin_ref[...]
    _mla_heads(lambda sl: (q[:, sl] * cos + q_rot[:, sl] * sin).astype(BF16), kq_ref[0], vp_ref[0], o_ref)


def _sample_mla_attn(z, cos_t, sin_t, g_q, wq, wqp, kq_all, vp_all, n_batch):
    m = z.shape[0]
    t = ROW_BLK
    nblk = m // n_batch // t
    nk = kq_all.shape[1]
    hw = MLA_HEADS * LANE
    return pl.pallas_call(
        _sample_mla_attn_kernel,
        out_shape=jax.ShapeDtypeStruct((m, MLA_W), F32),
        grid=(n_batch, nblk),
        in_specs=[pl.BlockSpec((t, MLA_Q_LORA), lambda b, i: (b * nblk + i, C_MLA_CQ // MLA_Q_LORA)),
                  pl.BlockSpec((t, LANE), lambda b, i: (i, 0)),
                  pl.BlockSpec((t, LANE), lambda b, i: (i, 0)),
                  _const_spec((1, MLA_Q_LORA)),
                  _const_spec((MLA_Q_LORA, hw)), _const_spec((MLA_Q_LORA, hw)),
                  pl.BlockSpec((1, nk, hw), lambda b, i: (b, 0, 0)),
                  pl.BlockSpec((1, nk, hw), lambda b, i: (b, 0, 0))],
        out_specs=pl.BlockSpec((t, MLA_W), lambda b, i: (b * nblk + i, 0)),
        compiler_params=_params(("arbitrary", "arbitrary")),
        name="sample_mla_attn",
    )(z, cos_t, sin_t, g_q, wq, wqp, kq_all, vp_all)


def _hg_direction(fwd, q_ref, zf_ref, v_ref, loglb_ref, l1m_ref, s_ref, o_ref, qs, ks, bs, vs, os):
    t = ROW_BLK
    nchunk = t // HG_CHUNK
    q = _silu(q_ref[...].astype(F32))
    zf = zf_ref[...].astype(F32)
    v = v_ref[...].astype(F32)
    soft = jnp.log(1.0 + jnp.exp(-jnp.abs(zf)))
    a1 = jnp.broadcast_to(loglb_ref[...], zf.shape)
    a2 = l1m_ref[...] + (jnp.minimum(zf, 0.0) - soft)
    logf = jnp.maximum(a1, a2) + jnp.log(1.0 + jnp.exp(-jnp.abs(a1 - a2)))
    k = jnp.exp(l1m_ref[...] - jnp.maximum(zf, 0.0) - soft)

    def chunk_masks(n):
        ri = lax.broadcasted_iota(jnp.int32, (n, n), 0)
        ci = lax.broadcasted_iota(jnp.int32, (n, n), 1)
        causal = (ri >= ci) if fwd else (ri <= ci)
        same_sub = (ri // HG_SUB) == (ci // HG_SUB)
        same = (ri // HG_CHUNK) == (ci // HG_CHUNK)
        return same_sub & causal, same & jnp.logical_not(same_sub) & causal, same_sub, same

    tri_sub, _, same_sub, same = chunk_masks(t)
    hi = logf.astype(BF16)
    lo = (logf - hi.astype(F32)).astype(BF16)

    def seg_sum(mask):
        m16 = jnp.where(mask, 1.0, 0.0).astype(BF16)
        return _mm(m16, hi) + _mm(m16, lo)

    b_sub = seg_sum(tri_sub)
    nsub = t // HG_SUB
    b3 = b_sub.reshape(nsub, HG_SUB, HG_KW)
    edge = b3[:, HG_SUB - 1:HG_SUB, :] if fwd else b3[:, 0:1, :]
    tot_sub = jnp.broadcast_to(edge, (nsub, HG_SUB, HG_KW)).reshape(t, HG_KW)
    pair = edge.reshape(nchunk, 2, 1, HG_KW)
    tot = jnp.broadcast_to(pair[:, 0:1] + pair[:, 1:2], (nchunk, 2, HG_SUB, HG_KW)).reshape(t, HG_KW)
    row = lax.broadcasted_iota(jnp.int32, (t, 1), 0) % HG_CHUNK
    later = (row >= HG_SUB) if fwd else (row < HG_SUB)
    b = b_sub + jnp.where(later, tot - tot_sub, 0.0)

    qt_sub = (q * jnp.exp(b_sub)).astype(BF16)
    kt_sub = (k * jnp.exp(jnp.minimum(-b_sub, HG_EXP_CLAMP))).astype(BF16)
    kh_sub = (k * jnp.exp(tot_sub - b_sub)).astype(BF16)
    qt16 = (q * jnp.exp(b)).astype(BF16)
    kh16 = (k * jnp.exp(tot - b)).astype(BF16)
    v16 = v.astype(BF16)

    hb = t // 2
    tri_hb, cross_hb, _, _ = chunk_masks(hb)
    low = lax.broadcasted_iota(jnp.int32, (1, LANE), 1) < HG_DK
    zero16 = jnp.zeros((), BF16)
    outs = []
    for p2 in range(HG_HEADS // 2):
        sl = slice(p2 * LANE, (p2 + 1) * LANE)
        parts = []
        for r0 in (0, hb):
            rs = slice(r0, r0 + hb)
            keys = jnp.concatenate([kt_sub[rs, sl], kh_sub[rs, sl]], axis=0)
            halves = []
            for keep in (low, jnp.logical_not(low)):
                p = _nt(jnp.where(keep, qt_sub[rs, sl], zero16), keys)
                a = jnp.where(tri_hb, p[:, :hb], 0.0) + jnp.where(cross_hb, p[:, hb:], 0.0)
                halves.append(_mm(a.astype(BF16), v16[rs, sl]))
            parts.append(jnp.where(low, halves[0], halves[1]))
        outs.append(jnp.concatenate(parts, axis=0))
    o_ref[...] = jnp.concatenate(outs, axis=-1)

    gw = 4 * HG_DK
    bd_r = lax.broadcasted_iota(jnp.int32, (gw, gw), 0) // HG_DV
    bd_c = lax.broadcasted_iota(jnp.int32, (gw, gw), 1) // HG_DK
    bd = bd_r == bd_c
    order = range(nchunk) if fwd else range(nchunk - 1, -1, -1)
    for g in range(2):
        ls = slice(g * gw, (g + 1) * gw)
        rows = {c: slice(c * HG_CHUNK, (c + 1) * HG_CHUNK) for c in order}
        updates = {c: jnp.where(bd, _tn(v16[rows[c], ls], kh16[rows[c], ls]), 0.0) for c in order}
        s = s_ref[g]
        entering = {}
        for c in order:
            entering[c] = s.astype(BF16)
            s = s * jnp.exp(tot[c * HG_CHUNK:c * HG_CHUNK + 1, ls]) + updates[c]
        s_ref[g] = s
        for c in order:
            inter = _nt(qt16[rows[c], ls], entering[c])
            o_ref[rows[c], ls] += inter
            os[rows[c], ls] = inter

    qs[...] = q
    ks[...] = k
    bs[...] = b
    vs[...] = v
    return [jnp.max(-jnp.minimum(tot_sub[c * HG_CHUNK:c * HG_CHUNK + 1, :],
                                 tot_sub[c * HG_CHUNK + HG_SUB:c * HG_CHUNK + HG_SUB + 1, :]))
            for c in range(nchunk)]


def _hg_pairwise(fwd, worst, o_ref, qs, ks, bs, vs, os):
    lane = lax.broadcasted_iota(jnp.int32, (HG_KW, LANE), 0) // HG_DK
    col = lax.broadcasted_iota(jnp.int32, (HG_KW, LANE), 1)
    head_sum = (lane == col).astype(F32)
    srow = lax.broadcasted_iota(jnp.int32, (HG_CHUNK, 1), 0)
    for c, worst_c in enumerate(worst):
        r0 = c * HG_CHUNK

        @pl.when(worst_c > HG_SAFE_DECAY)
        def _():
            kc = ks[r0:r0 + HG_CHUNK, :]
            bc = bs[r0:r0 + HG_CHUNK, :]
            vc = vs[r0:r0 + HG_CHUNK, :]

            def body(i, carry):
                qrow = qs[pl.ds(r0 + i, 1), :]
                brow = bs[pl.ds(r0 + i, 1), :]
                p = qrow * kc * jnp.exp(jnp.minimum(brow - bc, 0.0))
                keep = (srow <= i) if fwd else (srow >= i)
                p = jnp.where(keep, p, 0.0)
                a = _mm(p, head_sum, HI)
                a_full = _nt(a, head_sum, HI)
                o_ref[pl.ds(r0 + i, 1), :] = (jnp.sum(a_full * vc, axis=0, keepdims=True)
                                              + os[pl.ds(r0 + i, 1), :])
                return carry

            lax.fori_loop(0, HG_CHUNK, body, 0)


def _head_block(h):
    g, hh = divmod(h, 4)
    return g, slice(hh * HG_DV, (hh + 1) * HG_DV), slice(hh * HG_DK, (hh + 1) * HG_DK)


def _hgrn_kernel(*refs, has_state):
    (qf_ref, ff_ref, vf_ref, qb_ref, fb_ref, vb_ref, lbf_ref, l1f_ref, lbb_ref, l1b_ref) = refs[:10]
    s0_refs = refs[10:12] if has_state else (None, None)
    rest = refs[12:] if has_state else refs[10:]
    of_ref, ob_ref, sf_out, sb_out, sf, sb = rest[:6]
    scr_f, scr_b = rest[6:11], rest[11:16]
    i = pl.program_id(1)

    @pl.when(i == 0)
    def _():
        for scr, s0_ref in zip((sf, sb), s0_refs):
            scr[...] = jnp.zeros(scr.shape, F32)
            if s0_ref is not None:
                for h in range(HG_HEADS):
                    g, rv, ck = _head_block(h)
                    scr[g, rv, ck] = s0_ref[0, 0, h].T

    worst_f = _hg_direction(True, qf_ref, ff_ref, vf_ref, lbf_ref, l1f_ref, sf, of_ref, *scr_f)
    worst_b = _hg_direction(False, qb_ref, fb_ref, vb_ref, lbb_ref, l1b_ref, sb, ob_ref, *scr_b)

    @pl.when(functools.reduce(jnp.maximum, worst_f + worst_b) > HG_SAFE_DECAY)
    def _():
        _hg_pairwise(True, worst_f, of_ref, *scr_f)
        _hg_pairwise(False, worst_b, ob_ref, *scr_b)

    @pl.when(i == pl.num_programs(1) - 1)
    def _():
        for scr, out in ((sf, sf_out), (sb, sb_out)):
            for h in range(HG_HEADS):
                g, rv, ck = _head_block(h)
                out[0, h] = scr[g, rv, ck].T


def _hgrn(z, n_batch, lb_f, lb_b, s0f=None, s0b=None, layer=0):
    m = z.shape[0]
    t = ROW_BLK
    nblk = m // n_batch // t
    has_state = s0f is not None
    loglb_f, l1m_f = _lb_logs(lb_f)
    loglb_b, l1m_b = _lb_logs(lb_b)
    row = lambda a: a.reshape(1, HG_KW)

    def fmap(col):
        return lambda b, i: (b * nblk + i, col // HG_KW)

    def bmap(col):
        return lambda b, i: (b * nblk + nblk - 1 - i, col // HG_KW)

    blk = lambda imap: pl.BlockSpec((t, HG_KW), imap)
    st_out = pl.BlockSpec((1, HG_HEADS, HG_DK, HG_DV), lambda b, i: (b, 0, 0, 0))
    st_shape = jax.ShapeDtypeStruct((n_batch, HG_HEADS, HG_DK, HG_DV), F32)
    st_in = pl.BlockSpec((1, 1, HG_HEADS, HG_DK, HG_DV), lambda b, i: (b, layer, 0, 0, 0))
    bd_scratch = pltpu.VMEM((2, 4 * HG_DV, 4 * HG_DK), F32)
    return pl.pallas_call(
        functools.partial(_hgrn_kernel, has_state=has_state),
        out_shape=(jax.ShapeDtypeStruct((m, HG_W), F32), jax.ShapeDtypeStruct((m, HG_W), F32), st_shape, st_shape),
        grid=(n_batch, nblk),
        in_specs=[blk(fmap(M_HG_Q)), blk(fmap(M_HG_FF)), blk(fmap(M_HG_I)),
                  blk(bmap(M_HG_Q)), blk(bmap(M_HG_FB)), blk(bmap(M_HG_I)),
                  _const_spec((1, HG_KW)), _const_spec((1, HG_KW)), _const_spec((1, HG_KW)), _const_spec((1, HG_KW))]
                 + ([st_in, st_in] if has_state else []),
        out_specs=(pl.BlockSpec((t, HG_W), lambda b, i: (b * nblk + i, 0)),
                   pl.BlockSpec((t, HG_W), lambda b, i: (b * nblk + nblk - 1 - i, 0)),
                   st_out, st_out),
        scratch_shapes=[bd_scratch, bd_scratch] + [pltpu.VMEM((t, HG_KW), F32)] * 10,
        compiler_params=_params(("arbitrary", "arbitrary")),
        name="hgrn_scan",
    )(z, z, z, z, z, z, row(loglb_f), row(l1m_f), row(loglb_b), row(l1m_b), *((s0f, s0b) if has_state else ()))


def _out_kernel(*refs, has_next):
    (x_ref, mod_ref, gpost_ref, ghg_ref, ona_ref, omla_ref, of_ref, ob_ref,
     gna_ref, gmla_ref, ghgate_ref, mg_ref, wna_ref, wmla_ref, whg_ref, wout_ref) = refs[:16]
    y_ref = refs[-2] if has_next else refs[-1]
    o = of_ref[...] + ob_ref[...]
    hr = lax.broadcasted_iota(jnp.int32, (HG_W, HG_W), 0) // HG_DV
    hc = lax.broadcasted_iota(jnp.int32, (HG_W, HG_W), 1) // HG_DV
    head_mean = jnp.where(hr == hc, 1.0 / HG_DV, 0.0).astype(BF16)
    sq = o * o
    sq_hi = sq.astype(BF16)
    sq_lo = (sq - sq_hi.astype(F32)).astype(BF16)
    ms = _mm(sq_hi, head_mean) + _mm(sq_lo, head_mean)
    o_hg = o * lax.rsqrt(ms + NORM_EPS) * ghg_ref[...]

    def branch(o_b, gate_ref, w_ref):
        return _mm((o_b * _silu(gate_ref[...].astype(F32))).astype(BF16), w_ref[...])

    def merge_gate(i):
        return jax.nn.sigmoid(mg_ref[:, i * D_MODEL:(i + 1) * D_MODEL].astype(F32))

    merged = (merge_gate(0) * branch(ona_ref[...], gna_ref, wna_ref)
              + merge_gate(1) * branch(omla_ref[...], gmla_ref, wmla_ref)
              + merge_gate(2) * branch(o_hg, ghgate_ref, whg_ref))
    out = _mm(merged.astype(BF16), wout_ref[...])
    y = x_ref[...] + mod_ref[0, 2:3, :] * _rms(out, gpost_ref[...])
    y_ref[...] = y
    if has_next:
        modn_ref, gpren_ref, hn_ref = refs[16], refs[17], refs[-1]
        hn_ref[...] = _modulated_norm(y, gpren_ref[...], modn_ref).astype(BF16)


def _out_proj(x, z_head, z_mid, z_mg, mod3, g_post, g_hg, o_na, o_mla, o_f, o_b, w_na, w_mla, w_hg, w_out,
              rows_per_mod, mod_base, next_mod3=None, next_g_pre=None):
    m = x.shape[0]
    t = ROW_BLK
    tiles_per_mod = rows_per_mod // t
    has_next = next_mod3 is not None
    w512 = lambda: pl.BlockSpec((t, NA_W), lambda i: (i, 0))
    zcol = lambda col, w: pl.BlockSpec((t, w), lambda i: (i, col // w))
    mod_spec = pl.BlockSpec((1, 3, D_MODEL), lambda i: (mod_base + i // tiles_per_mod, 0, 0))
    row_spec = pl.BlockSpec((t, D_MODEL), lambda i: (i, 0))
    one = pl.Buffered(1)
    out = pl.pallas_call(
        functools.partial(_out_kernel, has_next=has_next),
        out_shape=(jax.ShapeDtypeStruct((m, D_MODEL), F32),)
                  + ((jax.ShapeDtypeStruct((m, D_MODEL), BF16),) if has_next else ()),
        grid=(m // t,),
        in_specs=[row_spec, mod_spec,
                  _const_spec((1, D_MODEL)), _const_spec((1, HG_W)),
                  w512(), w512(), w512(), w512(),
                  zcol(C_NA_G, NA_W), zcol(M_MLA_G, MLA_W), zcol(M_HG_G, HG_W),
                  zcol(0, MG_W),
                  pl.BlockSpec((NA_W, D_MODEL), lambda i: (0, 0), pipeline_mode=one),
                  pl.BlockSpec((MLA_W, D_MODEL), lambda i: (0, 0), pipeline_mode=one),
                  pl.BlockSpec((HG_W, D_MODEL), lambda i: (0, 0), pipeline_mode=one),
                  pl.BlockSpec((D_MODEL, D_MODEL), lambda i: (0, 0), pipeline_mode=one)]
                 + ([mod_spec, _const_spec((1, D_MODEL))] if has_next else []),
        out_specs=(row_spec,) + ((row_spec,) if has_next else ()),
        compiler_params=_params(("arbitrary",)),
        name="out_proj",
    )(x, mod3, g_post, g_hg, o_na, o_mla, o_f, o_b, z_head, z_mid, z_mid, z_mg, w_na, w_mla, w_hg, w_out,
      *((next_mod3, next_g_pre) if has_next else ()))
    return out if has_next else (out[0], None)


def kernel(x_prompt, x_sample, cache_na_k, cache_na_v, cache_mla_ckv, cache_mla_krope, state_hgrn_fwd, state_hgrn_bwd, c, c_ctx, w_ada, b_ada, g_pre, g_post, w_in, na_rpb, g_mla_q, w_mla_uq, g_mla_kv, w_mla_ukv, hg_lb_fwd, hg_lb_bwd, g_hg_out, w_br_na, w_br_mla, w_br_hg, w_out):
    bp, sp, _ = x_prompt.shape
    bs, ss, _ = x_sample.shape
    past = cache_na_k.shape[2]

    cvec = jnp.concatenate([c_ctx[None, :], c, jnp.zeros((8 - 1 - bs, D_MODEL), F32)], axis=0)
    mod = _modulation(cvec, w_ada, b_ada).reshape(DEPTH, 8, 3, D_MODEL)

    lb_f_all = _lower_bounds(hg_lb_fwd)
    lb_b_all = _lower_bounds(hg_lb_bwd)
    cos_t, sin_t = _rope_tables(ss)
    cache_k = cache_na_k.reshape(bs, DEPTH, past, NA_W)
    cache_v = cache_na_v.reshape(bs, DEPTH, past, NA_W)
    cache_kr_pad = _rope_lanes(cache_mla_krope)

    yp = x_prompt.reshape(bp * sp, D_MODEL)
    ys = x_sample.reshape(bs * ss, D_MODEL)
    new_k, new_v, new_ckv, new_kr, new_sf, new_sb = [], [], [], [], [], []
    w_head, w_tail = _pack_w_in(w_in)
    hp = _prenorm(yp, mod[0], g_pre[0][None, :], bp * sp, 0)
    hs = _prenorm(ys, mod[0], g_pre[0][None, :], ss, 1)
    for l in range(DEPTH):
        wq, wqp = _pack_w_uq(w_mla_uq[l])
        wuk, wuv = _pack_w_ukv(w_mla_ukv[l])
        w_na, w_mla, w_hg, w_o = (w.astype(BF16) for w in (w_br_na[l], w_br_mla[l], w_br_hg[l], w_out[l]))
        g_q, g_kv = (g_mla_q[l] * MLA_QK ** -0.5)[None, :], g_mla_kv[l][None, :]
        gpost, ghg = g_post[l][None, :], g_hg_out[l][None, :]
        mod3 = mod[l]
        nxt = (mod[l + 1], g_pre[l + 1][None, :]) if l + 1 < DEPTH else (None, None)

        zp_head, zp_mid, zp_mg = _in_proj(hp, w_head, w_tail, l)
        o_na, o_mla, ckvn, k_new, v_new, kr_new = _prompt_attn(zp_head, bp, g_q, g_kv, wq, wuk, wuv)
        o_f, o_b, sf, sb = _hgrn(zp_mid, bp, lb_f_all[l], lb_b_all[l])
        yp, hp = _out_proj(yp, zp_head, zp_mid, zp_mg, mod3, gpost, ghg, o_na, o_mla, o_f, o_b,
                           w_na, w_mla, w_hg, w_o, bp * sp, 0, *nxt)
        new_k.append(k_new.reshape(bp, sp, NA_HEADS, NA_HD))
        new_v.append(v_new.reshape(bp, sp, NA_HEADS, NA_HD))
        new_ckv.append(ckvn.reshape(bp, sp, MLA_KV_LORA))
        new_kr.append(kr_new.reshape(bp, sp, MLA_ROPE))
        new_sf.append(sf)
        new_sb.append(sb)

        zs_head, zs_mid, zs_mg = _in_proj(hs, w_head, w_tail, l)
        o_na = _sample_na(zs_head, cache_k, cache_v, _na_bias_table(na_rpb[l]), l, bs)
        kq_all, vp_all = _sample_mla_kv(zs_head, cache_mla_ckv, cache_kr_pad, cos_t, sin_t, g_kv, wuk, wuv, l, bs)
        o_mla = _sample_mla_attn(zs_head, cos_t, sin_t, g_q, wq, wqp, kq_all, vp_all, bs)
        o_f, o_b, _, _ = _hgrn(zs_mid, bs, lb_f_all[l], lb_b_all[l], state_hgrn_fwd, state_hgrn_bwd, l)
        ys, hs = _out_proj(ys, zs_head, zs_mid, zs_mg, mod3, gpost, ghg, o_na, o_mla, o_f, o_b,
                           w_na, w_mla, w_hg, w_o, ss, 1, *nxt)

    return (yp.reshape(bp, sp, D_MODEL), ys.reshape(bs, ss, D_MODEL),
            jnp.stack(new_k, axis=1), jnp.stack(new_v, axis=1), jnp.stack(new_ckv, axis=1),
            jnp.stack(new_kr, axis=1), jnp.stack(new_sf, axis=1), jnp.stack(new_sb, axis=1))
```

```python
import functools

import numpy as np
import jax
import jax.numpy as jnp
from jax import lax
from jax.experimental import pallas as pl
from jax.experimental.pallas import tpu as pltpu

D_MODEL = 2048
DEPTH = 2
GRID_W = 64
NORM_EPS = 1e-6
NEG_BIG = -1e30
NA_HEADS = 8
NA_HD = 64
NA_W = NA_HEADS * NA_HD
NA_KR = 8
NA_KW = 16
MLA_HEADS = 8
MLA_NOPE = 64
MLA_ROPE = 32
MLA_VD = 64
MLA_QK = MLA_NOPE + MLA_ROPE
MLA_W = MLA_HEADS * MLA_VD
MLA_Q_LORA = 512
MLA_KV_LORA = 256
ROPE_BASE = 10000.0
HG_HEADS = 8
HG_DK = 64
HG_DV = 64
HG_KW = HG_HEADS * HG_DK
HG_W = HG_HEADS * HG_DV
N_BRANCH = 3
PROJ_WIDTHS = (NA_W, NA_W, NA_W, NA_W, MLA_Q_LORA, MLA_KV_LORA, MLA_ROPE, MLA_W,
               HG_KW, HG_KW, HG_KW, HG_W, HG_W, N_BRANCH * D_MODEL)

F32 = jnp.float32
BF16 = jnp.bfloat16
HI = lax.Precision.HIGHEST

LANE = 128
ROW_BLK = 256
HG_CHUNK = 32
HG_SUB = 16
HG_SAFE_DECAY = 72.0
HG_EXP_CLAMP = 80.0
VMEM_LIMIT = 56 * 1024 * 1024

HEAD_W = 3072
MID_W = 3072
MG_W = N_BRANCH * D_MODEL
SPLIT_COL = 2848
C_NA_Q, C_NA_K, C_NA_V, C_NA_G = 0, 512, 1024, 1536
C_MLA_CQ, C_MLA_CKV, C_MLA_KR, C_MLA_KRP = 2048, 2560, 2816, 2944
M_MLA_G, M_HG_Q, M_HG_FF, M_HG_FB, M_HG_I, M_HG_G = 0, 512, 1024, 1536, 2048, 2560


def _nt(a, b, precision=None):
    return lax.dot_general(a, b, (((1,), (1,)), ((), ())), preferred_element_type=F32, precision=precision)


def _tn(a, b, precision=None):
    return lax.dot_general(a, b, (((0,), (0,)), ((), ())), preferred_element_type=F32, precision=precision)


def _mm(a, b, precision=None):
    return jnp.dot(a, b, preferred_element_type=F32, precision=precision)


def _rms(x, g):
    return x * lax.rsqrt(jnp.mean(x * x, axis=-1, keepdims=True) + NORM_EPS) * g


def _sigmoid(x):
    return 0.5 * jnp.tanh(0.5 * x) + 0.5


def _silu(x):
    return x * _sigmoid(x)


def _params(sem):
    return pltpu.CompilerParams(dimension_semantics=sem, vmem_limit_bytes=VMEM_LIMIT)


def _const_spec(shape):
    nd = len(shape)
    return pl.BlockSpec(shape, lambda *_: (0,) * nd)


def _rot_half(a):
    parts = []
    for ax in range(2):
        lo_, hi_ = a[..., ax * 16:ax * 16 + 8], a[..., ax * 16 + 8:ax * 16 + 16]
        parts += [-hi_, lo_]
    return jnp.concatenate(parts, axis=-1)


def _pack_w_in(w):
    assert sum(PROJ_WIDTHS[:7]) == SPLIT_COL and sum(PROJ_WIDTHS[7:13]) == MID_W and PROJ_WIDTHS[13] == MG_W
    head = w[..., :SPLIT_COL].astype(BF16)
    kr = head[..., C_MLA_KR:SPLIT_COL]
    return jnp.concatenate([head[..., :C_MLA_KR], _rope_lanes(kr), _rope_lanes(_rot_half(kr))], axis=-1), \
        w[..., SPLIT_COL:].astype(BF16)


def _rope_lanes(a):
    z = lambda n: jnp.zeros(a.shape[:-1] + (n,), a.dtype)
    return jnp.concatenate([z(MLA_NOPE), a, z(LANE - MLA_QK)], axis=-1)


def _pack_w_uq(w):
    w3 = w.reshape(MLA_Q_LORA, MLA_HEADS, MLA_QK)
    nope, rope = w3[..., :MLA_NOPE], w3[..., MLA_NOPE:]
    pad = jnp.zeros(nope.shape[:-1] + (LANE - MLA_QK,), w.dtype)
    flat = lambda a: a.reshape(MLA_Q_LORA, MLA_HEADS * LANE).astype(BF16)
    return flat(jnp.concatenate([nope, rope, pad], axis=-1)), flat(_rope_lanes(_rot_half(rope)))


def _pack_w_ukv(w):
    w3 = w.reshape(MLA_KV_LORA, MLA_HEADS, MLA_NOPE + MLA_VD)
    pad = jnp.zeros((MLA_KV_LORA, MLA_HEADS, LANE - MLA_NOPE), w.dtype)
    flat = lambda a: a.reshape(MLA_KV_LORA, MLA_HEADS * LANE).astype(BF16)
    return flat(jnp.concatenate([w3[..., :MLA_NOPE], pad], -1)), flat(jnp.concatenate([w3[..., MLA_NOPE:], pad], -1))


def _rope_tables(n):
    t = np.arange(n)
    pos = np.stack([t // GRID_W, t % GRID_W]).astype(np.float32)
    axis_dim = MLA_ROPE // 2
    inv = ROPE_BASE ** (-jnp.arange(0, axis_dim, 2, dtype=F32) / axis_dim)
    ang = jnp.asarray(pos)[:, :, None] * inv
    ang = jnp.concatenate([ang, ang], axis=-1)
    ang = jnp.concatenate([ang[0], ang[1]], axis=-1)
    cos_t = jnp.concatenate([jnp.ones((n, MLA_NOPE), F32), jnp.cos(ang), jnp.zeros((n, LANE - MLA_QK), F32)], -1)
    return cos_t, _rope_lanes(jnp.sin(ang))


def _na_bias_table(rpb):
    rows = 32
    c = np.arange(GRID_W)
    win0 = np.clip(c - NA_KW // 2, 0, GRID_W - NA_KW)
    kc = np.arange(GRID_W)
    col_ok = (kc[None, :] >= win0[:, None]) & (kc[None, :] < win0[:, None] + NA_KW)
    rpb = rpb.astype(F32)
    edge = GRID_W - NA_KW
    ext = jnp.concatenate([jnp.broadcast_to(rpb[..., :1], rpb.shape[:-1] + (edge,)), rpb,
                           jnp.broadcast_to(rpb[..., -1:], rpb.shape[:-1] + (edge,))], axis=-1)
    toep = jnp.stack([ext[..., GRID_W - 1 - ci:2 * GRID_W - 1 - ci] for ci in range(GRID_W)], axis=2)
    toep = jnp.where(jnp.asarray(col_ok), toep, NEG_BIG)
    masked = jnp.full((NA_HEADS, GRID_W, GRID_W), NEG_BIG, F32)
    out = []
    for start, r0 in ((0, 0), (4, 8), (20, 28)):
        per_q = []
        for qr in range(4):
            r = r0 + qr
            kr0 = min(max(r - NA_KR // 2, 0), rows - NA_KR)
            tiles = []
            for j in range(12):
                kabs = start + j
                ok = kr0 <= kabs < kr0 + NA_KR
                tiles.append(toep[:, kabs - r + NA_KR - 1] if ok else masked)
            per_q.append(jnp.concatenate(tiles, axis=-1))
        out.append(jnp.concatenate(per_q, axis=1))
    return jnp.stack(out)


def _lower_bounds(p):
    s = jax.nn.softmax(p.astype(F32), axis=0)
    return jnp.cumsum(s, axis=0) - s[0]


def _lb_logs(lb):
    pos = lb > 0
    log_lb = jnp.where(pos, jnp.log(jnp.where(pos, lb, 1.0)), NEG_BIG)
    return log_lb, jnp.log1p(-lb)


def _mod_kernel(c_ref, w_ref, b_ref, o_ref):
    s = _silu(c_ref[...])
    o_ref[0] = _mm(s, w_ref[0], HI) + b_ref[0]


def _modulation(cvec, w_ada, b_ada):
    tn = 1024
    n3 = 3 * D_MODEL
    return pl.pallas_call(
        _mod_kernel,
        out_shape=jax.ShapeDtypeStruct((DEPTH, 8, n3), F32),
        grid=(DEPTH, n3 // tn),
        in_specs=[pl.BlockSpec((8, D_MODEL), lambda l, j: (0, 0)),
                  pl.BlockSpec((1, D_MODEL, tn), lambda l, j: (l, 0, j)),
                  pl.BlockSpec((1, 1, tn), lambda l, j: (l, 0, j))],
        out_specs=pl.BlockSpec((1, 8, tn), lambda l, j: (l, 0, j)),
        compiler_params=_params(("arbitrary", "arbitrary")),
        name="adaln_mod",
    )(cvec, w_ada, b_ada.reshape(DEPTH, 1, n3))


IN_TM, IN_TN = 2048, 1024


def _modulated_norm(x, g, mod_ref):
    return _rms(x, g) * (1.0 + mod_ref[0, 1:2, :]) + mod_ref[0, 0:1, :]


def _prenorm_kernel(x_ref, mod_ref, g_ref, h_ref):
    h_ref[...] = _modulated_norm(x_ref[...], g_ref[...], mod_ref).astype(BF16)


def _prenorm(x, mod3, g_pre, rows_per_mod, mod_base):
    m = x.shape[0]
    t = ROW_BLK
    tiles_per_mod = rows_per_mod // t
    return pl.pallas_call(
        _prenorm_kernel,
        out_shape=jax.ShapeDtypeStruct((m, D_MODEL), BF16),
        grid=(m // t,),
        in_specs=[pl.BlockSpec((t, D_MODEL), lambda i: (i, 0)),
                  pl.BlockSpec((1, 3, D_MODEL), lambda i: (mod_base + i // tiles_per_mod, 0, 0)),
                  _const_spec((1, D_MODEL))],
        out_specs=pl.BlockSpec((t, D_MODEL), lambda i: (i, 0)),
        compiler_params=_params(("arbitrary",)),
        name="prenorm",
    )(x, mod3, g_pre)


def _proj_kernel(h_ref, w_ref, z_ref):
    z_ref[...] = _mm(h_ref[...], w_ref[0]).astype(z_ref.dtype)


def _proj(h, w, layer, tile0, width, out_dtype):
    m = h.shape[0]
    tm, tn = IN_TM, IN_TN
    return pl.pallas_call(
        _proj_kernel,
        out_shape=jax.ShapeDtypeStruct((m, width), out_dtype),
        grid=(m // tm, width // tn),
        in_specs=[pl.BlockSpec((tm, D_MODEL), lambda i, j: (i, 0)),
                  pl.BlockSpec((1, D_MODEL, tn), lambda i, j: (layer, 0, tile0 + j))],
        out_specs=pl.BlockSpec((tm, tn), lambda i, j: (i, j)),
        compiler_params=_params(("arbitrary", "arbitrary")),
        name="in_proj",
    )(h, w)


def _in_proj(h, w_head, w_tail, layer):
    z_head = _proj(h, w_head, layer, 0, HEAD_W, F32)
    z_mid = _proj(h, w_tail, layer, 0, MID_W, BF16)
    z_mg = _proj(h, w_tail, layer, MID_W // IN_TN, MG_W, BF16)
    return z_head, z_mid, z_mg


def _softmax_pv(s_list, v_list):
    m = s_list[0].max(axis=-1, keepdims=True)
    for s in s_list[1:]:
        m = jnp.maximum(m, s.max(axis=-1, keepdims=True))
    acc, den = None, None
    for s, v in zip(s_list, v_list):
        e = jnp.exp(s - m)
        d = e.sum(axis=-1, keepdims=True)
        o = _mm(e.astype(BF16), v)
        acc = o if acc is None else acc + o
        den = d if den is None else den + d
    return acc, den


def _na_head_pairs(q, k_fn, v_fn, scores_fn):
    low = lax.broadcasted_iota(jnp.int32, (1, LANE), 1) < NA_HD
    outs = []
    for p in range(NA_HEADS // 2):
        sl = slice(p * LANE, (p + 1) * LANE)
        q2, k_pair, v_pair = q[:, sl], k_fn(sl), v_fn(sl)
        halves = []
        for half in range(2):
            qh = jnp.where(low if half == 0 else jnp.logical_not(low), q2, 0.0).astype(BF16)
            acc, den = _softmax_pv(scores_fn(qh, k_pair, 2 * p + half), v_pair)
            halves.append(acc / den)
        outs.append(jnp.where(low, halves[0], halves[1]))
    return jnp.concatenate(outs, axis=-1)


def _ones_lane():
    return (lax.broadcasted_iota(jnp.int32, (1, LANE), 1) == MLA_VD).astype(F32)


def _mla_heads(q_fn, kq, vp, o_ref):
    outs = []
    for h in range(MLA_HEADS):
        sl = slice(h * LANE, (h + 1) * LANE)
        s = _nt(q_fn(sl), kq[:, sl])
        e = jnp.exp(s - s.max(axis=-1, keepdims=True))
        acc = _mm(e.astype(BF16), vp[:, sl])
        outs.append(acc[:, :MLA_VD] / acc[:, MLA_VD:MLA_VD + 1])
    o_ref[...] = jnp.concatenate(outs, axis=-1)


def _mla_keys_values(ckvn16, kr_tile, wuk_ref, wuv_ref):
    kq = _mm(ckvn16, wuk_ref[...]) + jnp.concatenate([kr_tile] * MLA_HEADS, axis=-1)
    vp = _mm(ckvn16, wuv_ref[...]) + jnp.concatenate([_ones_lane()] * MLA_HEADS, axis=-1)
    return kq.astype(BF16), vp.astype(BF16)


def _prompt_attn_kernel(qkv_ref, cq_ref, ckv_ref, kr_ref, gq_ref, gkv_ref, wq_ref, wuk_ref, wuv_ref,
                        ona_ref, omla_ref, ckvn_ref, k_ref, v_ref, kro_ref):
    qkv = qkv_ref[...]
    k_ref[...] = qkv[:, NA_W:2 * NA_W]
    v_ref[...] = qkv[:, 2 * NA_W:3 * NA_W]
    kro_ref[...] = kr_ref[:, MLA_NOPE:MLA_QK]
    ckvn = _rms(ckv_ref[...], gkv_ref[...])
    ckvn_ref[...] = ckvn
    scale = NA_HD ** -0.5
    outs = []
    for h in range(NA_HEADS):
        qh = (qkv[:, h * NA_HD:(h + 1) * NA_HD] * scale).astype(BF16)
        kh = qkv[:, NA_W + h * NA_HD:NA_W + (h + 1) * NA_HD].astype(BF16)
        vh = qkv[:, 2 * NA_W + h * NA_HD:2 * NA_W + (h + 1) * NA_HD].astype(BF16)
        acc, den = _softmax_pv([_nt(qh, kh)], [vh])
        outs.append(acc / den)
    ona_ref[...] = jnp.concatenate(outs, axis=-1)

    cqn = _rms(cq_ref[...], gq_ref[...]).astype(BF16)
    q = _mm(cqn, wq_ref[...]).astype(BF16)
    kq, vp = _mla_keys_values(ckvn.astype(BF16), kr_ref[...], wuk_ref, wuv_ref)
    _mla_heads(lambda sl: q[:, sl], kq, vp, omla_ref)


def _prompt_attn(z, n_batch, g_q, g_kv, wq, wuk, wuv):
    m = z.shape[0]
    t = ROW_BLK
    hw = MLA_HEADS * LANE
    return pl.pallas_call(
        _prompt_attn_kernel,
        out_shape=(jax.ShapeDtypeStruct((m, NA_W), F32), jax.ShapeDtypeStruct((m, MLA_W), F32),
                   jax.ShapeDtypeStruct((m, MLA_KV_LORA), F32), jax.ShapeDtypeStruct((m, NA_W), F32),
                   jax.ShapeDtypeStruct((m, NA_W), F32), jax.ShapeDtypeStruct((m, MLA_ROPE), F32)),
        grid=(n_batch,),
        in_specs=[pl.BlockSpec((t, 3 * NA_W), lambda b: (b, C_NA_Q // (3 * NA_W))),
                  pl.BlockSpec((t, MLA_Q_LORA), lambda b: (b, C_MLA_CQ // MLA_Q_LORA)),
                  pl.BlockSpec((t, MLA_KV_LORA), lambda b: (b, C_MLA_CKV // MLA_KV_LORA)),
                  pl.BlockSpec((t, LANE), lambda b: (b, C_MLA_KR // LANE)),
                  _const_spec((1, MLA_Q_LORA)), _const_spec((1, MLA_KV_LORA)),
                  _const_spec((MLA_Q_LORA, hw)), _const_spec((MLA_KV_LORA, hw)), _const_spec((MLA_KV_LORA, hw))],
        out_specs=(pl.BlockSpec((t, NA_W), lambda b: (b, 0)), pl.BlockSpec((t, MLA_W), lambda b: (b, 0)),
                   pl.BlockSpec((t, MLA_KV_LORA), lambda b: (b, 0)), pl.BlockSpec((t, NA_W), lambda b: (b, 0)),
                   pl.BlockSpec((t, NA_W), lambda b: (b, 0)), pl.BlockSpec((t, MLA_ROPE), lambda b: (b, 0))),
        compiler_params=_params(("arbitrary",)),
        name="prompt_attn",
    )(z, z, z, z, g_q, g_kv, wq, wuk, wuv)


def _sample_na_kernel(q_ref, k0_ref, k1_ref, k2_ref, v0_ref, v1_ref, v2_ref, kc_ref, vc_ref, bias_ref, o_ref):
    q = q_ref[...] * NA_HD ** -0.5
    kl = [r[...] for r in (k0_ref, k1_ref, k2_ref)]
    vl = [r[...] for r in (v0_ref, v1_ref, v2_ref)]
    kc, vc = kc_ref[0, 0], vc_ref[0, 0]

    def scores(qh, k_pair, h):
        s_list = [_nt(qh, k_pair[i]) + bias_ref[0, h, :, i * ROW_BLK:(i + 1) * ROW_BLK] for i in range(3)]
        return s_list + [_nt(qh, k_pair[3])]

    o_ref[...] = _na_head_pairs(q, lambda sl: [a[:, sl].astype(BF16) for a in kl + [kc]],
                                lambda sl: [a[:, sl].astype(BF16) for a in vl + [vc]], scores)


def _sample_na(z, cache_k, cache_v, bias_tab, layer, n_batch):
    m = z.shape[0]
    t = ROW_BLK
    nblk = m // n_batch // t
    past = cache_k.shape[2]

    def kv_map(col, i):
        return lambda b, rb: (b * nblk + jnp.clip(rb - 1, 0, nblk - 3) + i, col // NA_W)

    def variant(b, rb):
        return (jnp.where(rb == 0, 0, jnp.where(rb == nblk - 1, 2, 1)), 0, 0, 0)

    cache_spec = pl.BlockSpec((1, 1, past, NA_W), lambda b, rb: (b, layer, 0, 0))
    return pl.pallas_call(
        _sample_na_kernel,
        out_shape=jax.ShapeDtypeStruct((m, NA_W), F32),
        grid=(n_batch, nblk),
        in_specs=[pl.BlockSpec((t, NA_W), lambda b, rb: (b * nblk + rb, C_NA_Q // NA_W))]
                 + [pl.BlockSpec((t, NA_W), kv_map(C_NA_K, i)) for i in range(3)]
                 + [pl.BlockSpec((t, NA_W), kv_map(C_NA_V, i)) for i in range(3)]
                 + [cache_spec, cache_spec,
                    pl.BlockSpec((1, NA_HEADS, t, 3 * t), variant)],
        out_specs=pl.BlockSpec((t, NA_W), lambda b, rb: (b * nblk + rb, 0)),
        compiler_params=_params(("arbitrary", "arbitrary")),
        name="sample_na",
    )(z, z, z, z, z, z, z, cache_k, cache_v, bias_tab)


def _sample_mla_kv_kernel(cckv_ref, ckr_ref, ckv_ref, kr_ref, krp_ref, cos_ref, sin_ref, gkv_ref, wuk_ref, wuv_ref,
                          kq_ref, vp_ref):
    j = pl.program_id(1)

    @pl.when(j == 0)
    def _():
        kq_ref[0], vp_ref[0] = _mla_keys_values(cckv_ref[0, 0].astype(BF16), ckr_ref[0, 0], wuk_ref, wuv_ref)

    @pl.when(j > 0)
    def _():
        ckvn = _rms(ckv_ref[...], gkv_ref[...])
        kr_roped = kr_ref[...] * cos_ref[...] + krp_ref[...] * sin_ref[...]
        kq_ref[0], vp_ref[0] = _mla_keys_values(ckvn.astype(BF16), kr_roped, wuk_ref, wuv_ref)


def _sample_mla_kv(z, cache_ckv, cache_kr_pad, cos_t, sin_t, g_kv, wuk, wuv, layer, n_batch):
    m = z.shape[0]
    n = m // n_batch
    past = cache_ckv.shape[2]
    t = past
    nb = n // t
    hw = MLA_HEADS * LANE

    def zrow(b, j):
        return b * nb + jnp.maximum(j - 1, 0)

    return pl.pallas_call(
        _sample_mla_kv_kernel,
        out_shape=(jax.ShapeDtypeStruct((n_batch, past + n, hw), BF16),
                   jax.ShapeDtypeStruct((n_batch, past + n, hw), BF16)),
        grid=(n_batch, nb + 1),
        in_specs=[pl.BlockSpec((1, 1, past, MLA_KV_LORA), lambda b, j: (b, layer, 0, 0)),
                  pl.BlockSpec((1, 1, past, LANE), lambda b, j: (b, layer, 0, 0)),
                  pl.BlockSpec((t, MLA_KV_LORA), lambda b, j: (zrow(b, j), C_MLA_CKV // MLA_KV_LORA)),
                  pl.BlockSpec((t, LANE), lambda b, j: (zrow(b, j), C_MLA_KR // LANE)),
                  pl.BlockSpec((t, LANE), lambda b, j: (zrow(b, j), C_MLA_KRP // LANE)),
                  pl.BlockSpec((t, LANE), lambda b, j: (jnp.maximum(j - 1, 0), 0)),
                  pl.BlockSpec((t, LANE), lambda b, j: (jnp.maximum(j - 1, 0), 0)),
                  _const_spec((1, MLA_KV_LORA)), _const_spec((MLA_KV_LORA, hw)), _const_spec((MLA_KV_LORA, hw))],
        out_specs=(pl.BlockSpec((1, t, hw), lambda b, j: (b, j, 0)),
                   pl.BlockSpec((1, t, hw), lambda b, j: (b, j, 0))),
        compiler_params=_params(("arbitrary", "arbitrary")),
        name="sample_mla_kv",
    )(cache_ckv, cache_kr_pad, z, z, z, cos_t, sin_t, g_kv, wuk, wuv)


def _sample_mla_attn_kernel(cq_ref, cos_ref, sin_ref, gq_ref, wq_ref, wqp_ref, kq_ref, vp_ref, o_ref):
    cqn = _rms(cq_ref[...], gq_ref[...]).astype(BF16)
    q = _mm(cqn, wq_ref[...])
    q_rot = _mm(cqn, wqp_ref[...])
    cos, sin = cos_ref[...], sin_ref[...]
    _mla_heads(lambda sl: (q[:, sl] * cos + q_rot[:, sl] * sin).astype(BF16), kq_ref[0], vp_ref[0], o_ref)


def _sample_mla_attn(z, cos_t, sin_t, g_q, wq, wqp, kq_all, vp_all, n_batch):
    m = z.shape[0]
    t = ROW_BLK
    nblk = m // n_batch // t
    nk = kq_all.shape[1]
    hw = MLA_HEADS * LANE
    return pl.pallas_call(
        _sample_mla_attn_kernel,
        out_shape=jax.ShapeDtypeStruct((m, MLA_W), F32),
        grid=(n_batch, nblk),
        in_specs=[pl.BlockSpec((t, MLA_Q_LORA), lambda b, i: (b * nblk + i, C_MLA_CQ // MLA_Q_LORA)),
                  pl.BlockSpec((t, LANE), lambda b, i: (i, 0)),
                  pl.BlockSpec((t, LANE), lambda b, i: (i, 0)),
                  _const_spec((1, MLA_Q_LORA)),
                  _const_spec((MLA_Q_LORA, hw)), _const_spec((MLA_Q_LORA, hw)),
                  pl.BlockSpec((1, nk, hw), lambda b, i: (b, 0, 0)),
                  pl.BlockSpec((1, nk, hw), lambda b, i: (b, 0, 0))],
        out_specs=pl.BlockSpec((t, MLA_W), lambda b, i: (b * nblk + i, 0)),
        compiler_params=_params(("arbitrary", "arbitrary")),
        name="sample_mla_attn",
    )(z, cos_t, sin_t, g_q, wq, wqp, kq_all, vp_all)


def _hg_direction(fwd, q_ref, zf_ref, v_ref, loglb_ref, l1m_ref, s_ref, o_ref, qs, ks, bs, vs, os):
    t = ROW_BLK
    nchunk = t // HG_CHUNK
    q = _silu(q_ref[...].astype(F32))
    zf = zf_ref[...].astype(F32)
    v = v_ref[...].astype(F32)
    soft = jnp.log(1.0 + jnp.exp(-jnp.abs(zf)))
    a1 = jnp.broadcast_to(loglb_ref[...], zf.shape)
    a2 = l1m_ref[...] + (jnp.minimum(zf, 0.0) - soft)
    logf = jnp.maximum(a1, a2) + jnp.log(1.0 + jnp.exp(-jnp.abs(a1 - a2)))
    k = jnp.exp(l1m_ref[...] - jnp.maximum(zf, 0.0) - soft)

    def chunk_masks(n):
        ri = lax.broadcasted_iota(jnp.int32, (n, n), 0)
        ci = lax.broadcasted_iota(jnp.int32, (n, n), 1)
        causal = (ri >= ci) if fwd else (ri <= ci)
        same_sub = (ri // HG_SUB) == (ci // HG_SUB)
        same = (ri // HG_CHUNK) == (ci // HG_CHUNK)
        return same_sub & causal, same & jnp.logical_not(same_sub) & causal, same_sub, same

    tri_sub, _, same_sub, same = chunk_masks(t)
    hi = logf.astype(BF16)
    lo = (logf - hi.astype(F32)).astype(BF16)

    def seg_sum(mask):
        m16 = jnp.where(mask, 1.0, 0.0).astype(BF16)
        return _mm(m16, hi) + _mm(m16, lo)

    b_sub = seg_sum(tri_sub)
    nsub = t // HG_SUB
    b3 = b_sub.reshape(nsub, HG_SUB, HG_KW)
    edge = b3[:, HG_SUB - 1:HG_SUB, :] if fwd else b3[:, 0:1, :]
    tot_sub = jnp.broadcast_to(edge, (nsub, HG_SUB, HG_KW)).reshape(t, HG_KW)
    pair = edge.reshape(nchunk, 2, 1, HG_KW)
    tot = jnp.broadcast_to(pair[:, 0:1] + pair[:, 1:2], (nchunk, 2, HG_SUB, HG_KW)).reshape(t, HG_KW)
    row = lax.broadcasted_iota(jnp.int32, (t, 1), 0) % HG_CHUNK
    later = (row >= HG_SUB) if fwd else (row < HG_SUB)
    b = b_sub + jnp.where(later, tot - tot_sub, 0.0)

    qt_sub = (q * jnp.exp(b_sub)).astype(BF16)
    kt_sub = (k * jnp.exp(jnp.minimum(-b_sub, HG_EXP_CLAMP))).astype(BF16)
    kh_sub = (k * jnp.exp(tot_sub - b_sub)).astype(BF16)
    qt16 = (q * jnp.exp(b)).astype(BF16)
    kh16 = (k * jnp.exp(tot - b)).astype(BF16)
    v16 = v.astype(BF16)

    hb = t // 2
    tri_hb, cross_hb, _, _ = chunk_masks(hb)
    low = lax.broadcasted_iota(jnp.int32, (1, LANE), 1) < HG_DK
    zero16 = jnp.zeros((), BF16)
    outs = []
    for p2 in range(HG_HEADS // 2):
        sl = slice(p2 * LANE, (p2 + 1) * LANE)
        parts = []
        for r0 in (0, hb):
            rs = slice(r0, r0 + hb)
            keys = jnp.concatenate([kt_sub[rs, sl], kh_sub[rs, sl]], axis=0)
            halves = []
            for keep in (low, jnp.logical_not(low)):
                p = _nt(jnp.where(keep, qt_sub[rs, sl], zero16), keys)
                a = jnp.where(tri_hb, p[:, :hb], 0.0) + jnp.where(cross_hb, p[:, hb:], 0.0)
                halves.append(_mm(a.astype(BF16), v16[rs, sl]))
            parts.append(jnp.where(low, halves[0], halves[1]))
        outs.append(jnp.concatenate(parts, axis=0))
    o_ref[...] = jnp.concatenate(outs, axis=-1)

    gw = 4 * HG_DK
    bd_r = lax.broadcasted_iota(jnp.int32, (gw, gw), 0) // HG_DV
    bd_c = lax.broadcasted_iota(jnp.int32, (gw, gw), 1) // HG_DK
    bd = bd_r == bd_c
    order = range(nchunk) if fwd else range(nchunk - 1, -1, -1)
    for g in range(2):
        ls = slice(g * gw, (g + 1) * gw)
        rows = {c: slice(c * HG_CHUNK, (c + 1) * HG_CHUNK) for c in order}
        updates = {c: jnp.where(bd, _tn(v16[rows[c], ls], kh16[rows[c], ls]), 0.0) for c in order}
        s = s_ref[g]
        entering = {}
        for c in order:
            entering[c] = s.astype(BF16)
            s = s * jnp.exp(tot[c * HG_CHUNK:c * HG_CHUNK + 1, ls]) + updates[c]
        s_ref[g] = s
        for c in order:
            inter = _nt(qt16[rows[c], ls], entering[c])
            o_ref[rows[c], ls] += inter
            os[rows[c], ls] = inter

    qs[...] = q
    ks[...] = k
    bs[...] = b
    vs[...] = v
    return [jnp.max(-jnp.minimum(tot_sub[c * HG_CHUNK:c * HG_CHUNK + 1, :],
                                 tot_sub[c * HG_CHUNK + HG_SUB:c * HG_CHUNK + HG_SUB + 1, :]))
            for c in range(nchunk)]


def _hg_pairwise(fwd, worst, o_ref, qs, ks, bs, vs, os):
    lane = lax.broadcasted_iota(jnp.int32, (HG_KW, LANE), 0) // HG_DK
    col = lax.broadcasted_iota(jnp.int32, (HG_KW, LANE), 1)
    head_sum = (lane == col).astype(F32)
    srow = lax.broadcasted_iota(jnp.int32, (HG_CHUNK, 1), 0)
    for c, worst_c in enumerate(worst):
        r0 = c * HG_CHUNK

        @pl.when(worst_c > HG_SAFE_DECAY)
        def _():
            kc = ks[r0:r0 + HG_CHUNK, :]
            bc = bs[r0:r0 + HG_CHUNK, :]
            vc = vs[r0:r0 + HG_CHUNK, :]

            def body(i, carry):
                qrow = qs[pl.ds(r0 + i, 1), :]
                brow = bs[pl.ds(r0 + i, 1), :]
                p = qrow * kc * jnp.exp(jnp.minimum(brow - bc, 0.0))
                keep = (srow <= i) if fwd else (srow >= i)
                p = jnp.where(keep, p, 0.0)
                a = _mm(p, head_sum, HI)
                a_full = _nt(a, head_sum, HI)
                o_ref[pl.ds(r0 + i, 1), :] = (jnp.sum(a_full * vc, axis=0, keepdims=True)
                                              + os[pl.ds(r0 + i, 1), :])
                return carry

            lax.fori_loop(0, HG_CHUNK, body, 0)


def _head_block(h):
    g, hh = divmod(h, 4)
    return g, slice(hh * HG_DV, (hh + 1) * HG_DV), slice(hh * HG_DK, (hh + 1) * HG_DK)


def _hgrn_kernel(*refs, has_state):
    (qf_ref, ff_ref, vf_ref, qb_ref, fb_ref, vb_ref, lbf_ref, l1f_ref, lbb_ref, l1b_ref) = refs[:10]
    s0_refs = refs[10:12] if has_state else (None, None)
    rest = refs[12:] if has_state else refs[10:]
    of_ref, ob_ref, sf_out, sb_out, sf, sb = rest[:6]
    scr_f, scr_b = rest[6:11], rest[11:16]
    i = pl.program_id(1)

    @pl.when(i == 0)
    def _():
        for scr, s0_ref in zip((sf, sb), s0_refs):
            scr[...] = jnp.zeros(scr.shape, F32)
            if s0_ref is not None:
                for h in range(HG_HEADS):
                    g, rv, ck = _head_block(h)
                    scr[g, rv, ck] = s0_ref[0, 0, h].T

    worst_f = _hg_direction(True, qf_ref, ff_ref, vf_ref, lbf_ref, l1f_ref, sf, of_ref, *scr_f)
    worst_b = _hg_direction(False, qb_ref, fb_ref, vb_ref, lbb_ref, l1b_ref, sb, ob_ref, *scr_b)

    @pl.when(functools.reduce(jnp.maximum, worst_f + worst_b) > HG_SAFE_DECAY)
    def _():
        _hg_pairwise(True, worst_f, of_ref, *scr_f)
        _hg_pairwise(False, worst_b, ob_ref, *scr_b)

    @pl.when(i == pl.num_programs(1) - 1)
    def _():
        for scr, out in ((sf, sf_out), (sb, sb_out)):
            for h in range(HG_HEADS):
                g, rv, ck = _head_block(h)
                out[0, h] = scr[g, rv, ck].T


def _hgrn(z, n_batch, lb_f, lb_b, s0f=None, s0b=None, layer=0):
    m = z.shape[0]
    t = ROW_BLK
    nblk = m // n_batch // t
    has_state = s0f is not None
    loglb_f, l1m_f = _lb_logs(lb_f)
    loglb_b, l1m_b = _lb_logs(lb_b)
    row = lambda a: a.reshape(1, HG_KW)

    def fmap(col):
        return lambda b, i: (b * nblk + i, col // HG_KW)

    def bmap(col):
        return lambda b, i: (b * nblk + nblk - 1 - i, col // HG_KW)

    blk = lambda imap: pl.BlockSpec((t, HG_KW), imap)
    st_out = pl.BlockSpec((1, HG_HEADS, HG_DK, HG_DV), lambda b, i: (b, 0, 0, 0))
    st_shape = jax.ShapeDtypeStruct((n_batch, HG_HEADS, HG_DK, HG_DV), F32)
    st_in = pl.BlockSpec((1, 1, HG_HEADS, HG_DK, HG_DV), lambda b, i: (b, layer, 0, 0, 0))
    bd_scratch = pltpu.VMEM((2, 4 * HG_DV, 4 * HG_DK), F32)
    return pl.pallas_call(
        functools.partial(_hgrn_kernel, has_state=has_state),
        out_shape=(jax.ShapeDtypeStruct((m, HG_W), F32), jax.ShapeDtypeStruct((m, HG_W), F32), st_shape, st_shape),
        grid=(n_batch, nblk),
        in_specs=[blk(fmap(M_HG_Q)), blk(fmap(M_HG_FF)), blk(fmap(M_HG_I)),
                  blk(bmap(M_HG_Q)), blk(bmap(M_HG_FB)), blk(bmap(M_HG_I)),
                  _const_spec((1, HG_KW)), _const_spec((1, HG_KW)), _const_spec((1, HG_KW)), _const_spec((1, HG_KW))]
                 + ([st_in, st_in] if has_state else []),
        out_specs=(pl.BlockSpec((t, HG_W), lambda b, i: (b * nblk + i, 0)),
                   pl.BlockSpec((t, HG_W), lambda b, i: (b * nblk + nblk - 1 - i, 0)),
                   st_out, st_out),
        scratch_shapes=[bd_scratch, bd_scratch] + [pltpu.VMEM((t, HG_KW), F32)] * 10,
        compiler_params=_params(("arbitrary", "arbitrary")),
        name="hgrn_scan",
    )(z, z, z, z, z, z, row(loglb_f), row(l1m_f), row(loglb_b), row(l1m_b), *((s0f, s0b) if has_state else ()))


def _out_kernel(*refs, has_next):
    (x_ref, mod_ref, gpost_ref, ghg_ref, ona_ref, omla_ref, of_ref, ob_ref,
     gna_ref, gmla_ref, ghgate_ref, mg_ref, wna_ref, wmla_ref, whg_ref, wout_ref) = refs[:16]
    y_ref = refs[-2] if has_next else refs[-1]
    o = of_ref[...] + ob_ref[...]
    hr = lax.broadcasted_iota(jnp.int32, (HG_W, HG_W), 0) // HG_DV
    hc = lax.broadcasted_iota(jnp.int32, (HG_W, HG_W), 1) // HG_DV
    head_mean = jnp.where(hr == hc, 1.0 / HG_DV, 0.0).astype(BF16)
    sq = o * o
    sq_hi = sq.astype(BF16)
    sq_lo = (sq - sq_hi.astype(F32)).astype(BF16)
    ms = _mm(sq_hi, head_mean) + _mm(sq_lo, head_mean)
    o_hg = o * lax.rsqrt(ms + NORM_EPS) * ghg_ref[...]

    def branch(o_b, gate_ref, w_ref):
        return _mm((o_b * _silu(gate_ref[...].astype(F32))).astype(BF16), w_ref[...])

    def merge_gate(i):
        return _sigmoid(mg_ref[:, i * D_MODEL:(i + 1) * D_MODEL].astype(F32))

    merged = (merge_gate(0) * branch(ona_ref[...], gna_ref, wna_ref)
              + merge_gate(1) * branch(omla_ref[...], gmla_ref, wmla_ref)
              + merge_gate(2) * branch(o_hg, ghgate_ref, whg_ref))
    out = _mm(merged.astype(BF16), wout_ref[...])
    y = x_ref[...] + mod_ref[0, 2:3, :] * _rms(out, gpost_ref[...])
    y_ref[...] = y
    if has_next:
        modn_ref, gpren_ref, hn_ref = refs[16], refs[17], refs[-1]
        hn_ref[...] = _modulated_norm(y, gpren_ref[...], modn_ref).astype(BF16)


def _out_proj(x, z_head, z_mid, z_mg, mod3, g_post, g_hg, o_na, o_mla, o_f, o_b, w_na, w_mla, w_hg, w_out,
              rows_per_mod, mod_base, next_mod3=None, next_g_pre=None):
    m = x.shape[0]
    t = ROW_BLK
    tiles_per_mod = rows_per_mod // t
    has_next = next_mod3 is not None
    w512 = lambda: pl.BlockSpec((t, NA_W), lambda i: (i, 0))
    zcol = lambda col, w: pl.BlockSpec((t, w), lambda i: (i, col // w))
    mod_spec = pl.BlockSpec((1, 3, D_MODEL), lambda i: (mod_base + i // tiles_per_mod, 0, 0))
    row_spec = pl.BlockSpec((t, D_MODEL), lambda i: (i, 0))
    one = pl.Buffered(1)
    out = pl.pallas_call(
        functools.partial(_out_kernel, has_next=has_next),
        out_shape=(jax.ShapeDtypeStruct((m, D_MODEL), F32),)
                  + ((jax.ShapeDtypeStruct((m, D_MODEL), BF16),) if has_next else ()),
        grid=(m // t,),
        in_specs=[row_spec, mod_spec,
                  _const_spec((1, D_MODEL)), _const_spec((1, HG_W)),
                  w512(), w512(), w512(), w512(),
                  zcol(C_NA_G, NA_W), zcol(M_MLA_G, MLA_W), zcol(M_HG_G, HG_W),
                  zcol(0, MG_W),
                  pl.BlockSpec((NA_W, D_MODEL), lambda i: (0, 0), pipeline_mode=one),
                  pl.BlockSpec((MLA_W, D_MODEL), lambda i: (0, 0), pipeline_mode=one),
                  pl.BlockSpec((HG_W, D_MODEL), lambda i: (0, 0), pipeline_mode=one),
                  pl.BlockSpec((D_MODEL, D_MODEL), lambda i: (0, 0), pipeline_mode=one)]
                 + ([mod_spec, _const_spec((1, D_MODEL))] if has_next else []),
        out_specs=(row_spec,) + ((row_spec,) if has_next else ()),
        compiler_params=_params(("arbitrary",)),
        name="out_proj",
    )(x, mod3, g_post, g_hg, o_na, o_mla, o_f, o_b, z_head, z_mid, z_mid, z_mg, w_na, w_mla, w_hg, w_out,
      *((next_mod3, next_g_pre) if has_next else ()))
    return out if has_next else (out[0], None)


def kernel(x_prompt, x_sample, cache_na_k, cache_na_v, cache_mla_ckv, cache_mla_krope, state_hgrn_fwd, state_hgrn_bwd, c, c_ctx, w_ada, b_ada, g_pre, g_post, w_in, na_rpb, g_mla_q, w_mla_uq, g_mla_kv, w_mla_ukv, hg_lb_fwd, hg_lb_bwd, g_hg_out, w_br_na, w_br_mla, w_br_hg, w_out):
    bp, sp, _ = x_prompt.shape
    bs, ss, _ = x_sample.shape
    past = cache_na_k.shape[2]

    cvec = jnp.concatenate([c_ctx[None, :], c, jnp.zeros((8 - 1 - bs, D_MODEL), F32)], axis=0)
    mod = _modulation(cvec, w_ada, b_ada).reshape(DEPTH, 8, 3, D_MODEL)

    lb_f_all = _lower_bounds(hg_lb_fwd)
    lb_b_all = _lower_bounds(hg_lb_bwd)
    cos_t, sin_t = _rope_tables(ss)
    cache_k = cache_na_k.reshape(bs, DEPTH, past, NA_W)
    cache_v = cache_na_v.reshape(bs, DEPTH, past, NA_W)
    cache_kr_pad = _rope_lanes(cache_mla_krope)

    yp = x_prompt.reshape(bp * sp, D_MODEL)
    ys = x_sample.reshape(bs * ss, D_MODEL)
    new_k, new_v, new_ckv, new_kr, new_sf, new_sb = [], [], [], [], [], []
    w_head, w_tail = _pack_w_in(w_in)
    hp = _prenorm(yp, mod[0], g_pre[0][None, :], bp * sp, 0)
    hs = _prenorm(ys, mod[0], g_pre[0][None, :], ss, 1)
    for l in range(DEPTH):
        wq, wqp = _pack_w_uq(w_mla_uq[l])
        wuk, wuv = _pack_w_ukv(w_mla_ukv[l])
        w_na, w_mla, w_hg, w_o = (w.astype(BF16) for w in (w_br_na[l], w_br_mla[l], w_br_hg[l], w_out[l]))
        g_q, g_kv = (g_mla_q[l] * MLA_QK ** -0.5)[None, :], g_mla_kv[l][None, :]
        gpost, ghg = g_post[l][None, :], g_hg_out[l][None, :]
        mod3 = mod[l]
        nxt = (mod[l + 1], g_pre[l + 1][None, :]) if l + 1 < DEPTH else (None, None)

        zp_head, zp_mid, zp_mg = _in_proj(hp, w_head, w_tail, l)
        o_na, o_mla, ckvn, k_new, v_new, kr_new = _prompt_attn(zp_head, bp, g_q, g_kv, wq, wuk, wuv)
        o_f, o_b, sf, sb = _hgrn(zp_mid, bp, lb_f_all[l], lb_b_all[l])
        yp, hp = _out_proj(yp, zp_head, zp_mid, zp_mg, mod3, gpost, ghg, o_na, o_mla, o_f, o_b,
                           w_na, w_mla, w_hg, w_o, bp * sp, 0, *nxt)
        new_k.append(k_new.reshape(bp, sp, NA_HEADS, NA_HD))
        new_v.append(v_new.reshape(bp, sp, NA_HEADS, NA_HD))
        new_ckv.append(ckvn.reshape(bp, sp, MLA_KV_LORA))
        new_kr.append(kr_new.reshape(bp, sp, MLA_ROPE))
        new_sf.append(sf)
        new_sb.append(sb)

        zs_head, zs_mid, zs_mg = _in_proj(hs, w_head, w_tail, l)
        o_na = _sample_na(zs_head, cache_k, cache_v, _na_bias_table(na_rpb[l]), l, bs)
        kq_all, vp_all = _sample_mla_kv(zs_head, cache_mla_ckv, cache_kr_pad, cos_t, sin_t, g_kv, wuk, wuv, l, bs)
        o_mla = _sample_mla_attn(zs_head, cos_t, sin_t, g_q, wq, wqp, kq_all, vp_all, bs)
        o_f, o_b, _, _ = _hgrn(zs_mid, bs, lb_f_all[l], lb_b_all[l], state_hgrn_fwd, state_hgrn_bwd, l)
        ys, hs = _out_proj(ys, zs_head, zs_mid, zs_mg, mod3, gpost, ghg, o_na, o_mla, o_f, o_b,
                           w_na, w_mla, w_hg, w_o, ss, 1, *nxt)

    return (yp.reshape(bp, sp, D_MODEL), ys.reshape(bs, ss, D_MODEL),
            jnp.stack(new_k, axis=1), jnp.stack(new_v, axis=1), jnp.stack(new_ckv, axis=1),
            jnp.stack(new_kr, axis=1), jnp.stack(new_sf, axis=1), jnp.stack(new_sb, axis=1))
```

```python
import functools

import numpy as np
import jax
import jax.numpy as jnp
from jax import lax
from jax.experimental import pallas as pl
from jax.experimental.pallas import tpu as pltpu

D_MODEL = 2048
DEPTH = 2
GRID_W = 64
NORM_EPS = 1e-6
NEG_BIG = -1e30
NA_HEADS = 8
NA_HD = 64
NA_W = NA_HEADS * NA_HD
NA_KR = 8
NA_KW = 16
MLA_HEADS = 8
MLA_NOPE = 64
MLA_ROPE = 32
MLA_VD = 64
MLA_QK = MLA_NOPE + MLA_ROPE
MLA_W = MLA_HEADS * MLA_VD
MLA_Q_LORA = 512
MLA_KV_LORA = 256
ROPE_BASE = 10000.0
HG_HEADS = 8
HG_DK = 64
HG_DV = 64
HG_KW = HG_HEADS * HG_DK
HG_W = HG_HEADS * HG_DV
N_BRANCH = 3
PROJ_WIDTHS = (NA_W, NA_W, NA_W, NA_W, MLA_Q_LORA, MLA_KV_LORA, MLA_ROPE, MLA_W,
               HG_KW, HG_KW, HG_KW, HG_W, HG_W, N_BRANCH * D_MODEL)

F32 = jnp.float32
BF16 = jnp.bfloat16
HI = lax.Precision.HIGHEST
LOG2E = 1.4426950408889634

LANE = 128
ROW_BLK = 256
HG_CHUNK = 32
HG_SUB = 16
HG_SAFE_DECAY = 72.0
HG_EXP_CLAMP = 80.0
VMEM_LIMIT = 56 * 1024 * 1024

HEAD_W = 3072
MID_W = 3072
MG_W = N_BRANCH * D_MODEL
SPLIT_COL = 2848
C_NA_Q, C_NA_K, C_NA_V, C_NA_G = 0, 512, 1024, 1536
C_MLA_CQ, C_MLA_CKV, C_MLA_KR, C_MLA_KRP = 2048, 2560, 2816, 2944
M_MLA_G, M_HG_Q, M_HG_FF, M_HG_FB, M_HG_I, M_HG_G = 0, 512, 1024, 1536, 2048, 2560


def _nt(a, b, precision=None):
    return lax.dot_general(a, b, (((1,), (1,)), ((), ())), preferred_element_type=F32, precision=precision)


def _tn(a, b, precision=None):
    return lax.dot_general(a, b, (((0,), (0,)), ((), ())), preferred_element_type=F32, precision=precision)


def _mm(a, b, precision=None):
    return jnp.dot(a, b, preferred_element_type=F32, precision=precision)


def _rms(x, g):
    return x * lax.rsqrt(jnp.mean(x * x, axis=-1, keepdims=True) + NORM_EPS) * g


def _sigmoid(x):
    return 0.5 * jnp.tanh(0.5 * x) + 0.5


def _silu(x):
    return x * _sigmoid(x)


def _params(sem):
    return pltpu.CompilerParams(dimension_semantics=sem, vmem_limit_bytes=VMEM_LIMIT)


def _const_spec(shape):
    nd = len(shape)
    return pl.BlockSpec(shape, lambda *_: (0,) * nd)


def _rot_half(a):
    parts = []
    for ax in range(2):
        lo_, hi_ = a[..., ax * 16:ax * 16 + 8], a[..., ax * 16 + 8:ax * 16 + 16]
        parts += [-hi_, lo_]
    return jnp.concatenate(parts, axis=-1)


def _pack_w_in(w):
    assert sum(PROJ_WIDTHS[:7]) == SPLIT_COL and sum(PROJ_WIDTHS[7:13]) == MID_W and PROJ_WIDTHS[13] == MG_W
    head = w[..., :SPLIT_COL].astype(BF16)
    kr = head[..., C_MLA_KR:SPLIT_COL]
    return jnp.concatenate([head[..., :C_MLA_KR], _rope_lanes(kr), _rope_lanes(_rot_half(kr))], axis=-1), \
        w[..., SPLIT_COL:].astype(BF16)


def _rope_lanes(a):
    z = lambda n: jnp.zeros(a.shape[:-1] + (n,), a.dtype)
    return jnp.concatenate([z(MLA_NOPE), a, z(LANE - MLA_QK)], axis=-1)


def _pack_w_uq(w):
    w3 = w.reshape(MLA_Q_LORA, MLA_HEADS, MLA_QK)
    nope, rope = w3[..., :MLA_NOPE], w3[..., MLA_NOPE:]
    pad = jnp.zeros(nope.shape[:-1] + (LANE - MLA_QK,), w.dtype)
    flat = lambda a: a.reshape(MLA_Q_LORA, MLA_HEADS * LANE).astype(BF16)
    return flat(jnp.concatenate([nope, rope, pad], axis=-1)), flat(_rope_lanes(_rot_half(rope)))


def _pack_w_ukv(w):
    w3 = w.reshape(MLA_KV_LORA, MLA_HEADS, MLA_NOPE + MLA_VD)
    pad = jnp.zeros((MLA_KV_LORA, MLA_HEADS, LANE - MLA_NOPE), w.dtype)
    flat = lambda a: a.reshape(MLA_KV_LORA, MLA_HEADS * LANE).astype(BF16)
    return flat(jnp.concatenate([w3[..., :MLA_NOPE], pad], -1)), flat(jnp.concatenate([w3[..., MLA_NOPE:], pad], -1))


def _rope_tables(n):
    t = np.arange(n)
    pos = np.stack([t // GRID_W, t % GRID_W]).astype(np.float32)
    axis_dim = MLA_ROPE // 2
    inv = ROPE_BASE ** (-jnp.arange(0, axis_dim, 2, dtype=F32) / axis_dim)
    ang = jnp.asarray(pos)[:, :, None] * inv
    ang = jnp.concatenate([ang, ang], axis=-1)
    ang = jnp.concatenate([ang[0], ang[1]], axis=-1)
    cos_t = jnp.concatenate([jnp.ones((n, MLA_NOPE), F32), jnp.cos(ang), jnp.zeros((n, LANE - MLA_QK), F32)], -1)
    return cos_t, _rope_lanes(jnp.sin(ang))


def _na_bias_table(rpb):
    rows = 32
    c = np.arange(GRID_W)
    win0 = np.clip(c - NA_KW // 2, 0, GRID_W - NA_KW)
    kc = np.arange(GRID_W)
    col_ok = (kc[None, :] >= win0[:, None]) & (kc[None, :] < win0[:, None] + NA_KW)
    rpb = rpb.astype(F32)
    edge = GRID_W - NA_KW
    ext = jnp.concatenate([jnp.broadcast_to(rpb[..., :1], rpb.shape[:-1] + (edge,)), rpb,
                           jnp.broadcast_to(rpb[..., -1:], rpb.shape[:-1] + (edge,))], axis=-1)
    toep = jnp.stack([ext[..., GRID_W - 1 - ci:2 * GRID_W - 1 - ci] for ci in range(GRID_W)], axis=2)
    toep = jnp.where(jnp.asarray(col_ok), toep * LOG2E, NEG_BIG)
    masked = jnp.full((NA_HEADS, GRID_W, GRID_W), NEG_BIG, F32)
    out = []
    for start, r0 in ((0, 0), (4, 8), (20, 28)):
        per_q = []
        for qr in range(4):
            r = r0 + qr
            kr0 = min(max(r - NA_KR // 2, 0), rows - NA_KR)
            tiles = []
            for j in range(12):
                kabs = start + j
                ok = kr0 <= kabs < kr0 + NA_KR
                tiles.append(toep[:, kabs - r + NA_KR - 1] if ok else masked)
            per_q.append(jnp.concatenate(tiles, axis=-1))
        out.append(jnp.concatenate(per_q, axis=1))
    return jnp.stack(out)


def _lower_bounds(p):
    s = jax.nn.softmax(p.astype(F32), axis=0)
    return jnp.cumsum(s, axis=0) - s[0]


def _lb_logs(lb):
    pos = lb > 0
    log_lb = jnp.where(pos, jnp.log(jnp.where(pos, lb, 1.0)), NEG_BIG)
    return log_lb, jnp.log1p(-lb)


def _mod_kernel(c_ref, w_ref, b_ref, o_ref):
    s = _silu(c_ref[...])
    o_ref[0] = _mm(s, w_ref[0], HI) + b_ref[0]


def _modulation(cvec, w_ada, b_ada):
    tn = 1024
    n3 = 3 * D_MODEL
    return pl.pallas_call(
        _mod_kernel,
        out_shape=jax.ShapeDtypeStruct((DEPTH, 8, n3), F32),
        grid=(DEPTH, n3 // tn),
        in_specs=[pl.BlockSpec((8, D_MODEL), lambda l, j: (0, 0)),
                  pl.BlockSpec((1, D_MODEL, tn), lambda l, j: (l, 0, j)),
                  pl.BlockSpec((1, 1, tn), lambda l, j: (l, 0, j))],
        out_specs=pl.BlockSpec((1, 8, tn), lambda l, j: (l, 0, j)),
        compiler_params=_params(("arbitrary", "arbitrary")),
        name="adaln_mod",
    )(cvec, w_ada, b_ada.reshape(DEPTH, 1, n3))


IN_TM, IN_TN = 2048, 1024


def _modulated_norm(x, g, mod_ref):
    return _rms(x, g) * (1.0 + mod_ref[0, 1:2, :]) + mod_ref[0, 0:1, :]


def _prenorm_kernel(x_ref, mod_ref, g_ref, h_ref):
    h_ref[...] = _modulated_norm(x_ref[...], g_ref[...], mod_ref).astype(BF16)


def _prenorm(x, mod3, g_pre, rows_per_mod, mod_base):
    m = x.shape[0]
    t = ROW_BLK
    tiles_per_mod = rows_per_mod // t
    return pl.pallas_call(
        _prenorm_kernel,
        out_shape=jax.ShapeDtypeStruct((m, D_MODEL), BF16),
        grid=(m // t,),
        in_specs=[pl.BlockSpec((t, D_MODEL), lambda i: (i, 0)),
                  pl.BlockSpec((1, 3, D_MODEL), lambda i: (mod_base + i // tiles_per_mod, 0, 0)),
                  _const_spec((1, D_MODEL))],
        out_specs=pl.BlockSpec((t, D_MODEL), lambda i: (i, 0)),
        compiler_params=_params(("arbitrary",)),
        name="prenorm",
    )(x, mod3, g_pre)


def _proj_kernel(h_ref, w_ref, z_ref):
    z_ref[...] = _mm(h_ref[...], w_ref[0]).astype(z_ref.dtype)


def _proj(h, w, layer, tile0, width, out_dtype):
    m = h.shape[0]
    tm, tn = IN_TM, IN_TN
    return pl.pallas_call(
        _proj_kernel,
        out_shape=jax.ShapeDtypeStruct((m, width), out_dtype),
        grid=(m // tm, width // tn),
        in_specs=[pl.BlockSpec((tm, D_MODEL), lambda i, j: (i, 0)),
                  pl.BlockSpec((1, D_MODEL, tn), lambda i, j: (layer, 0, tile0 + j))],
        out_specs=pl.BlockSpec((tm, tn), lambda i, j: (i, j)),
        compiler_params=_params(("arbitrary", "arbitrary")),
        name="in_proj",
    )(h, w)


def _in_proj(h, w_head, w_tail, layer):
    z_head = _proj(h, w_head, layer, 0, HEAD_W, F32)
    z_mid = _proj(h, w_tail, layer, 0, MID_W, BF16)
    z_mg = _proj(h, w_tail, layer, MID_W // IN_TN, MG_W, BF16)
    return z_head, z_mid, z_mg


def _softmax_pv(s_list, v_list):
    m = s_list[0].max(axis=-1, keepdims=True)
    for s in s_list[1:]:
        m = jnp.maximum(m, s.max(axis=-1, keepdims=True))
    acc, den = None, None
    for s, v in zip(s_list, v_list):
        e = jnp.exp2(s - m)
        d = e.sum(axis=-1, keepdims=True)
        o = _mm(e.astype(BF16), v)
        acc = o if acc is None else acc + o
        den = d if den is None else den + d
    return acc, den


def _na_head_pairs(q, k_fn, v_fn, scores_fn):
    low = lax.broadcasted_iota(jnp.int32, (1, LANE), 1) < NA_HD
    outs = []
    for p in range(NA_HEADS // 2):
        sl = slice(p * LANE, (p + 1) * LANE)
        q2, k_pair, v_pair = q[:, sl], k_fn(sl), v_fn(sl)
        halves = []
        for half in range(2):
            qh = jnp.where(low if half == 0 else jnp.logical_not(low), q2, 0.0).astype(BF16)
            acc, den = _softmax_pv(scores_fn(qh, k_pair, 2 * p + half), v_pair)
            halves.append(acc / den)
        outs.append(jnp.where(low, halves[0], halves[1]))
    return jnp.concatenate(outs, axis=-1)


def _ones_lane():
    return (lax.broadcasted_iota(jnp.int32, (1, LANE), 1) == MLA_VD).astype(F32)


def _mla_heads(q_fn, kq, vp, o_ref):
    outs = []
    for h in range(MLA_HEADS):
        sl = slice(h * LANE, (h + 1) * LANE)
        s = _nt(q_fn(sl), kq[:, sl])
        e = jnp.exp2(s - s.max(axis=-1, keepdims=True))
        acc = _mm(e.astype(BF16), vp[:, sl])
        outs.append(acc[:, :MLA_VD] / acc[:, MLA_VD:MLA_VD + 1])
    o_ref[...] = jnp.concatenate(outs, axis=-1)


def _mla_keys_values(ckvn16, kr_tile, wuk_ref, wuv_ref):
    kq = _mm(ckvn16, wuk_ref[...]) + jnp.concatenate([kr_tile] * MLA_HEADS, axis=-1)
    vp = _mm(ckvn16, wuv_ref[...]) + jnp.concatenate([_ones_lane()] * MLA_HEADS, axis=-1)
    return kq.astype(BF16), vp.astype(BF16)


def _prompt_attn_kernel(qkv_ref, cq_ref, ckv_ref, kr_ref, gq_ref, gkv_ref, wq_ref, wuk_ref, wuv_ref,
                        ona_ref, omla_ref, ckvn_ref, k_ref, v_ref, kro_ref):
    qkv = qkv_ref[...]
    k_ref[...] = qkv[:, NA_W:2 * NA_W]
    v_ref[...] = qkv[:, 2 * NA_W:3 * NA_W]
    kro_ref[...] = kr_ref[:, MLA_NOPE:MLA_QK]
    ckvn = _rms(ckv_ref[...], gkv_ref[...])
    ckvn_ref[...] = ckvn
    scale = NA_HD ** -0.5 * LOG2E
    outs = []
    for h in range(NA_HEADS):
        qh = (qkv[:, h * NA_HD:(h + 1) * NA_HD] * scale).astype(BF16)
        kh = qkv[:, NA_W + h * NA_HD:NA_W + (h + 1) * NA_HD].astype(BF16)
        vh = qkv[:, 2 * NA_W + h * NA_HD:2 * NA_W + (h + 1) * NA_HD].astype(BF16)
        acc, den = _softmax_pv([_nt(qh, kh)], [vh])
        outs.append(acc / den)
    ona_ref[...] = jnp.concatenate(outs, axis=-1)

    cqn = _rms(cq_ref[...], gq_ref[...]).astype(BF16)
    q = _mm(cqn, wq_ref[...]).astype(BF16)
    kq, vp = _mla_keys_values(ckvn.astype(BF16), kr_ref[...], wuk_ref, wuv_ref)
    _mla_heads(lambda sl: q[:, sl], kq, vp, omla_ref)


def _prompt_attn(z, n_batch, g_q, g_kv, wq, wuk, wuv):
    m = z.shape[0]
    t = ROW_BLK
    hw = MLA_HEADS * LANE
    return pl.pallas_call(
        _prompt_attn_kernel,
        out_shape=(jax.ShapeDtypeStruct((m, NA_W), F32), jax.ShapeDtypeStruct((m, MLA_W), F32),
                   jax.ShapeDtypeStruct((m, MLA_KV_LORA), F32), jax.ShapeDtypeStruct((m, NA_W), F32),
                   jax.ShapeDtypeStruct((m, NA_W), F32), jax.ShapeDtypeStruct((m, MLA_ROPE), F32)),
        grid=(n_batch,),
        in_specs=[pl.BlockSpec((t, 3 * NA_W), lambda b: (b, C_NA_Q // (3 * NA_W))),
                  pl.BlockSpec((t, MLA_Q_LORA), lambda b: (b, C_MLA_CQ // MLA_Q_LORA)),
                  pl.BlockSpec((t, MLA_KV_LORA), lambda b: (b, C_MLA_CKV // MLA_KV_LORA)),
                  pl.BlockSpec((t, LANE), lambda b: (b, C_MLA_KR // LANE)),
                  _const_spec((1, MLA_Q_LORA)), _const_spec((1, MLA_KV_LORA)),
                  _const_spec((MLA_Q_LORA, hw)), _const_spec((MLA_KV_LORA, hw)), _const_spec((MLA_KV_LORA, hw))],
        out_specs=(pl.BlockSpec((t, NA_W), lambda b: (b, 0)), pl.BlockSpec((t, MLA_W), lambda b: (b, 0)),
                   pl.BlockSpec((t, MLA_KV_LORA), lambda b: (b, 0)), pl.BlockSpec((t, NA_W), lambda b: (b, 0)),
                   pl.BlockSpec((t, NA_W), lambda b: (b, 0)), pl.BlockSpec((t, MLA_ROPE), lambda b: (b, 0))),
        compiler_params=_params(("arbitrary",)),
        name="prompt_attn",
    )(z, z, z, z, g_q, g_kv, wq, wuk, wuv)


def _sample_na_kernel(q_ref, k0_ref, k1_ref, k2_ref, v0_ref, v1_ref, v2_ref, kc_ref, vc_ref, bias_ref, o_ref):
    q = q_ref[...] * (NA_HD ** -0.5 * LOG2E)
    kl = [r[...] for r in (k0_ref, k1_ref, k2_ref)]
    vl = [r[...] for r in (v0_ref, v1_ref, v2_ref)]
    kc, vc = kc_ref[0, 0], vc_ref[0, 0]

    def scores(qh, k_pair, h):
        s_list = [_nt(qh, k_pair[i]) + bias_ref[0, h, :, i * ROW_BLK:(i + 1) * ROW_BLK] for i in range(3)]
        return s_list + [_nt(qh, k_pair[3])]

    o_ref[...] = _na_head_pairs(q, lambda sl: [a[:, sl].astype(BF16) for a in kl + [kc]],
                                lambda sl: [a[:, sl].astype(BF16) for a in vl + [vc]], scores)


def _sample_na(z, cache_k, cache_v, bias_tab, layer, n_batch):
    m = z.shape[0]
    t = ROW_BLK
    nblk = m // n_batch // t
    past = cache_k.shape[2]

    def kv_map(col, i):
        return lambda b, rb: (b * nblk + jnp.clip(rb - 1, 0, nblk - 3) + i, col // NA_W)

    def variant(b, rb):
        return (jnp.where(rb == 0, 0, jnp.where(rb == nblk - 1, 2, 1)), 0, 0, 0)

    cache_spec = pl.BlockSpec((1, 1, past, NA_W), lambda b, rb: (b, layer, 0, 0))
    return pl.pallas_call(
        _sample_na_kernel,
        out_shape=jax.ShapeDtypeStruct((m, NA_W), F32),
        grid=(n_batch, nblk),
        in_specs=[pl.BlockSpec((t, NA_W), lambda b, rb: (b * nblk + rb, C_NA_Q // NA_W))]
                 + [pl.BlockSpec((t, NA_W), kv_map(C_NA_K, i)) for i in range(3)]
                 + [pl.BlockSpec((t, NA_W), kv_map(C_NA_V, i)) for i in range(3)]
                 + [cache_spec, cache_spec,
                    pl.BlockSpec((1, NA_HEADS, t, 3 * t), variant)],
        out_specs=pl.BlockSpec((t, NA_W), lambda b, rb: (b * nblk + rb, 0)),
        compiler_params=_params(("arbitrary", "arbitrary")),
        name="sample_na",
    )(z, z, z, z, z, z, z, cache_k, cache_v, bias_tab)


def _sample_mla_kv_kernel(cckv_ref, ckr_ref, ckv_ref, kr_ref, krp_ref, cos_ref, sin_ref, gkv_ref, wuk_ref, wuv_ref,
                          kq_ref, vp_ref):
    j = pl.program_id(1)

    @pl.when(j == 0)
    def _():
        kq_ref[0], vp_ref[0] = _mla_keys_values(cckv_ref[0, 0].astype(BF16), ckr_ref[0, 0], wuk_ref, wuv_ref)

    @pl.when(j > 0)
    def _():
        ckvn = _rms(ckv_ref[...], gkv_ref[...])
        kr_roped = kr_ref[...] * cos_ref[...] + krp_ref[...] * sin_ref[...]
        kq_ref[0], vp_ref[0] = _mla_keys_values(ckvn.astype(BF16), kr_roped, wuk_ref, wuv_ref)


def _sample_mla_kv(z, cache_ckv, cache_kr_pad, cos_t, sin_t, g_kv, wuk, wuv, layer, n_batch):
    m = z.shape[0]
    n = m // n_batch
    past = cache_ckv.shape[2]
    t = past
    nb = n // t
    hw = MLA_HEADS * LANE

    def zrow(b, j):
        return b * nb + jnp.maximum(j - 1, 0)

    return pl.pallas_call(
        _sample_mla_kv_kernel,
        out_shape=(jax.ShapeDtypeStruct((n_batch, past + n, hw), BF16),
                   jax.ShapeDtypeStruct((n_batch, past + n, hw), BF16)),
        grid=(n_batch, nb + 1),
        in_specs=[pl.BlockSpec((1, 1, past, MLA_KV_LORA), lambda b, j: (b, layer, 0, 0)),
                  pl.BlockSpec((1, 1, past, LANE), lambda b, j: (b, layer, 0, 0)),
                  pl.BlockSpec((t, MLA_KV_LORA), lambda b, j: (zrow(b, j), C_MLA_CKV // MLA_KV_LORA)),
                  pl.BlockSpec((t, LANE), lambda b, j: (zrow(b, j), C_MLA_KR // LANE)),
                  pl.BlockSpec((t, LANE), lambda b, j: (zrow(b, j), C_MLA_KRP // LANE)),
                  pl.BlockSpec((t, LANE), lambda b, j: (jnp.maximum(j - 1, 0), 0)),
                  pl.BlockSpec((t, LANE), lambda b, j: (jnp.maximum(j - 1, 0), 0)),
                  _const_spec((1, MLA_KV_LORA)), _const_spec((MLA_KV_LORA, hw)), _const_spec((MLA_KV_LORA, hw))],
        out_specs=(pl.BlockSpec((1, t, hw), lambda b, j: (b, j, 0)),
                   pl.BlockSpec((1, t, hw), lambda b, j: (b, j, 0))),
        compiler_params=_params(("arbitrary", "arbitrary")),
        name="sample_mla_kv",
    )(cache_ckv, cache_kr_pad, z, z, z, cos_t, sin_t, g_kv, wuk, wuv)


def _sample_mla_attn_kernel(cq_ref, cos_ref, sin_ref, gq_ref, wq_ref, wqp_ref, kq_ref, vp_ref, o_ref):
    cqn = _rms(cq_ref[...], gq_ref[...]).astype(BF16)
    q = _mm(cqn, wq_ref[...])
    q_rot = _mm(cqn, wqp_ref[...])
    cos, sin = cos_ref[...], sin_ref[...]
    _mla_heads(lambda sl: (q[:, sl] * cos + q_rot[:, sl] * sin).astype(BF16), kq_ref[0], vp_ref[0], o_ref)


def _sample_mla_attn(z, cos_t, sin_t, g_q, wq, wqp, kq_all, vp_all, n_batch):
    m = z.shape[0]
    t = ROW_BLK
    nblk = m // n_batch // t
    nk = kq_all.shape[1]
    hw = MLA_HEADS * LANE
    return pl.pallas_call(
        _sample_mla_attn_kernel,
        out_shape=jax.ShapeDtypeStruct((m, MLA_W), F32),
        grid=(n_batch, nblk),
        in_specs=[pl.BlockSpec((t, MLA_Q_LORA), lambda b, i: (b * nblk + i, C_MLA_CQ // MLA_Q_LORA)),
                  pl.BlockSpec((t, LANE), lambda b, i: (i, 0)),
                  pl.BlockSpec((t, LANE), lambda b, i: (i, 0)),
                  _const_spec((1, MLA_Q_LORA)),
                  _const_spec((MLA_Q_LORA, hw)), _const_spec((MLA_Q_LORA, hw)),
                  pl.BlockSpec((1, nk, hw), lambda b, i: (b, 0, 0)),
                  pl.BlockSpec((1, nk, hw), lambda b, i: (b, 0, 0))],
        out_specs=pl.BlockSpec((t, MLA_W), lambda b, i: (b * nblk + i, 0)),
        compiler_params=_params(("arbitrary", "arbitrary")),
        name="sample_mla_attn",
    )(z, cos_t, sin_t, g_q, wq, wqp, kq_all, vp_all)


def _hg_direction(fwd, q_ref, zf_ref, v_ref, loglb_ref, l1m_ref, s_ref, o_ref, qs, ks, bs, vs, os):
    t = ROW_BLK
    nchunk = t // HG_CHUNK
    q = _silu(q_ref[...].astype(F32))
    zf = zf_ref[...].astype(F32)
    v = v_ref[...].astype(F32)
    soft = jnp.log(1.0 + jnp.exp(-jnp.abs(zf)))
    a1 = jnp.broadcast_to(loglb_ref[...], zf.shape)
    a2 = l1m_ref[...] + (jnp.minimum(zf, 0.0) - soft)
    logf = jnp.maximum(a1, a2) + jnp.log(1.0 + jnp.exp(-jnp.abs(a1 - a2)))
    k = jnp.exp(l1m_ref[...] - jnp.maximum(zf, 0.0) - soft)

    def chunk_masks(n):
        ri = lax.broadcasted_iota(jnp.int32, (n, n), 0)
        ci = lax.broadcasted_iota(jnp.int32, (n, n), 1)
        causal = (ri >= ci) if fwd else (ri <= ci)
        same_sub = (ri // HG_SUB) == (ci // HG_SUB)
        same = (ri // HG_CHUNK) == (ci // HG_CHUNK)
        return same_sub & causal, same & jnp.logical_not(same_sub) & causal, same_sub, same

    tri_sub, _, same_sub, same = chunk_masks(t)
    hi = logf.astype(BF16)
    lo = (logf - hi.astype(F32)).astype(BF16)

    def seg_sum(mask):
        m16 = jnp.where(mask, 1.0, 0.0).astype(BF16)
        return _mm(m16, hi) + _mm(m16, lo)

    b_sub = seg_sum(tri_sub)
    nsub = t // HG_SUB
    b3 = b_sub.reshape(nsub, HG_SUB, HG_KW)
    edge = b3[:, HG_SUB - 1:HG_SUB, :] if fwd else b3[:, 0:1, :]
    tot_sub = jnp.broadcast_to(edge, (nsub, HG_SUB, HG_KW)).reshape(t, HG_KW)
    pair = edge.reshape(nchunk, 2, 1, HG_KW)
    tot = jnp.broadcast_to(pair[:, 0:1] + pair[:, 1:2], (nchunk, 2, HG_SUB, HG_KW)).reshape(t, HG_KW)
    row = lax.broadcasted_iota(jnp.int32, (t, 1), 0) % HG_CHUNK
    later = (row >= HG_SUB) if fwd else (row < HG_SUB)
    b = b_sub + jnp.where(later, tot - tot_sub, 0.0)

    qt_sub = (q * jnp.exp(b_sub)).astype(BF16)
    kt_sub = (k * jnp.exp(jnp.minimum(-b_sub, HG_EXP_CLAMP))).astype(BF16)
    kh_sub = (k * jnp.exp(tot_sub - b_sub)).astype(BF16)
    qt16 = (q * jnp.exp(b)).astype(BF16)
    kh16 = (k * jnp.exp(tot - b)).astype(BF16)
    v16 = v.astype(BF16)

    hb = t // 2
    tri_hb, cross_hb, _, _ = chunk_masks(hb)
    low = lax.broadcasted_iota(jnp.int32, (1, LANE), 1) < HG_DK
    zero16 = jnp.zeros((), BF16)
    outs = []
    for p2 in range(HG_HEADS // 2):
        sl = slice(p2 * LANE, (p2 + 1) * LANE)
        parts = []
        for r0 in (0, hb):
            rs = slice(r0, r0 + hb)
            keys = jnp.concatenate([kt_sub[rs, sl], kh_sub[rs, sl]], axis=0)
            halves = []
            for keep in (low, jnp.logical_not(low)):
                p = _nt(jnp.where(keep, qt_sub[rs, sl], zero16), keys)
                a = jnp.where(tri_hb, p[:, :hb], 0.0) + jnp.where(cross_hb, p[:, hb:], 0.0)
                halves.append(_mm(a.astype(BF16), v16[rs, sl]))
            parts.append(jnp.where(low, halves[0], halves[1]))
        outs.append(jnp.concatenate(parts, axis=0))
    o_ref[...] = jnp.concatenate(outs, axis=-1)

    gw = 4 * HG_DK
    bd_r = lax.broadcasted_iota(jnp.int32, (gw, gw), 0) // HG_DV
    bd_c = lax.broadcasted_iota(jnp.int32, (gw, gw), 1) // HG_DK
    bd = bd_r == bd_c
    order = range(nchunk) if fwd else range(nchunk - 1, -1, -1)
    for g in range(2):
        ls = slice(g * gw, (g + 1) * gw)
        rows = {c: slice(c * HG_CHUNK, (c + 1) * HG_CHUNK) for c in order}
        updates = {c: jnp.where(bd, _tn(v16[rows[c], ls], kh16[rows[c], ls]), 0.0) for c in order}
        s = s_ref[g]
        entering = {}
        for c in order:
            entering[c] = s.astype(BF16)
            s = s * jnp.exp(tot[c * HG_CHUNK:c * HG_CHUNK + 1, ls]) + updates[c]
        s_ref[g] = s
        for c in order:
            inter = _nt(qt16[rows[c], ls], entering[c])
            o_ref[rows[c], ls] += inter
            os[rows[c], ls] = inter

    qs[...] = q
    ks[...] = k
    bs[...] = b
    vs[...] = v
    return [jnp.max(-jnp.minimum(tot_sub[c * HG_CHUNK:c * HG_CHUNK + 1, :],
                                 tot_sub[c * HG_CHUNK + HG_SUB:c * HG_CHUNK + HG_SUB + 1, :]))
            for c in range(nchunk)]


def _hg_pairwise(fwd, worst, o_ref, qs, ks, bs, vs, os):
    lane = lax.broadcasted_iota(jnp.int32, (HG_KW, LANE), 0) // HG_DK
    col = lax.broadcasted_iota(jnp.int32, (HG_KW, LANE), 1)
    head_sum = (lane == col).astype(F32)
    srow = lax.broadcasted_iota(jnp.int32, (HG_CHUNK, 1), 0)
    for c, worst_c in enumerate(worst):
        r0 = c * HG_CHUNK

        @pl.when(worst_c > HG_SAFE_DECAY)
        def _():
            kc = ks[r0:r0 + HG_CHUNK, :]
            bc = bs[r0:r0 + HG_CHUNK, :]
            vc = vs[r0:r0 + HG_CHUNK, :]

            def body(i, carry):
                qrow = qs[pl.ds(r0 + i, 1), :]
                brow = bs[pl.ds(r0 + i, 1), :]
                p = qrow * kc * jnp.exp(jnp.minimum(brow - bc, 0.0))
                keep = (srow <= i) if fwd else (srow >= i)
                p = jnp.where(keep, p, 0.0)
                a = _mm(p, head_sum, HI)
                a_full = _nt(a, head_sum, HI)
                o_ref[pl.ds(r0 + i, 1), :] = (jnp.sum(a_full * vc, axis=0, keepdims=True)
                                              + os[pl.ds(r0 + i, 1), :])
                return carry

            lax.fori_loop(0, HG_CHUNK, body, 0)


def _head_block(h):
    g, hh = divmod(h, 4)
    return g, slice(hh * HG_DV, (hh + 1) * HG_DV), slice(hh * HG_DK, (hh + 1) * HG_DK)


def _hgrn_kernel(*refs, has_state):
    (qf_ref, ff_ref, vf_ref, qb_ref, fb_ref, vb_ref, lbf_ref, l1f_ref, lbb_ref, l1b_ref) = refs[:10]
    s0_refs = refs[10:12] if has_state else (None, None)
    rest = refs[12:] if has_state else refs[10:]
    of_ref, ob_ref, sf_out, sb_out, sf, sb = rest[:6]
    scr_f, scr_b = rest[6:11], rest[11:16]
    i = pl.program_id(1)

    @pl.when(i == 0)
    def _():
        for scr, s0_ref in zip((sf, sb), s0_refs):
            scr[...] = jnp.zeros(scr.shape, F32)
            if s0_ref is not None:
                for h in range(HG_HEADS):
                    g, rv, ck = _head_block(h)
                    scr[g, rv, ck] = s0_ref[0, 0, h].T

    worst_f = _hg_direction(True, qf_ref, ff_ref, vf_ref, lbf_ref, l1f_ref, sf, of_ref, *scr_f)
    worst_b = _hg_direction(False, qb_ref, fb_ref, vb_ref, lbb_ref, l1b_ref, sb, ob_ref, *scr_b)

    @pl.when(functools.reduce(jnp.maximum, worst_f + worst_b) > HG_SAFE_DECAY)
    def _():
        _hg_pairwise(True, worst_f, of_ref, *scr_f)
        _hg_pairwise(False, worst_b, ob_ref, *scr_b)

    @pl.when(i == pl.num_programs(1) - 1)
    def _():
        for scr, out in ((sf, sf_out), (sb, sb_out)):
            for h in range(HG_HEADS):
                g, rv, ck = _head_block(h)
                out[0, h] = scr[g, rv, ck].T


def _hgrn(z, n_batch, lb_f, lb_b, s0f=None, s0b=None, layer=0):
    m = z.shape[0]
    t = ROW_BLK
    nblk = m // n_batch // t
    has_state = s0f is not None
    loglb_f, l1m_f = _lb_logs(lb_f)
    loglb_b, l1m_b = _lb_logs(lb_b)
    row = lambda a: a.reshape(1, HG_KW)

    def fmap(col):
        return lambda b, i: (b * nblk + i, col // HG_KW)

    def bmap(col):
        return lambda b, i: (b * nblk + nblk - 1 - i, col // HG_KW)

    blk = lambda imap: pl.BlockSpec((t, HG_KW), imap)
    st_out = pl.BlockSpec((1, HG_HEADS, HG_DK, HG_DV), lambda b, i: (b, 0, 0, 0))
    st_shape = jax.ShapeDtypeStruct((n_batch, HG_HEADS, HG_DK, HG_DV), F32)
    st_in = pl.BlockSpec((1, 1, HG_HEADS, HG_DK, HG_DV), lambda b, i: (b, layer, 0, 0, 0))
    bd_scratch = pltpu.VMEM((2, 4 * HG_DV, 4 * HG_DK), F32)
    return pl.pallas_call(
        functools.partial(_hgrn_kernel, has_state=has_state),
        out_shape=(jax.ShapeDtypeStruct((m, HG_W), F32), jax.ShapeDtypeStruct((m, HG_W), F32), st_shape, st_shape),
        grid=(n_batch, nblk),
        in_specs=[blk(fmap(M_HG_Q)), blk(fmap(M_HG_FF)), blk(fmap(M_HG_I)),
                  blk(bmap(M_HG_Q)), blk(bmap(M_HG_FB)), blk(bmap(M_HG_I)),
                  _const_spec((1, HG_KW)), _const_spec((1, HG_KW)), _const_spec((1, HG_KW)), _const_spec((1, HG_KW))]
                 + ([st_in, st_in] if has_state else []),
        out_specs=(pl.BlockSpec((t, HG_W), lambda b, i: (b * nblk + i, 0)),
                   pl.BlockSpec((t, HG_W), lambda b, i: (b * nblk + nblk - 1 - i, 0)),
                   st_out, st_out),
        scratch_shapes=[bd_scratch, bd_scratch] + [pltpu.VMEM((t, HG_KW), F32)] * 10,
        compiler_params=_params(("arbitrary", "arbitrary")),
        name="hgrn_scan",
    )(z, z, z, z, z, z, row(loglb_f), row(l1m_f), row(loglb_b), row(l1m_b), *((s0f, s0b) if has_state else ()))


def _out_kernel(*refs, has_next):
    (x_ref, mod_ref, gpost_ref, ghg_ref, ona_ref, omla_ref, of_ref, ob_ref,
     gna_ref, gmla_ref, ghgate_ref, mg_ref, wna_ref, wmla_ref, whg_ref, wout_ref) = refs[:16]
    y_ref = refs[-2] if has_next else refs[-1]
    o = of_ref[...] + ob_ref[...]
    hr = lax.broadcasted_iota(jnp.int32, (HG_W, HG_W), 0) // HG_DV
    hc = lax.broadcasted_iota(jnp.int32, (HG_W, HG_W), 1) // HG_DV
    head_mean = jnp.where(hr == hc, 1.0 / HG_DV, 0.0).astype(BF16)
    sq = o * o
    sq_hi = sq.astype(BF16)
    sq_lo = (sq - sq_hi.astype(F32)).astype(BF16)
    ms = _mm(sq_hi, head_mean) + _mm(sq_lo, head_mean)
    o_hg = o * lax.rsqrt(ms + NORM_EPS) * ghg_ref[...]

    def branch(o_b, gate_ref, w_ref):
        return _mm((o_b * _silu(gate_ref[...].astype(F32))).astype(BF16), w_ref[...])

    def merge_gate(i):
        return _sigmoid(mg_ref[:, i * D_MODEL:(i + 1) * D_MODEL].astype(F32))

    merged = (merge_gate(0) * branch(ona_ref[...], gna_ref, wna_ref)
              + merge_gate(1) * branch(omla_ref[...], gmla_ref, wmla_ref)
              + merge_gate(2) * branch(o_hg, ghgate_ref, whg_ref))
    out = _mm(merged.astype(BF16), wout_ref[...])
    y = x_ref[...] + mod_ref[0, 2:3, :] * _rms(out, gpost_ref[...])
    y_ref[...] = y
    if has_next:
        modn_ref, gpren_ref, hn_ref = refs[16], refs[17], refs[-1]
        hn_ref[...] = _modulated_norm(y, gpren_ref[...], modn_ref).astype(BF16)


def _out_proj(x, z_head, z_mid, z_mg, mod3, g_post, g_hg, o_na, o_mla, o_f, o_b, w_na, w_mla, w_hg, w_out,
              rows_per_mod, mod_base, next_mod3=None, next_g_pre=None):
    m = x.shape[0]
    t = ROW_BLK
    tiles_per_mod = rows_per_mod // t
    has_next = next_mod3 is not None
    w512 = lambda: pl.BlockSpec((t, NA_W), lambda i: (i, 0))
    zcol = lambda col, w: pl.BlockSpec((t, w), lambda i: (i, col // w))
    mod_spec = pl.BlockSpec((1, 3, D_MODEL), lambda i: (mod_base + i // tiles_per_mod, 0, 0))
    row_spec = pl.BlockSpec((t, D_MODEL), lambda i: (i, 0))
    one = pl.Buffered(1)
    out = pl.pallas_call(
        functools.partial(_out_kernel, has_next=has_next),
        out_shape=(jax.ShapeDtypeStruct((m, D_MODEL), F32),)
                  + ((jax.ShapeDtypeStruct((m, D_MODEL), BF16),) if has_next else ()),
        grid=(m // t,),
        in_specs=[row_spec, mod_spec,
                  _const_spec((1, D_MODEL)), _const_spec((1, HG_W)),
                  w512(), w512(), w512(), w512(),
                  zcol(C_NA_G, NA_W), zcol(M_MLA_G, MLA_W), zcol(M_HG_G, HG_W),
                  zcol(0, MG_W),
                  pl.BlockSpec((NA_W, D_MODEL), lambda i: (0, 0), pipeline_mode=one),
                  pl.BlockSpec((MLA_W, D_MODEL), lambda i: (0, 0), pipeline_mode=one),
                  pl.BlockSpec((HG_W, D_MODEL), lambda i: (0, 0), pipeline_mode=one),
                  pl.BlockSpec((D_MODEL, D_MODEL), lambda i: (0, 0), pipeline_mode=one)]
                 + ([mod_spec, _const_spec((1, D_MODEL))] if has_next else []),
        out_specs=(row_spec,) + ((row_spec,) if has_next else ()),
        compiler_params=_params(("arbitrary",)),
        name="out_proj",
    )(x, mod3, g_post, g_hg, o_na, o_mla, o_f, o_b, z_head, z_mid, z_mid, z_mg, w_na, w_mla, w_hg, w_out,
      *((next_mod3, next_g_pre) if has_next else ()))
    return out if has_next else (out[0], None)


def kernel(x_prompt, x_sample, cache_na_k, cache_na_v, cache_mla_ckv, cache_mla_krope, state_hgrn_fwd, state_hgrn_bwd, c, c_ctx, w_ada, b_ada, g_pre, g_post, w_in, na_rpb, g_mla_q, w_mla_uq, g_mla_kv, w_mla_ukv, hg_lb_fwd, hg_lb_bwd, g_hg_out, w_br_na, w_br_mla, w_br_hg, w_out):
    bp, sp, _ = x_prompt.shape
    bs, ss, _ = x_sample.shape
    past = cache_na_k.shape[2]

    cvec = jnp.concatenate([c_ctx[None, :], c, jnp.zeros((8 - 1 - bs, D_MODEL), F32)], axis=0)
    mod = _modulation(cvec, w_ada, b_ada).reshape(DEPTH, 8, 3, D_MODEL)

    lb_f_all = _lower_bounds(hg_lb_fwd)
    lb_b_all = _lower_bounds(hg_lb_bwd)
    cos_t, sin_t = _rope_tables(ss)
    cache_k = cache_na_k.reshape(bs, DEPTH, past, NA_W)
    cache_v = cache_na_v.reshape(bs, DEPTH, past, NA_W)
    cache_kr_pad = _rope_lanes(cache_mla_krope)

    yp = x_prompt.reshape(bp * sp, D_MODEL)
    ys = x_sample.reshape(bs * ss, D_MODEL)
    new_k, new_v, new_ckv, new_kr, new_sf, new_sb = [], [], [], [], [], []
    w_head, w_tail = _pack_w_in(w_in)
    hp = _prenorm(yp, mod[0], g_pre[0][None, :], bp * sp, 0)
    hs = _prenorm(ys, mod[0], g_pre[0][None, :], ss, 1)
    for l in range(DEPTH):
        wq, wqp = _pack_w_uq(w_mla_uq[l])
        wuk, wuv = _pack_w_ukv(w_mla_ukv[l])
        w_na, w_mla, w_hg, w_o = (w.astype(BF16) for w in (w_br_na[l], w_br_mla[l], w_br_hg[l], w_out[l]))
        g_q, g_kv = (g_mla_q[l] * (MLA_QK ** -0.5 * LOG2E))[None, :], g_mla_kv[l][None, :]
        gpost, ghg = g_post[l][None, :], g_hg_out[l][None, :]
        mod3 = mod[l]
        nxt = (mod[l + 1], g_pre[l + 1][None, :]) if l + 1 < DEPTH else (None, None)

        zp_head, zp_mid, zp_mg = _in_proj(hp, w_head, w_tail, l)
        o_na, o_mla, ckvn, k_new, v_new, kr_new = _prompt_attn(zp_head, bp, g_q, g_kv, wq, wuk, wuv)
        o_f, o_b, sf, sb = _hgrn(zp_mid, bp, lb_f_all[l], lb_b_all[l])
        yp, hp = _out_proj(yp, zp_head, zp_mid, zp_mg, mod3, gpost, ghg, o_na, o_mla, o_f, o_b,
                           w_na, w_mla, w_hg, w_o, bp * sp, 0, *nxt)
        new_k.append(k_new.reshape(bp, sp, NA_HEADS, NA_HD))
        new_v.append(v_new.reshape(bp, sp, NA_HEADS, NA_HD))
        new_ckv.append(ckvn.reshape(bp, sp, MLA_KV_LORA))
        new_kr.append(kr_new.reshape(bp, sp, MLA_ROPE))
        new_sf.append(sf)
        new_sb.append(sb)

        zs_head, zs_mid, zs_mg = _in_proj(hs, w_head, w_tail, l)
        o_na = _sample_na(zs_head, cache_k, cache_v, _na_bias_table(na_rpb[l]), l, bs)
        kq_all, vp_all = _sample_mla_kv(zs_head, cache_mla_ckv, cache_kr_pad, cos_t, sin_t, g_kv, wuk, wuv, l, bs)
        o_mla = _sample_mla_attn(zs_head, cos_t, sin_t, g_q, wq, wqp, kq_all, vp_all, bs)
        o_f, o_b, _, _ = _hgrn(zs_mid, bs, lb_f_all[l], lb_b_all[l], state_hgrn_fwd, state_hgrn_bwd, l)
        ys, hs = _out_proj(ys, zs_head, zs_mid, zs_mg, mod3, gpost, ghg, o_na, o_mla, o_f, o_b,
                           w_na, w_mla, w_hg, w_o, ss, 1, *nxt)

    return (yp.reshape(bp, sp, D_MODEL), ys.reshape(bs, ss, D_MODEL),
            jnp.stack(new_k, axis=1), jnp.stack(new_v, axis=1), jnp.stack(new_ckv, axis=1),
            jnp.stack(new_kr, axis=1), jnp.stack(new_sf, axis=1), jnp.stack(new_sb, axis=1))
```

```python
import functools

import numpy as np
import jax
import jax.numpy as jnp
from jax import lax
from jax.experimental import pallas as pl
from jax.experimental.pallas import tpu as pltpu

D_MODEL = 2048
DEPTH = 2
GRID_W = 64
NORM_EPS = 1e-6
NEG_BIG = -1e30
NA_HEADS = 8
NA_HD = 64
NA_W = NA_HEADS * NA_HD
NA_KR = 8
NA_KW = 16
MLA_HEADS = 8
MLA_NOPE = 64
MLA_ROPE = 32
MLA_VD = 64
MLA_QK = MLA_NOPE + MLA_ROPE
MLA_W = MLA_HEADS * MLA_VD
MLA_Q_LORA = 512
MLA_KV_LORA = 256
ROPE_BASE = 10000.0
HG_HEADS = 8
HG_DK = 64
HG_DV = 64
HG_KW = HG_HEADS * HG_DK
HG_W = HG_HEADS * HG_DV
N_BRANCH = 3
PROJ_WIDTHS = (NA_W, NA_W, NA_W, NA_W, MLA_Q_LORA, MLA_KV_LORA, MLA_ROPE, MLA_W,
               HG_KW, HG_KW, HG_KW, HG_W, HG_W, N_BRANCH * D_MODEL)

F32 = jnp.float32
BF16 = jnp.bfloat16
HI = lax.Precision.HIGHEST
LOG2E = 1.4426950408889634

LANE = 128
ROW_BLK = 256
HG_CHUNK = 32
HG_SUB = 16
HG_SAFE_DECAY = 72.0
HG_EXP_CLAMP = 80.0
VMEM_LIMIT = 56 * 1024 * 1024

HEAD_W = 3072
MID_W = 3072
MG_W = N_BRANCH * D_MODEL
SPLIT_COL = 2848
C_NA_Q, C_NA_K, C_NA_V, C_NA_G = 0, 512, 1024, 1536
C_MLA_CQ, C_MLA_CKV, C_MLA_KR, C_MLA_KRP = 2048, 2560, 2816, 2944
M_MLA_G, M_HG_Q, M_HG_FF, M_HG_FB, M_HG_I, M_HG_G = 0, 512, 1024, 1536, 2048, 2560


def _nt(a, b, precision=None):
    return lax.dot_general(a, b, (((1,), (1,)), ((), ())), preferred_element_type=F32, precision=precision)


def _tn(a, b, precision=None):
    return lax.dot_general(a, b, (((0,), (0,)), ((), ())), preferred_element_type=F32, precision=precision)


def _mm(a, b, precision=None):
    return jnp.dot(a, b, preferred_element_type=F32, precision=precision)


def _rms(x, g):
    return x * lax.rsqrt(jnp.mean(x * x, axis=-1, keepdims=True) + NORM_EPS) * g


def _sigmoid(x):
    return 0.5 * jnp.tanh(0.5 * x) + 0.5


def _silu(x):
    return x * _sigmoid(x)


def _params(sem):
    return pltpu.CompilerParams(dimension_semantics=sem, vmem_limit_bytes=VMEM_LIMIT)


def _const_spec(shape):
    nd = len(shape)
    return pl.BlockSpec(shape, lambda *_: (0,) * nd)


def _rot_half(a):
    parts = []
    for ax in range(2):
        lo_, hi_ = a[..., ax * 16:ax * 16 + 8], a[..., ax * 16 + 8:ax * 16 + 16]
        parts += [-hi_, lo_]
    return jnp.concatenate(parts, axis=-1)


def _pack_w_in(w):
    assert sum(PROJ_WIDTHS[:7]) == SPLIT_COL and sum(PROJ_WIDTHS[7:13]) == MID_W and PROJ_WIDTHS[13] == MG_W
    head = w[..., :SPLIT_COL].astype(BF16)
    kr = head[..., C_MLA_KR:SPLIT_COL]
    return jnp.concatenate([head[..., :C_MLA_KR], _rope_lanes(kr), _rope_lanes(_rot_half(kr))], axis=-1), \
        w[..., SPLIT_COL:].astype(BF16)


def _rope_lanes(a):
    z = lambda n: jnp.zeros(a.shape[:-1] + (n,), a.dtype)
    return jnp.concatenate([z(MLA_NOPE), a, z(LANE - MLA_QK)], axis=-1)


def _pack_w_uq(w):
    w3 = w.reshape(MLA_Q_LORA, MLA_HEADS, MLA_QK)
    nope, rope = w3[..., :MLA_NOPE], w3[..., MLA_NOPE:]
    pad = jnp.zeros(nope.shape[:-1] + (LANE - MLA_QK,), w.dtype)
    flat = lambda a: a.reshape(MLA_Q_LORA, MLA_HEADS * LANE).astype(BF16)
    return flat(jnp.concatenate([nope, rope, pad], axis=-1)), flat(_rope_lanes(_rot_half(rope)))


def _pack_w_ukv(w):
    w3 = w.reshape(MLA_KV_LORA, MLA_HEADS, MLA_NOPE + MLA_VD)
    pad = jnp.zeros((MLA_KV_LORA, MLA_HEADS, LANE - MLA_NOPE), w.dtype)
    flat = lambda a: a.reshape(MLA_KV_LORA, MLA_HEADS * LANE).astype(BF16)
    return flat(jnp.concatenate([w3[..., :MLA_NOPE], pad], -1)), flat(jnp.concatenate([w3[..., MLA_NOPE:], pad], -1))


def _rope_tables(n):
    t = np.arange(n)
    pos = np.stack([t // GRID_W, t % GRID_W]).astype(np.float32)
    axis_dim = MLA_ROPE // 2
    inv = ROPE_BASE ** (-jnp.arange(0, axis_dim, 2, dtype=F32) / axis_dim)
    ang = jnp.asarray(pos)[:, :, None] * inv
    ang = jnp.concatenate([ang, ang], axis=-1)
    ang = jnp.concatenate([ang[0], ang[1]], axis=-1)
    cos_t = jnp.concatenate([jnp.ones((n, MLA_NOPE), F32), jnp.cos(ang), jnp.zeros((n, LANE - MLA_QK), F32)], -1)
    return cos_t, _rope_lanes(jnp.sin(ang))


def _na_bias_table(rpb):
    rows = 32
    c = np.arange(GRID_W)
    win0 = np.clip(c - NA_KW // 2, 0, GRID_W - NA_KW)
    kc = np.arange(GRID_W)
    col_ok = (kc[None, :] >= win0[:, None]) & (kc[None, :] < win0[:, None] + NA_KW)
    rpb = rpb.astype(F32)
    edge = GRID_W - NA_KW
    ext = jnp.concatenate([jnp.broadcast_to(rpb[..., :1], rpb.shape[:-1] + (edge,)), rpb,
                           jnp.broadcast_to(rpb[..., -1:], rpb.shape[:-1] + (edge,))], axis=-1)
    toep = jnp.stack([ext[..., GRID_W - 1 - ci:2 * GRID_W - 1 - ci] for ci in range(GRID_W)], axis=2)
    toep = jnp.where(jnp.asarray(col_ok), toep * LOG2E, NEG_BIG)
    masked = jnp.full((NA_HEADS, GRID_W, GRID_W), NEG_BIG, F32)
    out = []
    for start, r0 in ((0, 0), (4, 8), (20, 28)):
        per_q = []
        for qr in range(4):
            r = r0 + qr
            kr0 = min(max(r - NA_KR // 2, 0), rows - NA_KR)
            tiles = []
            for j in range(12):
                kabs = start + j
                ok = kr0 <= kabs < kr0 + NA_KR
                tiles.append(toep[:, kabs - r + NA_KR - 1] if ok else masked)
            per_q.append(jnp.concatenate(tiles, axis=-1))
        out.append(jnp.concatenate(per_q, axis=1))
    return jnp.stack(out)


def _lower_bounds(p):
    s = jax.nn.softmax(p.astype(F32), axis=0)
    return jnp.cumsum(s, axis=0) - s[0]


def _lb_logs(lb):
    pos = lb > 0
    log_lb = jnp.where(pos, jnp.log(jnp.where(pos, lb, 1.0)), NEG_BIG)
    return log_lb, jnp.log1p(-lb)


def _mod_kernel(c_ref, w_ref, b_ref, o_ref):
    s = _silu(c_ref[...])
    o_ref[0] = _mm(s, w_ref[0], HI) + b_ref[0]


def _modulation(cvec, w_ada, b_ada):
    tn = 1024
    n3 = 3 * D_MODEL
    return pl.pallas_call(
        _mod_kernel,
        out_shape=jax.ShapeDtypeStruct((DEPTH, 8, n3), F32),
        grid=(DEPTH, n3 // tn),
        in_specs=[pl.BlockSpec((8, D_MODEL), lambda l, j: (0, 0)),
                  pl.BlockSpec((1, D_MODEL, tn), lambda l, j: (l, 0, j)),
                  pl.BlockSpec((1, 1, tn), lambda l, j: (l, 0, j))],
        out_specs=pl.BlockSpec((1, 8, tn), lambda l, j: (l, 0, j)),
        compiler_params=_params(("arbitrary", "arbitrary")),
        name="adaln_mod",
    )(cvec, w_ada, b_ada.reshape(DEPTH, 1, n3))


IN_TM, IN_TN = 2048, 1024


def _modulated_norm(x, g, mod_ref):
    return _rms(x, g) * (1.0 + mod_ref[0, 1:2, :]) + mod_ref[0, 0:1, :]


def _prenorm_kernel(x_ref, mod_ref, g_ref, h_ref):
    h_ref[...] = _modulated_norm(x_ref[...], g_ref[...], mod_ref).astype(BF16)


def _prenorm(x, mod3, g_pre, rows_per_mod, mod_base):
    m = x.shape[0]
    t = ROW_BLK
    tiles_per_mod = rows_per_mod // t
    return pl.pallas_call(
        _prenorm_kernel,
        out_shape=jax.ShapeDtypeStruct((m, D_MODEL), BF16),
        grid=(m // t,),
        in_specs=[pl.BlockSpec((t, D_MODEL), lambda i: (i, 0)),
                  pl.BlockSpec((1, 3, D_MODEL), lambda i: (mod_base + i // tiles_per_mod, 0, 0)),
                  _const_spec((1, D_MODEL))],
        out_specs=pl.BlockSpec((t, D_MODEL), lambda i: (i, 0)),
        compiler_params=_params(("arbitrary",)),
        name="prenorm",
    )(x, mod3, g_pre)


def _proj_kernel(h_ref, w_ref, z_ref):
    z_ref[...] = _mm(h_ref[...], w_ref[0]).astype(z_ref.dtype)


def _proj(h, w, layer, tile0, width, out_dtype):
    m = h.shape[0]
    tm, tn = IN_TM, IN_TN
    return pl.pallas_call(
        _proj_kernel,
        out_shape=jax.ShapeDtypeStruct((m, width), out_dtype),
        grid=(m // tm, width // tn),
        in_specs=[pl.BlockSpec((tm, D_MODEL), lambda i, j: (i, 0)),
                  pl.BlockSpec((1, D_MODEL, tn), lambda i, j: (layer, 0, tile0 + j))],
        out_specs=pl.BlockSpec((tm, tn), lambda i, j: (i, j)),
        compiler_params=_params(("arbitrary", "arbitrary")),
        name="in_proj",
    )(h, w)


def _in_proj(h, w_head, w_tail, layer):
    z_head = _proj(h, w_head, layer, 0, HEAD_W, F32)
    z_mid = _proj(h, w_tail, layer, 0, MID_W, BF16)
    z_mg = _proj(h, w_tail, layer, MID_W // IN_TN, MG_W, BF16)
    return z_head, z_mid, z_mg


def _softmax_pv(s_list, v_list):
    m = s_list[0].max(axis=-1, keepdims=True)
    for s in s_list[1:]:
        m = jnp.maximum(m, s.max(axis=-1, keepdims=True))
    acc, den = None, None
    for s, v in zip(s_list, v_list):
        e = jnp.exp2(s - m)
        d = e.sum(axis=-1, keepdims=True)
        o = _mm(e.astype(BF16), v)
        acc = o if acc is None else acc + o
        den = d if den is None else den + d
    return acc, den


def _na_head_pairs(q, k_fn, v_fn, scores_fn):
    low = lax.broadcasted_iota(jnp.int32, (1, LANE), 1) < NA_HD
    outs = []
    for p in range(NA_HEADS // 2):
        sl = slice(p * LANE, (p + 1) * LANE)
        q2, k_pair, v_pair = q[:, sl], k_fn(sl), v_fn(sl)
        halves = []
        for half in range(2):
            qh = jnp.where(low if half == 0 else jnp.logical_not(low), q2, 0.0).astype(BF16)
            acc, den = _softmax_pv(scores_fn(qh, k_pair, 2 * p + half), v_pair)
            halves.append(acc / den)
        outs.append(jnp.where(low, halves[0], halves[1]))
    return jnp.concatenate(outs, axis=-1)


def _ones_lane():
    return (lax.broadcasted_iota(jnp.int32, (1, LANE), 1) == MLA_VD).astype(F32)


def _mla_heads(q_fn, kq, vp, o_ref):
    outs = []
    for h in range(MLA_HEADS):
        sl = slice(h * LANE, (h + 1) * LANE)
        s = _nt(q_fn(sl), kq[:, sl])
        e = jnp.exp2(s - s.max(axis=-1, keepdims=True))
        acc = _mm(e.astype(BF16), vp[:, sl])
        outs.append(acc[:, :MLA_VD] / acc[:, MLA_VD:MLA_VD + 1])
    o_ref[...] = jnp.concatenate(outs, axis=-1)


def _mla_keys_values(ckvn16, kr_tile, wuk_ref, wuv_ref):
    kq = _mm(ckvn16, wuk_ref[...]) + jnp.concatenate([kr_tile] * MLA_HEADS, axis=-1)
    vp = _mm(ckvn16, wuv_ref[...]) + jnp.concatenate([_ones_lane()] * MLA_HEADS, axis=-1)
    return kq.astype(BF16), vp.astype(BF16)


def _prompt_attn_kernel(qkv_ref, cq_ref, ckv_ref, kr_ref, gq_ref, gkv_ref, wq_ref, wuk_ref, wuv_ref,
                        ona_ref, omla_ref, ckvn_ref, k_ref, v_ref, kro_ref):
    qkv = qkv_ref[...]
    k_ref[...] = qkv[:, NA_W:2 * NA_W]
    v_ref[...] = qkv[:, 2 * NA_W:3 * NA_W]
    kro_ref[...] = kr_ref[:, MLA_NOPE:MLA_QK]
    ckvn = _rms(ckv_ref[...], gkv_ref[...])
    ckvn_ref[...] = ckvn
    scale = NA_HD ** -0.5 * LOG2E
    outs = []
    for h in range(NA_HEADS):
        qh = (qkv[:, h * NA_HD:(h + 1) * NA_HD] * scale).astype(BF16)
        kh = qkv[:, NA_W + h * NA_HD:NA_W + (h + 1) * NA_HD].astype(BF16)
        vh = qkv[:, 2 * NA_W + h * NA_HD:2 * NA_W + (h + 1) * NA_HD].astype(BF16)
        acc, den = _softmax_pv([_nt(qh, kh)], [vh])
        outs.append(acc / den)
    ona_ref[...] = jnp.concatenate(outs, axis=-1)

    cqn = _rms(cq_ref[...], gq_ref[...]).astype(BF16)
    q = _mm(cqn, wq_ref[...]).astype(BF16)
    kq, vp = _mla_keys_values(ckvn.astype(BF16), kr_ref[...], wuk_ref, wuv_ref)
    _mla_heads(lambda sl: q[:, sl], kq, vp, omla_ref)


def _prompt_attn(z, n_batch, g_q, g_kv, wq, wuk, wuv):
    m = z.shape[0]
    t = ROW_BLK
    hw = MLA_HEADS * LANE
    return pl.pallas_call(
        _prompt_attn_kernel,
        out_shape=(jax.ShapeDtypeStruct((m, NA_W), F32), jax.ShapeDtypeStruct((m, MLA_W), F32),
                   jax.ShapeDtypeStruct((m, MLA_KV_LORA), F32), jax.ShapeDtypeStruct((m, NA_W), F32),
                   jax.ShapeDtypeStruct((m, NA_W), F32), jax.ShapeDtypeStruct((m, MLA_ROPE), F32)),
        grid=(n_batch,),
        in_specs=[pl.BlockSpec((t, 3 * NA_W), lambda b: (b, C_NA_Q // (3 * NA_W))),
                  pl.BlockSpec((t, MLA_Q_LORA), lambda b: (b, C_MLA_CQ // MLA_Q_LORA)),
                  pl.BlockSpec((t, MLA_KV_LORA), lambda b: (b, C_MLA_CKV // MLA_KV_LORA)),
                  pl.BlockSpec((t, LANE), lambda b: (b, C_MLA_KR // LANE)),
                  _const_spec((1, MLA_Q_LORA)), _const_spec((1, MLA_KV_LORA)),
                  _const_spec((MLA_Q_LORA, hw)), _const_spec((MLA_KV_LORA, hw)), _const_spec((MLA_KV_LORA, hw))],
        out_specs=(pl.BlockSpec((t, NA_W), lambda b: (b, 0)), pl.BlockSpec((t, MLA_W), lambda b: (b, 0)),
                   pl.BlockSpec((t, MLA_KV_LORA), lambda b: (b, 0)), pl.BlockSpec((t, NA_W), lambda b: (b, 0)),
                   pl.BlockSpec((t, NA_W), lambda b: (b, 0)), pl.BlockSpec((t, MLA_ROPE), lambda b: (b, 0))),
        compiler_params=_params(("arbitrary",)),
        name="prompt_attn",
    )(z, z, z, z, g_q, g_kv, wq, wuk, wuv)


def _sample_na_kernel(q_ref, k0_ref, k1_ref, k2_ref, v0_ref, v1_ref, v2_ref, kc_ref, vc_ref, bias_ref, o_ref):
    q = q_ref[...] * (NA_HD ** -0.5 * LOG2E)
    kl = [r[...] for r in (k0_ref, k1_ref, k2_ref)]
    vl = [r[...] for r in (v0_ref, v1_ref, v2_ref)]
    kc, vc = kc_ref[0, 0], vc_ref[0, 0]

    def scores(qh, k_pair, h):
        s_list = [_nt(qh, k_pair[i]) + bias_ref[0, h, :, i * ROW_BLK:(i + 1) * ROW_BLK] for i in range(3)]
        return s_list + [_nt(qh, k_pair[3])]

    o_ref[...] = _na_head_pairs(q, lambda sl: [a[:, sl].astype(BF16) for a in kl + [kc]],
                                lambda sl: [a[:, sl].astype(BF16) for a in vl + [vc]], scores)


def _sample_na(z, cache_k, cache_v, bias_tab, layer, n_batch):
    m = z.shape[0]
    t = ROW_BLK
    nblk = m // n_batch // t
    past = cache_k.shape[2]

    def kv_map(col, i):
        return lambda b, rb: (b * nblk + jnp.clip(rb - 1, 0, nblk - 3) + i, col // NA_W)

    def variant(b, rb):
        return (jnp.where(rb == 0, 0, jnp.where(rb == nblk - 1, 2, 1)), 0, 0, 0)

    cache_spec = pl.BlockSpec((1, 1, past, NA_W), lambda b, rb: (b, layer, 0, 0))
    return pl.pallas_call(
        _sample_na_kernel,
        out_shape=jax.ShapeDtypeStruct((m, NA_W), F32),
        grid=(n_batch, nblk),
        in_specs=[pl.BlockSpec((t, NA_W), lambda b, rb: (b * nblk + rb, C_NA_Q // NA_W))]
                 + [pl.BlockSpec((t, NA_W), kv_map(C_NA_K, i)) for i in range(3)]
                 + [pl.BlockSpec((t, NA_W), kv_map(C_NA_V, i)) for i in range(3)]
                 + [cache_spec, cache_spec,
                    pl.BlockSpec((1, NA_HEADS, t, 3 * t), variant)],
        out_specs=pl.BlockSpec((t, NA_W), lambda b, rb: (b * nblk + rb, 0)),
        compiler_params=_params(("arbitrary", "arbitrary")),
        name="sample_na",
    )(z, z, z, z, z, z, z, cache_k, cache_v, bias_tab)


def _sample_mla_kv_kernel(cckv_ref, ckr_ref, ckv_ref, kr_ref, krp_ref, cos_ref, sin_ref, gkv_ref, wuk_ref, wuv_ref,
                          kq_ref, vp_ref):
    j = pl.program_id(1)

    @pl.when(j == 0)
    def _():
        kq_ref[0], vp_ref[0] = _mla_keys_values(cckv_ref[0, 0].astype(BF16), ckr_ref[0, 0], wuk_ref, wuv_ref)

    @pl.when(j > 0)
    def _():
        ckvn = _rms(ckv_ref[...], gkv_ref[...])
        kr_roped = kr_ref[...] * cos_ref[...] + krp_ref[...] * sin_ref[...]
        kq_ref[0], vp_ref[0] = _mla_keys_values(ckvn.astype(BF16), kr_roped, wuk_ref, wuv_ref)


def _sample_mla_kv(z, cache_ckv, cache_kr_pad, cos_t, sin_t, g_kv, wuk, wuv, layer, n_batch):
    m = z.shape[0]
    n = m // n_batch
    past = cache_ckv.shape[2]
    t = past
    nb = n // t
    hw = MLA_HEADS * LANE

    def zrow(b, j):
        return b * nb + jnp.maximum(j - 1, 0)

    return pl.pallas_call(
        _sample_mla_kv_kernel,
        out_shape=(jax.ShapeDtypeStruct((n_batch, past + n, hw), BF16),
                   jax.ShapeDtypeStruct((n_batch, past + n, hw), BF16)),
        grid=(n_batch, nb + 1),
        in_specs=[pl.BlockSpec((1, 1, past, MLA_KV_LORA), lambda b, j: (b, layer, 0, 0)),
                  pl.BlockSpec((1, 1, past, LANE), lambda b, j: (b, layer, 0, 0)),
                  pl.BlockSpec((t, MLA_KV_LORA), lambda b, j: (zrow(b, j), C_MLA_CKV // MLA_KV_LORA)),
                  pl.BlockSpec((t, LANE), lambda b, j: (zrow(b, j), C_MLA_KR // LANE)),
                  pl.BlockSpec((t, LANE), lambda b, j: (zrow(b, j), C_MLA_KRP // LANE)),
                  pl.BlockSpec((t, LANE), lambda b, j: (jnp.maximum(j - 1, 0), 0)),
                  pl.BlockSpec((t, LANE), lambda b, j: (jnp.maximum(j - 1, 0), 0)),
                  _const_spec((1, MLA_KV_LORA)), _const_spec((MLA_KV_LORA, hw)), _const_spec((MLA_KV_LORA, hw))],
        out_specs=(pl.BlockSpec((1, t, hw), lambda b, j: (b, j, 0)),
                   pl.BlockSpec((1, t, hw), lambda b, j: (b, j, 0))),
        compiler_params=_params(("arbitrary", "arbitrary")),
        name="sample_mla_kv",
    )(cache_ckv, cache_kr_pad, z, z, z, cos_t, sin_t, g_kv, wuk, wuv)


def _sample_mla_attn_kernel(cq_ref, cos_ref, sin_ref, gq_ref, wq_ref, wqp_ref, kq_ref, vp_ref, o_ref):
    cqn = _rms(cq_ref[...], gq_ref[...]).astype(BF16)
    q = _mm(cqn, wq_ref[...])
    q_rot = _mm(cqn, wqp_ref[...])
    cos, sin = cos_ref[...], sin_ref[...]
    _mla_heads(lambda sl: (q[:, sl] * cos + q_rot[:, sl] * sin).astype(BF16), kq_ref[0], vp_ref[0], o_ref)


def _sample_mla_attn(z, cos_t, sin_t, g_q, wq, wqp, kq_all, vp_all, n_batch):
    m = z.shape[0]
    t = 2 * ROW_BLK
    nblk = m // n_batch // t
    nk = kq_all.shape[1]
    hw = MLA_HEADS * LANE
    return pl.pallas_call(
        _sample_mla_attn_kernel,
        out_shape=jax.ShapeDtypeStruct((m, MLA_W), F32),
        grid=(n_batch, nblk),
        in_specs=[pl.BlockSpec((t, MLA_Q_LORA), lambda b, i: (b * nblk + i, C_MLA_CQ // MLA_Q_LORA)),
                  pl.BlockSpec((t, LANE), lambda b, i: (i, 0)),
                  pl.BlockSpec((t, LANE), lambda b, i: (i, 0)),
                  _const_spec((1, MLA_Q_LORA)),
                  _const_spec((MLA_Q_LORA, hw)), _const_spec((MLA_Q_LORA, hw)),
                  pl.BlockSpec((1, nk, hw), lambda b, i: (b, 0, 0)),
                  pl.BlockSpec((1, nk, hw), lambda b, i: (b, 0, 0))],
        out_specs=pl.BlockSpec((t, MLA_W), lambda b, i: (b * nblk + i, 0)),
        compiler_params=_params(("arbitrary", "arbitrary")),
        name="sample_mla_attn",
    )(z, cos_t, sin_t, g_q, wq, wqp, kq_all, vp_all)


def _hg_direction(fwd, q_ref, zf_ref, v_ref, loglb_ref, l1m_ref, s_ref, o_ref, qs, ks, bs, vs, os):
    t = ROW_BLK
    nchunk = t // HG_CHUNK
    q = _silu(q_ref[...].astype(F32))
    zf = zf_ref[...].astype(F32)
    v = v_ref[...].astype(F32)
    soft = jnp.log(1.0 + jnp.exp(-jnp.abs(zf)))
    a1 = jnp.broadcast_to(loglb_ref[...], zf.shape)
    a2 = l1m_ref[...] + (jnp.minimum(zf, 0.0) - soft)
    logf = jnp.maximum(a1, a2) + jnp.log(1.0 + jnp.exp(-jnp.abs(a1 - a2)))
    k = jnp.exp(l1m_ref[...] - jnp.maximum(zf, 0.0) - soft)

    def chunk_masks(n):
        ri = lax.broadcasted_iota(jnp.int32, (n, n), 0)
        ci = lax.broadcasted_iota(jnp.int32, (n, n), 1)
        causal = (ri >= ci) if fwd else (ri <= ci)
        same_sub = (ri // HG_SUB) == (ci // HG_SUB)
        same = (ri // HG_CHUNK) == (ci // HG_CHUNK)
        return same_sub & causal, same & jnp.logical_not(same_sub) & causal, same_sub, same

    tri_sub, _, same_sub, same = chunk_masks(t)
    hi = logf.astype(BF16)
    lo = (logf - hi.astype(F32)).astype(BF16)

    def seg_sum(mask):
        m16 = jnp.where(mask, 1.0, 0.0).astype(BF16)
        return _mm(m16, hi) + _mm(m16, lo)

    b_sub = seg_sum(tri_sub)
    nsub = t // HG_SUB
    b3 = b_sub.reshape(nsub, HG_SUB, HG_KW)
    edge = b3[:, HG_SUB - 1:HG_SUB, :] if fwd else b3[:, 0:1, :]
    tot_sub = jnp.broadcast_to(edge, (nsub, HG_SUB, HG_KW)).reshape(t, HG_KW)
    pair = edge.reshape(nchunk, 2, 1, HG_KW)
    tot = jnp.broadcast_to(pair[:, 0:1] + pair[:, 1:2], (nchunk, 2, HG_SUB, HG_KW)).reshape(t, HG_KW)
    row = lax.broadcasted_iota(jnp.int32, (t, 1), 0) % HG_CHUNK
    later = (row >= HG_SUB) if fwd else (row < HG_SUB)
    b = b_sub + jnp.where(later, tot - tot_sub, 0.0)

    qt_sub = (q * jnp.exp(b_sub)).astype(BF16)
    kt_sub = (k * jnp.exp(jnp.minimum(-b_sub, HG_EXP_CLAMP))).astype(BF16)
    kh_sub = (k * jnp.exp(tot_sub - b_sub)).astype(BF16)
    qt16 = (q * jnp.exp(b)).astype(BF16)
    kh16 = (k * jnp.exp(tot - b)).astype(BF16)
    v16 = v.astype(BF16)

    hb = t // 2
    tri_hb, cross_hb, _, _ = chunk_masks(hb)
    low = lax.broadcasted_iota(jnp.int32, (1, LANE), 1) < HG_DK
    zero16 = jnp.zeros((), BF16)
    outs = []
    for p2 in range(HG_HEADS // 2):
        sl = slice(p2 * LANE, (p2 + 1) * LANE)
        parts = []
        for r0 in (0, hb):
            rs = slice(r0, r0 + hb)
            keys = jnp.concatenate([kt_sub[rs, sl], kh_sub[rs, sl]], axis=0)
            halves = []
            for keep in (low, jnp.logical_not(low)):
                p = _nt(jnp.where(keep, qt_sub[rs, sl], zero16), keys)
                a = jnp.where(tri_hb, p[:, :hb], 0.0) + jnp.where(cross_hb, p[:, hb:], 0.0)
                halves.append(_mm(a.astype(BF16), v16[rs, sl]))
            parts.append(jnp.where(low, halves[0], halves[1]))
        outs.append(jnp.concatenate(parts, axis=0))
    o_ref[...] = jnp.concatenate(outs, axis=-1)

    gw = 4 * HG_DK
    bd_r = lax.broadcasted_iota(jnp.int32, (gw, gw), 0) // HG_DV
    bd_c = lax.broadcasted_iota(jnp.int32, (gw, gw), 1) // HG_DK
    bd = bd_r == bd_c
    order = range(nchunk) if fwd else range(nchunk - 1, -1, -1)
    for g in range(2):
        ls = slice(g * gw, (g + 1) * gw)
        rows = {c: slice(c * HG_CHUNK, (c + 1) * HG_CHUNK) for c in order}
        updates = {c: jnp.where(bd, _tn(v16[rows[c], ls], kh16[rows[c], ls]), 0.0) for c in order}
        s = s_ref[g]
        entering = {}
        for c in order:
            entering[c] = s.astype(BF16)
            s = s * jnp.exp(tot[c * HG_CHUNK:c * HG_CHUNK + 1, ls]) + updates[c]
        s_ref[g] = s
        for c in order:
            inter = _nt(qt16[rows[c], ls], entering[c])
            o_ref[rows[c], ls] += inter
            os[rows[c], ls] = inter

    qs[...] = q
    ks[...] = k
    bs[...] = b
    vs[...] = v
    return [jnp.max(-jnp.minimum(tot_sub[c * HG_CHUNK:c * HG_CHUNK + 1, :],
                                 tot_sub[c * HG_CHUNK + HG_SUB:c * HG_CHUNK + HG_SUB + 1, :]))
            for c in range(nchunk)]


def _hg_pairwise(fwd, worst, o_ref, qs, ks, bs, vs, os):
    lane = lax.broadcasted_iota(jnp.int32, (HG_KW, LANE), 0) // HG_DK
    col = lax.broadcasted_iota(jnp.int32, (HG_KW, LANE), 1)
    head_sum = (lane == col).astype(F32)
    srow = lax.broadcasted_iota(jnp.int32, (HG_CHUNK, 1), 0)
    for c, worst_c in enumerate(worst):
        r0 = c * HG_CHUNK

        @pl.when(worst_c > HG_SAFE_DECAY)
        def _():
            kc = ks[r0:r0 + HG_CHUNK, :]
            bc = bs[r0:r0 + HG_CHUNK, :]
            vc = vs[r0:r0 + HG_CHUNK, :]

            def body(i, carry):
                qrow = qs[pl.ds(r0 + i, 1), :]
                brow = bs[pl.ds(r0 + i, 1), :]
                p = qrow * kc * jnp.exp(jnp.minimum(brow - bc, 0.0))
                keep = (srow <= i) if fwd else (srow >= i)
                p = jnp.where(keep, p, 0.0)
                a = _mm(p, head_sum, HI)
                a_full = _nt(a, head_sum, HI)
                o_ref[pl.ds(r0 + i, 1), :] = (jnp.sum(a_full * vc, axis=0, keepdims=True)
                                              + os[pl.ds(r0 + i, 1), :])
                return carry

            lax.fori_loop(0, HG_CHUNK, body, 0)


def _head_block(h):
    g, hh = divmod(h, 4)
    return g, slice(hh * HG_DV, (hh + 1) * HG_DV), slice(hh * HG_DK, (hh + 1) * HG_DK)


def _hgrn_kernel(*refs, has_state):
    (qf_ref, ff_ref, vf_ref, qb_ref, fb_ref, vb_ref, lbf_ref, l1f_ref, lbb_ref, l1b_ref) = refs[:10]
    s0_refs = refs[10:12] if has_state else (None, None)
    rest = refs[12:] if has_state else refs[10:]
    of_ref, ob_ref, sf_out, sb_out, sf, sb = rest[:6]
    scr_f, scr_b = rest[6:11], rest[11:16]
    i = pl.program_id(1)

    @pl.when(i == 0)
    def _():
        for scr, s0_ref in zip((sf, sb), s0_refs):
            scr[...] = jnp.zeros(scr.shape, F32)
            if s0_ref is not None:
                for h in range(HG_HEADS):
                    g, rv, ck = _head_block(h)
                    scr[g, rv, ck] = s0_ref[0, 0, h].T

    worst_f = _hg_direction(True, qf_ref, ff_ref, vf_ref, lbf_ref, l1f_ref, sf, of_ref, *scr_f)
    worst_b = _hg_direction(False, qb_ref, fb_ref, vb_ref, lbb_ref, l1b_ref, sb, ob_ref, *scr_b)

    @pl.when(functools.reduce(jnp.maximum, worst_f + worst_b) > HG_SAFE_DECAY)
    def _():
        _hg_pairwise(True, worst_f, of_ref, *scr_f)
        _hg_pairwise(False, worst_b, ob_ref, *scr_b)

    @pl.when(i == pl.num_programs(1) - 1)
    def _():
        for scr, out in ((sf, sf_out), (sb, sb_out)):
            for h in range(HG_HEADS):
                g, rv, ck = _head_block(h)
                out[0, h] = scr[g, rv, ck].T


def _hgrn(z, n_batch, lb_f, lb_b, s0f=None, s0b=None, layer=0):
    m = z.shape[0]
    t = ROW_BLK
    nblk = m // n_batch // t
    has_state = s0f is not None
    loglb_f, l1m_f = _lb_logs(lb_f)
    loglb_b, l1m_b = _lb_logs(lb_b)
    row = lambda a: a.reshape(1, HG_KW)

    def fmap(col):
        return lambda b, i: (b * nblk + i, col // HG_KW)

    def bmap(col):
        return lambda b, i: (b * nblk + nblk - 1 - i, col // HG_KW)

    blk = lambda imap: pl.BlockSpec((t, HG_KW), imap)
    st_out = pl.BlockSpec((1, HG_HEADS, HG_DK, HG_DV), lambda b, i: (b, 0, 0, 0))
    st_shape = jax.ShapeDtypeStruct((n_batch, HG_HEADS, HG_DK, HG_DV), F32)
    st_in = pl.BlockSpec((1, 1, HG_HEADS, HG_DK, HG_DV), lambda b, i: (b, layer, 0, 0, 0))
    bd_scratch = pltpu.VMEM((2, 4 * HG_DV, 4 * HG_DK), F32)
    return pl.pallas_call(
        functools.partial(_hgrn_kernel, has_state=has_state),
        out_shape=(jax.ShapeDtypeStruct((m, HG_W), F32), jax.ShapeDtypeStruct((m, HG_W), F32), st_shape, st_shape),
        grid=(n_batch, nblk),
        in_specs=[blk(fmap(M_HG_Q)), blk(fmap(M_HG_FF)), blk(fmap(M_HG_I)),
                  blk(bmap(M_HG_Q)), blk(bmap(M_HG_FB)), blk(bmap(M_HG_I)),
                  _const_spec((1, HG_KW)), _const_spec((1, HG_KW)), _const_spec((1, HG_KW)), _const_spec((1, HG_KW))]
                 + ([st_in, st_in] if has_state else []),
        out_specs=(pl.BlockSpec((t, HG_W), lambda b, i: (b * nblk + i, 0)),
                   pl.BlockSpec((t, HG_W), lambda b, i: (b * nblk + nblk - 1 - i, 0)),
                   st_out, st_out),
        scratch_shapes=[bd_scratch, bd_scratch] + [pltpu.VMEM((t, HG_KW), F32)] * 10,
        compiler_params=_params(("arbitrary", "arbitrary")),
        name="hgrn_scan",
    )(z, z, z, z, z, z, row(loglb_f), row(l1m_f), row(loglb_b), row(l1m_b), *((s0f, s0b) if has_state else ()))


def _out_kernel(*refs, has_next):
    (x_ref, mod_ref, gpost_ref, ghg_ref, ona_ref, omla_ref, of_ref, ob_ref,
     gna_ref, gmla_ref, ghgate_ref, mg_ref, wna_ref, wmla_ref, whg_ref, wout_ref) = refs[:16]
    y_ref = refs[-2] if has_next else refs[-1]
    o = of_ref[...] + ob_ref[...]
    hr = lax.broadcasted_iota(jnp.int32, (HG_W, HG_W), 0) // HG_DV
    hc = lax.broadcasted_iota(jnp.int32, (HG_W, HG_W), 1) // HG_DV
    head_mean = jnp.where(hr == hc, 1.0 / HG_DV, 0.0).astype(BF16)
    sq = o * o
    sq_hi = sq.astype(BF16)
    sq_lo = (sq - sq_hi.astype(F32)).astype(BF16)
    ms = _mm(sq_hi, head_mean) + _mm(sq_lo, head_mean)
    o_hg = o * lax.rsqrt(ms + NORM_EPS) * ghg_ref[...]

    def branch(o_b, gate_ref, w_ref):
        return _mm((o_b * _silu(gate_ref[...].astype(F32))).astype(BF16), w_ref[...])

    def merge_gate(i):
        return _sigmoid(mg_ref[:, i * D_MODEL:(i + 1) * D_MODEL].astype(F32))

    merged = (merge_gate(0) * branch(ona_ref[...], gna_ref, wna_ref)
              + merge_gate(1) * branch(omla_ref[...], gmla_ref, wmla_ref)
              + merge_gate(2) * branch(o_hg, ghgate_ref, whg_ref))
    out = _mm(merged.astype(BF16), wout_ref[...])
    y = x_ref[...] + mod_ref[0, 2:3, :] * _rms(out, gpost_ref[...])
    y_ref[...] = y
    if has_next:
        modn_ref, gpren_ref, hn_ref = refs[16], refs[17], refs[-1]
        hn_ref[...] = _modulated_norm(y, gpren_ref[...], modn_ref).astype(BF16)


def _out_proj(x, z_head, z_mid, z_mg, mod3, g_post, g_hg, o_na, o_mla, o_f, o_b, w_na, w_mla, w_hg, w_out,
              rows_per_mod, mod_base, next_mod3=None, next_g_pre=None):
    m = x.shape[0]
    t = ROW_BLK
    tiles_per_mod = rows_per_mod // t
    has_next = next_mod3 is not None
    w512 = lambda: pl.BlockSpec((t, NA_W), lambda i: (i, 0))
    zcol = lambda col, w: pl.BlockSpec((t, w), lambda i: (i, col // w))
    mod_spec = pl.BlockSpec((1, 3, D_MODEL), lambda i: (mod_base + i // tiles_per_mod, 0, 0))
    row_spec = pl.BlockSpec((t, D_MODEL), lambda i: (i, 0))
    one = pl.Buffered(1)
    out = pl.pallas_call(
        functools.partial(_out_kernel, has_next=has_next),
        out_shape=(jax.ShapeDtypeStruct((m, D_MODEL), F32),)
                  + ((jax.ShapeDtypeStruct((m, D_MODEL), BF16),) if has_next else ()),
        grid=(m // t,),
        in_specs=[row_spec, mod_spec,
                  _const_spec((1, D_MODEL)), _const_spec((1, HG_W)),
                  w512(), w512(), w512(), w512(),
                  zcol(C_NA_G, NA_W), zcol(M_MLA_G, MLA_W), zcol(M_HG_G, HG_W),
                  zcol(0, MG_W),
                  pl.BlockSpec((NA_W, D_MODEL), lambda i: (0, 0), pipeline_mode=one),
                  pl.BlockSpec((MLA_W, D_MODEL), lambda i: (0, 0), pipeline_mode=one),
                  pl.BlockSpec((HG_W, D_MODEL), lambda i: (0, 0), pipeline_mode=one),
                  pl.BlockSpec((D_MODEL, D_MODEL), lambda i: (0, 0), pipeline_mode=one)]
                 + ([mod_spec, _const_spec((1, D_MODEL))] if has_next else []),
        out_specs=(row_spec,) + ((row_spec,) if has_next else ()),
        compiler_params=_params(("arbitrary",)),
        name="out_proj",
    )(x, mod3, g_post, g_hg, o_na, o_mla, o_f, o_b, z_head, z_mid, z_mid, z_mg, w_na, w_mla, w_hg, w_out,
      *((next_mod3, next_g_pre) if has_next else ()))
    return out if has_next else (out[0], None)


def kernel(x_prompt, x_sample, cache_na_k, cache_na_v, cache_mla_ckv, cache_mla_krope, state_hgrn_fwd, state_hgrn_bwd, c, c_ctx, w_ada, b_ada, g_pre, g_post, w_in, na_rpb, g_mla_q, w_mla_uq, g_mla_kv, w_mla_ukv, hg_lb_fwd, hg_lb_bwd, g_hg_out, w_br_na, w_br_mla, w_br_hg, w_out):
    bp, sp, _ = x_prompt.shape
    bs, ss, _ = x_sample.shape
    past = cache_na_k.shape[2]

    cvec = jnp.concatenate([c_ctx[None, :], c, jnp.zeros((8 - 1 - bs, D_MODEL), F32)], axis=0)
    mod = _modulation(cvec, w_ada, b_ada).reshape(DEPTH, 8, 3, D_MODEL)

    lb_f_all = _lower_bounds(hg_lb_fwd)
    lb_b_all = _lower_bounds(hg_lb_bwd)
    cos_t, sin_t = _rope_tables(ss)
    cache_k = cache_na_k.reshape(bs, DEPTH, past, NA_W)
    cache_v = cache_na_v.reshape(bs, DEPTH, past, NA_W)
    cache_kr_pad = _rope_lanes(cache_mla_krope)

    yp = x_prompt.reshape(bp * sp, D_MODEL)
    ys = x_sample.reshape(bs * ss, D_MODEL)
    new_k, new_v, new_ckv, new_kr, new_sf, new_sb = [], [], [], [], [], []
    w_head, w_tail = _pack_w_in(w_in)
    hp = _prenorm(yp, mod[0], g_pre[0][None, :], bp * sp, 0)
    hs = _prenorm(ys, mod[0], g_pre[0][None, :], ss, 1)
    for l in range(DEPTH):
        wq, wqp = _pack_w_uq(w_mla_uq[l])
        wuk, wuv = _pack_w_ukv(w_mla_ukv[l])
        w_na, w_mla, w_hg, w_o = (w.astype(BF16) for w in (w_br_na[l], w_br_mla[l], w_br_hg[l], w_out[l]))
        g_q, g_kv = (g_mla_q[l] * (MLA_QK ** -0.5 * LOG2E))[None, :], g_mla_kv[l][None, :]
        gpost, ghg = g_post[l][None, :], g_hg_out[l][None, :]
        mod3 = mod[l]
        nxt = (mod[l + 1], g_pre[l + 1][None, :]) if l + 1 < DEPTH else (None, None)

        zp_head, zp_mid, zp_mg = _in_proj(hp, w_head, w_tail, l)
        o_na, o_mla, ckvn, k_new, v_new, kr_new = _prompt_attn(zp_head, bp, g_q, g_kv, wq, wuk, wuv)
        o_f, o_b, sf, sb = _hgrn(zp_mid, bp, lb_f_all[l], lb_b_all[l])
        yp, hp = _out_proj(yp, zp_head, zp_mid, zp_mg, mod3, gpost, ghg, o_na, o_mla, o_f, o_b,
                           w_na, w_mla, w_hg, w_o, bp * sp, 0, *nxt)
        new_k.append(k_new.reshape(bp, sp, NA_HEADS, NA_HD))
        new_v.append(v_new.reshape(bp, sp, NA_HEADS, NA_HD))
        new_ckv.append(ckvn.reshape(bp, sp, MLA_KV_LORA))
        new_kr.append(kr_new.reshape(bp, sp, MLA_ROPE))
        new_sf.append(sf)
        new_sb.append(sb)

        zs_head, zs_mid, zs_mg = _in_proj(hs, w_head, w_tail, l)
        o_na = _sample_na(zs_head, cache_k, cache_v, _na_bias_table(na_rpb[l]), l, bs)
        kq_all, vp_all = _sample_mla_kv(zs_head, cache_mla_ckv, cache_kr_pad, cos_t, sin_t, g_kv, wuk, wuv, l, bs)
        o_mla = _sample_mla_attn(zs_head, cos_t, sin_t, g_q, wq, wqp, kq_all, vp_all, bs)
        o_f, o_b, _, _ = _hgrn(zs_mid, bs, lb_f_all[l], lb_b_all[l], state_hgrn_fwd, state_hgrn_bwd, l)
        ys, hs = _out_proj(ys, zs_head, zs_mid, zs_mg, mod3, gpost, ghg, o_na, o_mla, o_f, o_b,
                           w_na, w_mla, w_hg, w_o, ss, 1, *nxt)

    return (yp.reshape(bp, sp, D_MODEL), ys.reshape(bs, ss, D_MODEL),
            jnp.stack(new_k, axis=1), jnp.stack(new_v, axis=1), jnp.stack(new_ckv, axis=1),
            jnp.stack(new_kr, axis=1), jnp.stack(new_sf, axis=1), jnp.stack(new_sb, axis=1))
```

```python
import functools

import numpy as np
import jax
import jax.numpy as jnp
from jax import lax
from jax.experimental import pallas as pl
from jax.experimental.pallas import tpu as pltpu

D_MODEL = 2048
DEPTH = 2
GRID_W = 64
NORM_EPS = 1e-6
NEG_BIG = -1e30
NA_HEADS = 8
NA_HD = 64
NA_W = NA_HEADS * NA_HD
NA_KR = 8
NA_KW = 16
MLA_HEADS = 8
MLA_NOPE = 64
MLA_ROPE = 32
MLA_VD = 64
MLA_QK = MLA_NOPE + MLA_ROPE
MLA_W = MLA_HEADS * MLA_VD
MLA_Q_LORA = 512
MLA_KV_LORA = 256
ROPE_BASE = 10000.0
HG_HEADS = 8
HG_DK = 64
HG_DV = 64
HG_KW = HG_HEADS * HG_DK
HG_W = HG_HEADS * HG_DV
N_BRANCH = 3
PROJ_WIDTHS = (NA_W, NA_W, NA_W, NA_W, MLA_Q_LORA, MLA_KV_LORA, MLA_ROPE, MLA_W,
               HG_KW, HG_KW, HG_KW, HG_W, HG_W, N_BRANCH * D_MODEL)

F32 = jnp.float32
BF16 = jnp.bfloat16
HI = lax.Precision.HIGHEST
LOG2E = 1.4426950408889634

LANE = 128
ROW_BLK = 256
HG_CHUNK = 32
HG_SUB = 16
HG_SAFE_DECAY = 72.0
HG_EXP_CLAMP = 80.0
VMEM_LIMIT = 56 * 1024 * 1024

HEAD_W = 3072
MID_W = 3072
MG_W = N_BRANCH * D_MODEL
SPLIT_COL = 2848
C_NA_Q, C_NA_K, C_NA_V, C_NA_G = 0, 512, 1024, 1536
C_MLA_CQ, C_MLA_CKV, C_MLA_KR, C_MLA_KRP = 2048, 2560, 2816, 2944
M_MLA_G, M_HG_Q, M_HG_FF, M_HG_FB, M_HG_I, M_HG_G = 0, 512, 1024, 1536, 2048, 2560


def _nt(a, b, precision=None):
    return lax.dot_general(a, b, (((1,), (1,)), ((), ())), preferred_element_type=F32, precision=precision)


def _tn(a, b, precision=None):
    return lax.dot_general(a, b, (((0,), (0,)), ((), ())), preferred_element_type=F32, precision=precision)


def _mm(a, b, precision=None):
    return jnp.dot(a, b, preferred_element_type=F32, precision=precision)


def _rms(x, g):
    return x * lax.rsqrt(jnp.mean(x * x, axis=-1, keepdims=True) + NORM_EPS) * g


def _sigmoid(x):
    return 0.5 * jnp.tanh(0.5 * x) + 0.5


def _silu(x):
    return x * _sigmoid(x)


def _params(sem):
    return pltpu.CompilerParams(dimension_semantics=sem, vmem_limit_bytes=VMEM_LIMIT)


def _const_spec(shape):
    nd = len(shape)
    return pl.BlockSpec(shape, lambda *_: (0,) * nd)


def _rot_half(a):
    parts = []
    for ax in range(2):
        lo_, hi_ = a[..., ax * 16:ax * 16 + 8], a[..., ax * 16 + 8:ax * 16 + 16]
        parts += [-hi_, lo_]
    return jnp.concatenate(parts, axis=-1)


def _pack_w_in(w):
    assert sum(PROJ_WIDTHS[:7]) == SPLIT_COL and sum(PROJ_WIDTHS[7:13]) == MID_W and PROJ_WIDTHS[13] == MG_W
    head = w[..., :SPLIT_COL].astype(BF16)
    kr = head[..., C_MLA_KR:SPLIT_COL]
    return jnp.concatenate([head[..., :C_MLA_KR], _rope_lanes(kr), _rope_lanes(_rot_half(kr))], axis=-1), \
        w[..., SPLIT_COL:].astype(BF16)


def _rope_lanes(a):
    z = lambda n: jnp.zeros(a.shape[:-1] + (n,), a.dtype)
    return jnp.concatenate([z(MLA_NOPE), a, z(LANE - MLA_QK)], axis=-1)


def _pack_w_uq(w):
    w3 = w.reshape(MLA_Q_LORA, MLA_HEADS, MLA_QK)
    nope, rope = w3[..., :MLA_NOPE], w3[..., MLA_NOPE:]
    pad = jnp.zeros(nope.shape[:-1] + (LANE - MLA_QK,), w.dtype)
    flat = lambda a: a.reshape(MLA_Q_LORA, MLA_HEADS * LANE).astype(BF16)
    return flat(jnp.concatenate([nope, rope, pad], axis=-1)), flat(_rope_lanes(_rot_half(rope)))


def _pack_w_ukv(w):
    w3 = w.reshape(MLA_KV_LORA, MLA_HEADS, MLA_NOPE + MLA_VD)
    pad = jnp.zeros((MLA_KV_LORA, MLA_HEADS, LANE - MLA_NOPE), w.dtype)
    flat = lambda a: a.reshape(MLA_KV_LORA, MLA_HEADS * LANE).astype(BF16)
    return flat(jnp.concatenate([w3[..., :MLA_NOPE], pad], -1)), flat(jnp.concatenate([w3[..., MLA_NOPE:], pad], -1))


def _rope_tables(n):
    t = np.arange(n)
    pos = np.stack([t // GRID_W, t % GRID_W]).astype(np.float32)
    axis_dim = MLA_ROPE // 2
    inv = ROPE_BASE ** (-jnp.arange(0, axis_dim, 2, dtype=F32) / axis_dim)
    ang = jnp.asarray(pos)[:, :, None] * inv
    ang = jnp.concatenate([ang, ang], axis=-1)
    ang = jnp.concatenate([ang[0], ang[1]], axis=-1)
    cos_t = jnp.concatenate([jnp.ones((n, MLA_NOPE), F32), jnp.cos(ang), jnp.zeros((n, LANE - MLA_QK), F32)], -1)
    return cos_t, _rope_lanes(jnp.sin(ang))


def _na_bias_table(rpb):
    rows = 32
    c = np.arange(GRID_W)
    win0 = np.clip(c - NA_KW // 2, 0, GRID_W - NA_KW)
    kc = np.arange(GRID_W)
    col_ok = (kc[None, :] >= win0[:, None]) & (kc[None, :] < win0[:, None] + NA_KW)
    rpb = rpb.astype(F32)
    edge = GRID_W - NA_KW
    ext = jnp.concatenate([jnp.broadcast_to(rpb[..., :1], rpb.shape[:-1] + (edge,)), rpb,
                           jnp.broadcast_to(rpb[..., -1:], rpb.shape[:-1] + (edge,))], axis=-1)
    toep = jnp.stack([ext[..., GRID_W - 1 - ci:2 * GRID_W - 1 - ci] for ci in range(GRID_W)], axis=2)
    toep = jnp.where(jnp.asarray(col_ok), toep * LOG2E, NEG_BIG)
    masked = jnp.full((NA_HEADS, GRID_W, GRID_W), NEG_BIG, F32)
    out = []
    for start, r0 in ((0, 0), (4, 8), (20, 28)):
        per_q = []
        for qr in range(4):
            r = r0 + qr
            kr0 = min(max(r - NA_KR // 2, 0), rows - NA_KR)
            tiles = []
            for j in range(12):
                kabs = start + j
                ok = kr0 <= kabs < kr0 + NA_KR
                tiles.append(toep[:, kabs - r + NA_KR - 1] if ok else masked)
            per_q.append(jnp.concatenate(tiles, axis=-1))
        out.append(jnp.concatenate(per_q, axis=1))
    return jnp.stack(out)


def _lower_bounds(p):
    s = jax.nn.softmax(p.astype(F32), axis=0)
    return jnp.cumsum(s, axis=0) - s[0]


def _lb_logs(lb):
    pos = lb > 0
    log_lb = jnp.where(pos, jnp.log(jnp.where(pos, lb, 1.0)), NEG_BIG)
    return log_lb, jnp.log1p(-lb)


def _mod_kernel(c_ref, w_ref, b_ref, o_ref):
    s = _silu(c_ref[...])
    o_ref[0] = _mm(s, w_ref[0], HI) + b_ref[0]


def _modulation(cvec, w_ada, b_ada):
    tn = 1024
    n3 = 3 * D_MODEL
    return pl.pallas_call(
        _mod_kernel,
        out_shape=jax.ShapeDtypeStruct((DEPTH, 8, n3), F32),
        grid=(DEPTH, n3 // tn),
        in_specs=[pl.BlockSpec((8, D_MODEL), lambda l, j: (0, 0)),
                  pl.BlockSpec((1, D_MODEL, tn), lambda l, j: (l, 0, j)),
                  pl.BlockSpec((1, 1, tn), lambda l, j: (l, 0, j))],
        out_specs=pl.BlockSpec((1, 8, tn), lambda l, j: (l, 0, j)),
        compiler_params=_params(("arbitrary", "arbitrary")),
        name="adaln_mod",
    )(cvec, w_ada, b_ada.reshape(DEPTH, 1, n3))


IN_TM, IN_TN = 2048, 1024


def _modulated_norm(x, g, mod_ref):
    return _rms(x, g) * (1.0 + mod_ref[0, 1:2, :]) + mod_ref[0, 0:1, :]


def _prenorm_kernel(x_ref, mod_ref, g_ref, h_ref):
    h_ref[...] = _modulated_norm(x_ref[...], g_ref[...], mod_ref).astype(BF16)


def _prenorm(x, mod3, g_pre, rows_per_mod, mod_base):
    m = x.shape[0]
    t = ROW_BLK
    tiles_per_mod = rows_per_mod // t
    return pl.pallas_call(
        _prenorm_kernel,
        out_shape=jax.ShapeDtypeStruct((m, D_MODEL), BF16),
        grid=(m // t,),
        in_specs=[pl.BlockSpec((t, D_MODEL), lambda i: (i, 0)),
                  pl.BlockSpec((1, 3, D_MODEL), lambda i: (mod_base + i // tiles_per_mod, 0, 0)),
                  _const_spec((1, D_MODEL))],
        out_specs=pl.BlockSpec((t, D_MODEL), lambda i: (i, 0)),
        compiler_params=_params(("arbitrary",)),
        name="prenorm",
    )(x, mod3, g_pre)


def _proj_kernel(h_ref, w_ref, z_ref):
    z_ref[...] = _mm(h_ref[...], w_ref[0]).astype(z_ref.dtype)


def _proj(h, w, layer, tile0, width, out_dtype):
    m = h.shape[0]
    tm, tn = IN_TM, IN_TN
    return pl.pallas_call(
        _proj_kernel,
        out_shape=jax.ShapeDtypeStruct((m, width), out_dtype),
        grid=(m // tm, width // tn),
        in_specs=[pl.BlockSpec((tm, D_MODEL), lambda i, j: (i, 0)),
                  pl.BlockSpec((1, D_MODEL, tn), lambda i, j: (layer, 0, tile0 + j))],
        out_specs=pl.BlockSpec((tm, tn), lambda i, j: (i, j)),
        compiler_params=_params(("arbitrary", "arbitrary")),
        name="in_proj",
    )(h, w)


def _in_proj(h, w_head, w_tail, layer):
    z_head = _proj(h, w_head, layer, 0, HEAD_W, F32)
    z_mid = _proj(h, w_tail, layer, 0, MID_W, BF16)
    z_mg = _proj(h, w_tail, layer, MID_W // IN_TN, MG_W, BF16)
    return z_head, z_mid, z_mg


def _softmax_pv(s_list, v_list):
    m = s_list[0].max(axis=-1, keepdims=True)
    for s in s_list[1:]:
        m = jnp.maximum(m, s.max(axis=-1, keepdims=True))
    acc, den = None, None
    for s, v in zip(s_list, v_list):
        e = jnp.exp2(s - m)
        d = e.sum(axis=-1, keepdims=True)
        o = _mm(e.astype(BF16), v)
        acc = o if acc is None else acc + o
        den = d if den is None else den + d
    return acc, den


def _na_head_pairs(q, k_fn, v_fn, scores_fn):
    low = lax.broadcasted_iota(jnp.int32, (1, LANE), 1) < NA_HD
    tq = q.shape[0]
    outs = []
    for p in range(NA_HEADS // 2):
        sl = slice(p * LANE, (p + 1) * LANE)
        q2, k_pair, v_pair = q[:, sl], k_fn(sl), v_fn(sl)
        q_stack = jnp.concatenate([jnp.where(low, q2, 0.0), jnp.where(low, 0.0, q2)], axis=0).astype(BF16)
        acc, den = _softmax_pv(scores_fn(q_stack, k_pair, p), v_pair)
        o = acc / den
        outs.append(jnp.where(low, o[:tq], o[tq:]))
    return jnp.concatenate(outs, axis=-1)


def _ones_lane():
    return (lax.broadcasted_iota(jnp.int32, (1, LANE), 1) == MLA_VD).astype(F32)


def _mla_heads(q_fn, kq, vp, o_ref):
    outs = []
    for h in range(MLA_HEADS):
        sl = slice(h * LANE, (h + 1) * LANE)
        s = _nt(q_fn(sl), kq[:, sl])
        e = jnp.exp2(s - s.max(axis=-1, keepdims=True))
        acc = _mm(e.astype(BF16), vp[:, sl])
        outs.append(acc[:, :MLA_VD] / acc[:, MLA_VD:MLA_VD + 1])
    o_ref[...] = jnp.concatenate(outs, axis=-1)


def _mla_keys_values(ckvn16, kr_tile, wuk_ref, wuv_ref):
    kq = _mm(ckvn16, wuk_ref[...]) + jnp.concatenate([kr_tile] * MLA_HEADS, axis=-1)
    vp = _mm(ckvn16, wuv_ref[...]) + jnp.concatenate([_ones_lane()] * MLA_HEADS, axis=-1)
    return kq.astype(BF16), vp.astype(BF16)


def _prompt_attn_kernel(qkv_ref, cq_ref, ckv_ref, kr_ref, gq_ref, gkv_ref, wq_ref, wuk_ref, wuv_ref,
                        ona_ref, omla_ref, ckvn_ref, k_ref, v_ref, kro_ref):
    qkv = qkv_ref[...]
    k_ref[...] = qkv[:, NA_W:2 * NA_W]
    v_ref[...] = qkv[:, 2 * NA_W:3 * NA_W]
    kro_ref[...] = kr_ref[:, MLA_NOPE:MLA_QK]
    ckvn = _rms(ckv_ref[...], gkv_ref[...])
    ckvn_ref[...] = ckvn
    scale = NA_HD ** -0.5 * LOG2E
    outs = []
    for h in range(NA_HEADS):
        qh = (qkv[:, h * NA_HD:(h + 1) * NA_HD] * scale).astype(BF16)
        kh = qkv[:, NA_W + h * NA_HD:NA_W + (h + 1) * NA_HD].astype(BF16)
        vh = qkv[:, 2 * NA_W + h * NA_HD:2 * NA_W + (h + 1) * NA_HD].astype(BF16)
        acc, den = _softmax_pv([_nt(qh, kh)], [vh])
        outs.append(acc / den)
    ona_ref[...] = jnp.concatenate(outs, axis=-1)

    cqn = _rms(cq_ref[...], gq_ref[...]).astype(BF16)
    q = _mm(cqn, wq_ref[...]).astype(BF16)
    kq, vp = _mla_keys_values(ckvn.astype(BF16), kr_ref[...], wuk_ref, wuv_ref)
    _mla_heads(lambda sl: q[:, sl], kq, vp, omla_ref)


def _prompt_attn(z, n_batch, g_q, g_kv, wq, wuk, wuv):
    m = z.shape[0]
    t = ROW_BLK
    hw = MLA_HEADS * LANE
    return pl.pallas_call(
        _prompt_attn_kernel,
        out_shape=(jax.ShapeDtypeStruct((m, NA_W), F32), jax.ShapeDtypeStruct((m, MLA_W), F32),
                   jax.ShapeDtypeStruct((m, MLA_KV_LORA), F32), jax.ShapeDtypeStruct((m, NA_W), F32),
                   jax.ShapeDtypeStruct((m, NA_W), F32), jax.ShapeDtypeStruct((m, MLA_ROPE), F32)),
        grid=(n_batch,),
        in_specs=[pl.BlockSpec((t, 3 * NA_W), lambda b: (b, C_NA_Q // (3 * NA_W))),
                  pl.BlockSpec((t, MLA_Q_LORA), lambda b: (b, C_MLA_CQ // MLA_Q_LORA)),
                  pl.BlockSpec((t, MLA_KV_LORA), lambda b: (b, C_MLA_CKV // MLA_KV_LORA)),
                  pl.BlockSpec((t, LANE), lambda b: (b, C_MLA_KR // LANE)),
                  _const_spec((1, MLA_Q_LORA)), _const_spec((1, MLA_KV_LORA)),
                  _const_spec((MLA_Q_LORA, hw)), _const_spec((MLA_KV_LORA, hw)), _const_spec((MLA_KV_LORA, hw))],
        out_specs=(pl.BlockSpec((t, NA_W), lambda b: (b, 0)), pl.BlockSpec((t, MLA_W), lambda b: (b, 0)),
                   pl.BlockSpec((t, MLA_KV_LORA), lambda b: (b, 0)), pl.BlockSpec((t, NA_W), lambda b: (b, 0)),
                   pl.BlockSpec((t, NA_W), lambda b: (b, 0)), pl.BlockSpec((t, MLA_ROPE), lambda b: (b, 0))),
        compiler_params=_params(("arbitrary",)),
        name="prompt_attn",
    )(z, z, z, z, g_q, g_kv, wq, wuk, wuv)


def _sample_na_kernel(q_ref, k0_ref, k1_ref, k2_ref, v0_ref, v1_ref, v2_ref, kc_ref, vc_ref, bias_ref, o_ref):
    q = q_ref[...] * (NA_HD ** -0.5 * LOG2E)
    kl = [r[...] for r in (k0_ref, k1_ref, k2_ref)]
    vl = [r[...] for r in (v0_ref, v1_ref, v2_ref)]
    kc, vc = kc_ref[0, 0], vc_ref[0, 0]

    def scores(q_stack, k_pair, pair):
        def bias(i):
            b = bias_ref[0, 2 * pair:2 * pair + 2, :, i * ROW_BLK:(i + 1) * ROW_BLK]
            return b.reshape(2 * ROW_BLK, ROW_BLK)

        return [_nt(q_stack, k_pair[i]) + bias(i) for i in range(3)] + [_nt(q_stack, k_pair[3])]

    o_ref[...] = _na_head_pairs(q, lambda sl: [a[:, sl].astype(BF16) for a in kl + [kc]],
                                lambda sl: [a[:, sl].astype(BF16) for a in vl + [vc]], scores)


def _sample_na(z, cache_k, cache_v, bias_tab, layer, n_batch):
    m = z.shape[0]
    t = ROW_BLK
    nblk = m // n_batch // t
    past = cache_k.shape[2]

    def kv_map(col, i):
        return lambda b, rb: (b * nblk + jnp.clip(rb - 1, 0, nblk - 3) + i, col // NA_W)

    def variant(b, rb):
        return (jnp.where(rb == 0, 0, jnp.where(rb == nblk - 1, 2, 1)), 0, 0, 0)

    cache_spec = pl.BlockSpec((1, 1, past, NA_W), lambda b, rb: (b, layer, 0, 0))
    return pl.pallas_call(
        _sample_na_kernel,
        out_shape=jax.ShapeDtypeStruct((m, NA_W), F32),
        grid=(n_batch, nblk),
        in_specs=[pl.BlockSpec((t, NA_W), lambda b, rb: (b * nblk + rb, C_NA_Q // NA_W))]
                 + [pl.BlockSpec((t, NA_W), kv_map(C_NA_K, i)) for i in range(3)]
                 + [pl.BlockSpec((t, NA_W), kv_map(C_NA_V, i)) for i in range(3)]
                 + [cache_spec, cache_spec,
                    pl.BlockSpec((1, NA_HEADS, t, 3 * t), variant)],
        out_specs=pl.BlockSpec((t, NA_W), lambda b, rb: (b * nblk + rb, 0)),
        compiler_params=_params(("arbitrary", "arbitrary")),
        name="sample_na",
    )(z, z, z, z, z, z, z, cache_k, cache_v, bias_tab)


def _sample_mla_kv_kernel(cckv_ref, ckr_ref, ckv_ref, kr_ref, krp_ref, cos_ref, sin_ref, gkv_ref, wuk_ref, wuv_ref,
                          kq_ref, vp_ref):
    j = pl.program_id(1)

    @pl.when(j == 0)
    def _():
        kq_ref[0], vp_ref[0] = _mla_keys_values(cckv_ref[0, 0].astype(BF16), ckr_ref[0, 0], wuk_ref, wuv_ref)

    @pl.when(j > 0)
    def _():
        ckvn = _rms(ckv_ref[...], gkv_ref[...])
        kr_roped = kr_ref[...] * cos_ref[...] + krp_ref[...] * sin_ref[...]
        kq_ref[0], vp_ref[0] = _mla_keys_values(ckvn.astype(BF16), kr_roped, wuk_ref, wuv_ref)


def _sample_mla_kv(z, cache_ckv, cache_kr_pad, cos_t, sin_t, g_kv, wuk, wuv, layer, n_batch):
    m = z.shape[0]
    n = m // n_batch
    past = cache_ckv.shape[2]
    t = past
    nb = n // t
    hw = MLA_HEADS * LANE

    def zrow(b, j):
        return b * nb + jnp.maximum(j - 1, 0)

    return pl.pallas_call(
        _sample_mla_kv_kernel,
        out_shape=(jax.ShapeDtypeStruct((n_batch, past + n, hw), BF16),
                   jax.ShapeDtypeStruct((n_batch, past + n, hw), BF16)),
        grid=(n_batch, nb + 1),
        in_specs=[pl.BlockSpec((1, 1, past, MLA_KV_LORA), lambda b, j: (b, layer, 0, 0)),
                  pl.BlockSpec((1, 1, past, LANE), lambda b, j: (b, layer, 0, 0)),
                  pl.BlockSpec((t, MLA_KV_LORA), lambda b, j: (zrow(b, j), C_MLA_CKV // MLA_KV_LORA)),
                  pl.BlockSpec((t, LANE), lambda b, j: (zrow(b, j), C_MLA_KR // LANE)),
                  pl.BlockSpec((t, LANE), lambda b, j: (zrow(b, j), C_MLA_KRP // LANE)),
                  pl.BlockSpec((t, LANE), lambda b, j: (jnp.maximum(j - 1, 0), 0)),
                  pl.BlockSpec((t, LANE), lambda b, j: (jnp.maximum(j - 1, 0), 0)),
                  _const_spec((1, MLA_KV_LORA)), _const_spec((MLA_KV_LORA, hw)), _const_spec((MLA_KV_LORA, hw))],
        out_specs=(pl.BlockSpec((1, t, hw), lambda b, j: (b, j, 0)),
                   pl.BlockSpec((1, t, hw), lambda b, j: (b, j, 0))),
        compiler_params=_params(("arbitrary", "arbitrary")),
        name="sample_mla_kv",
    )(cache_ckv, cache_kr_pad, z, z, z, cos_t, sin_t, g_kv, wuk, wuv)


def _sample_mla_attn_kernel(cq_ref, cos_ref, sin_ref, gq_ref, wq_ref, wqp_ref, kq_ref, vp_ref, o_ref):
    cqn = _rms(cq_ref[...], gq_ref[...]).astype(BF16)
    q = _mm(cqn, wq_ref[...])
    q_rot = _mm(cqn, wqp_ref[...])
    cos, sin = cos_ref[...], sin_ref[...]
    _mla_heads(lambda sl: (q[:, sl] * cos + q_rot[:, sl] * sin).astype(BF16), kq_ref[0], vp_ref[0], o_ref)


def _sample_mla_attn(z, cos_t, sin_t, g_q, wq, wqp, kq_all, vp_all, n_batch):
    m = z.shape[0]
    t = 2 * ROW_BLK
    nblk = m // n_batch // t
    nk = kq_all.shape[1]
    hw = MLA_HEADS * LANE
    return pl.pallas_call(
        _sample_mla_attn_kernel,
        out_shape=jax.ShapeDtypeStruct((m, MLA_W), F32),
        grid=(n_batch, nblk),
        in_specs=[pl.BlockSpec((t, MLA_Q_LORA), lambda b, i: (b * nblk + i, C_MLA_CQ // MLA_Q_LORA)),
                  pl.BlockSpec((t, LANE), lambda b, i: (i, 0)),
                  pl.BlockSpec((t, LANE), lambda b, i: (i, 0)),
                  _const_spec((1, MLA_Q_LORA)),
                  _const_spec((MLA_Q_LORA, hw)), _const_spec((MLA_Q_LORA, hw)),
                  pl.BlockSpec((1, nk, hw), lambda b, i: (b, 0, 0)),
                  pl.BlockSpec((1, nk, hw), lambda b, i: (b, 0, 0))],
        out_specs=pl.BlockSpec((t, MLA_W), lambda b, i: (b * nblk + i, 0)),
        compiler_params=_params(("arbitrary", "arbitrary")),
        name="sample_mla_attn",
    )(z, cos_t, sin_t, g_q, wq, wqp, kq_all, vp_all)


def _hg_direction(fwd, q_ref, zf_ref, v_ref, loglb_ref, l1m_ref, s_ref, o_ref, qs, ks, bs, vs, os):
    t = ROW_BLK
    nchunk = t // HG_CHUNK
    q = _silu(q_ref[...].astype(F32))
    zf = zf_ref[...].astype(F32)
    v = v_ref[...].astype(F32)
    soft = jnp.log(1.0 + jnp.exp(-jnp.abs(zf)))
    a1 = jnp.broadcast_to(loglb_ref[...], zf.shape)
    a2 = l1m_ref[...] + (jnp.minimum(zf, 0.0) - soft)
    logf = jnp.maximum(a1, a2) + jnp.log(1.0 + jnp.exp(-jnp.abs(a1 - a2)))
    k = jnp.exp(l1m_ref[...] - jnp.maximum(zf, 0.0) - soft)

    def chunk_masks(n):
        ri = lax.broadcasted_iota(jnp.int32, (n, n), 0)
        ci = lax.broadcasted_iota(jnp.int32, (n, n), 1)
        causal = (ri >= ci) if fwd else (ri <= ci)
        same_sub = (ri // HG_SUB) == (ci // HG_SUB)
        same = (ri // HG_CHUNK) == (ci // HG_CHUNK)
        return same_sub & causal, same & jnp.logical_not(same_sub) & causal, same_sub, same

    tri_sub, _, same_sub, same = chunk_masks(t)
    hi = logf.astype(BF16)
    lo = (logf - hi.astype(F32)).astype(BF16)

    def seg_sum(mask):
        m16 = jnp.where(mask, 1.0, 0.0).astype(BF16)
        return _mm(m16, hi) + _mm(m16, lo)

    b_sub = seg_sum(tri_sub)
    nsub = t // HG_SUB
    b3 = b_sub.reshape(nsub, HG_SUB, HG_KW)
    edge = b3[:, HG_SUB - 1:HG_SUB, :] if fwd else b3[:, 0:1, :]
    tot_sub = jnp.broadcast_to(edge, (nsub, HG_SUB, HG_KW)).reshape(t, HG_KW)
    pair = edge.reshape(nchunk, 2, 1, HG_KW)
    tot = jnp.broadcast_to(pair[:, 0:1] + pair[:, 1:2], (nchunk, 2, HG_SUB, HG_KW)).reshape(t, HG_KW)
    row = lax.broadcasted_iota(jnp.int32, (t, 1), 0) % HG_CHUNK
    later = (row >= HG_SUB) if fwd else (row < HG_SUB)
    b = b_sub + jnp.where(later, tot - tot_sub, 0.0)

    qt_sub = (q * jnp.exp(b_sub)).astype(BF16)
    kt_sub = (k * jnp.exp(jnp.minimum(-b_sub, HG_EXP_CLAMP))).astype(BF16)
    kh_sub = (k * jnp.exp(tot_sub - b_sub)).astype(BF16)
    qt16 = (q * jnp.exp(b)).astype(BF16)
    kh16 = (k * jnp.exp(tot - b)).astype(BF16)
    v16 = v.astype(BF16)

    hb = t // 2
    tri_hb, cross_hb, _, _ = chunk_masks(hb)
    low = lax.broadcasted_iota(jnp.int32, (1, LANE), 1) < HG_DK
    zero16 = jnp.zeros((), BF16)
    outs = []
    for p2 in range(HG_HEADS // 2):
        sl = slice(p2 * LANE, (p2 + 1) * LANE)
        parts = []
        for r0 in (0, hb):
            rs = slice(r0, r0 + hb)
            keys = jnp.concatenate([kt_sub[rs, sl], kh_sub[rs, sl]], axis=0)
            halves = []
            for keep in (low, jnp.logical_not(low)):
                p = _nt(jnp.where(keep, qt_sub[rs, sl], zero16), keys)
                a = jnp.where(tri_hb, p[:, :hb], 0.0) + jnp.where(cross_hb, p[:, hb:], 0.0)
                halves.append(_mm(a.astype(BF16), v16[rs, sl]))
            parts.append(jnp.where(low, halves[0], halves[1]))
        outs.append(jnp.concatenate(parts, axis=0))
    o_ref[...] = jnp.concatenate(outs, axis=-1)

    gw = 4 * HG_DK
    bd_r = lax.broadcasted_iota(jnp.int32, (gw, gw), 0) // HG_DV
    bd_c = lax.broadcasted_iota(jnp.int32, (gw, gw), 1) // HG_DK
    bd = bd_r == bd_c
    order = range(nchunk) if fwd else range(nchunk - 1, -1, -1)
    for g in range(2):
        ls = slice(g * gw, (g + 1) * gw)
        rows = {c: slice(c * HG_CHUNK, (c + 1) * HG_CHUNK) for c in order}
        updates = {c: jnp.where(bd, _tn(v16[rows[c], ls], kh16[rows[c], ls]), 0.0) for c in order}
        s = s_ref[g]
        entering = {}
        for c in order:
            entering[c] = s.astype(BF16)
            s = s * jnp.exp(tot[c * HG_CHUNK:c * HG_CHUNK + 1, ls]) + updates[c]
        s_ref[g] = s
        for c in order:
            inter = _nt(qt16[rows[c], ls], entering[c])
            o_ref[rows[c], ls] += inter
            os[rows[c], ls] = inter

    qs[...] = q
    ks[...] = k
    bs[...] = b
    vs[...] = v
    return [jnp.max(-jnp.minimum(tot_sub[c * HG_CHUNK:c * HG_CHUNK + 1, :],
                                 tot_sub[c * HG_CHUNK + HG_SUB:c * HG_CHUNK + HG_SUB + 1, :]))
            for c in range(nchunk)]


def _hg_pairwise(fwd, worst, o_ref, qs, ks, bs, vs, os):
    lane = lax.broadcasted_iota(jnp.int32, (HG_KW, LANE), 0) // HG_DK
    col = lax.broadcasted_iota(jnp.int32, (HG_KW, LANE), 1)
    head_sum = (lane == col).astype(F32)
    srow = lax.broadcasted_iota(jnp.int32, (HG_CHUNK, 1), 0)
    for c, worst_c in enumerate(worst):
        r0 = c * HG_CHUNK

        @pl.when(worst_c > HG_SAFE_DECAY)
        def _():
            kc = ks[r0:r0 + HG_CHUNK, :]
            bc = bs[r0:r0 + HG_CHUNK, :]
            vc = vs[r0:r0 + HG_CHUNK, :]

            def body(i, carry):
                qrow = qs[pl.ds(r0 + i, 1), :]
                brow = bs[pl.ds(r0 + i, 1), :]
                p = qrow * kc * jnp.exp(jnp.minimum(brow - bc, 0.0))
                keep = (srow <= i) if fwd else (srow >= i)
                p = jnp.where(keep, p, 0.0)
                a = _mm(p, head_sum, HI)
                a_full = _nt(a, head_sum, HI)
                o_ref[pl.ds(r0 + i, 1), :] = (jnp.sum(a_full * vc, axis=0, keepdims=True)
                                              + os[pl.ds(r0 + i, 1), :])
                return carry

            lax.fori_loop(0, HG_CHUNK, body, 0)


def _head_block(h):
    g, hh = divmod(h, 4)
    return g, slice(hh * HG_DV, (hh + 1) * HG_DV), slice(hh * HG_DK, (hh + 1) * HG_DK)


def _hgrn_kernel(*refs, has_state):
    (qf_ref, ff_ref, vf_ref, qb_ref, fb_ref, vb_ref, lbf_ref, l1f_ref, lbb_ref, l1b_ref) = refs[:10]
    s0_refs = refs[10:12] if has_state else (None, None)
    rest = refs[12:] if has_state else refs[10:]
    of_ref, ob_ref, sf_out, sb_out, sf, sb = rest[:6]
    scr_f, scr_b = rest[6:11], rest[11:16]
    i = pl.program_id(1)

    @pl.when(i == 0)
    def _():
        for scr, s0_ref in zip((sf, sb), s0_refs):
            scr[...] = jnp.zeros(scr.shape, F32)
            if s0_ref is not None:
                for h in range(HG_HEADS):
                    g, rv, ck = _head_block(h)
                    scr[g, rv, ck] = s0_ref[0, 0, h].T

    worst_f = _hg_direction(True, qf_ref, ff_ref, vf_ref, lbf_ref, l1f_ref, sf, of_ref, *scr_f)
    worst_b = _hg_direction(False, qb_ref, fb_ref, vb_ref, lbb_ref, l1b_ref, sb, ob_ref, *scr_b)

    @pl.when(functools.reduce(jnp.maximum, worst_f + worst_b) > HG_SAFE_DECAY)
    def _():
        _hg_pairwise(True, worst_f, of_ref, *scr_f)
        _hg_pairwise(False, worst_b, ob_ref, *scr_b)

    @pl.when(i == pl.num_programs(1) - 1)
    def _():
        for scr, out in ((sf, sf_out), (sb, sb_out)):
            for h in range(HG_HEADS):
                g, rv, ck = _head_block(h)
                out[0, h] = scr[g, rv, ck].T


def _hgrn(z, n_batch, lb_f, lb_b, s0f=None, s0b=None, layer=0):
    m = z.shape[0]
    t = ROW_BLK
    nblk = m // n_batch // t
    has_state = s0f is not None
    loglb_f, l1m_f = _lb_logs(lb_f)
    loglb_b, l1m_b = _lb_logs(lb_b)
    row = lambda a: a.reshape(1, HG_KW)

    def fmap(col):
        return lambda b, i: (b * nblk + i, col // HG_KW)

    def bmap(col):
        return lambda b, i: (b * nblk + nblk - 1 - i, col // HG_KW)

    blk = lambda imap: pl.BlockSpec((t, HG_KW), imap)
    st_out = pl.BlockSpec((1, HG_HEADS, HG_DK, HG_DV), lambda b, i: (b, 0, 0, 0))
    st_shape = jax.ShapeDtypeStruct((n_batch, HG_HEADS, HG_DK, HG_DV), F32)
    st_in = pl.BlockSpec((1, 1, HG_HEADS, HG_DK, HG_DV), lambda b, i: (b, layer, 0, 0, 0))
    bd_scratch = pltpu.VMEM((2, 4 * HG_DV, 4 * HG_DK), F32)
    return pl.pallas_call(
        functools.partial(_hgrn_kernel, has_state=has_state),
        out_shape=(jax.ShapeDtypeStruct((m, HG_W), F32), jax.ShapeDtypeStruct((m, HG_W), F32), st_shape, st_shape),
        grid=(n_batch, nblk),
        in_specs=[blk(fmap(M_HG_Q)), blk(fmap(M_HG_FF)), blk(fmap(M_HG_I)),
                  blk(bmap(M_HG_Q)), blk(bmap(M_HG_FB)), blk(bmap(M_HG_I)),
                  _const_spec((1, HG_KW)), _const_spec((1, HG_KW)), _const_spec((1, HG_KW)), _const_spec((1, HG_KW))]
                 + ([st_in, st_in] if has_state else []),
        out_specs=(pl.BlockSpec((t, HG_W), lambda b, i: (b * nblk + i, 0)),
                   pl.BlockSpec((t, HG_W), lambda b, i: (b * nblk + nblk - 1 - i, 0)),
                   st_out, st_out),
        scratch_shapes=[bd_scratch, bd_scratch] + [pltpu.VMEM((t, HG_KW), F32)] * 10,
        compiler_params=_params(("arbitrary", "arbitrary")),
        name="hgrn_scan",
    )(z, z, z, z, z, z, row(loglb_f), row(l1m_f), row(loglb_b), row(l1m_b), *((s0f, s0b) if has_state else ()))


def _out_kernel(*refs, has_next):
    (x_ref, mod_ref, gpost_ref, ghg_ref, ona_ref, omla_ref, of_ref, ob_ref,
     gna_ref, gmla_ref, ghgate_ref, mg_ref, wna_ref, wmla_ref, whg_ref, wout_ref) = refs[:16]
    y_ref = refs[-2] if has_next else refs[-1]
    o = of_ref[...] + ob_ref[...]
    hr = lax.broadcasted_iota(jnp.int32, (HG_W, HG_W), 0) // HG_DV
    hc = lax.broadcasted_iota(jnp.int32, (HG_W, HG_W), 1) // HG_DV
    head_mean = jnp.where(hr == hc, 1.0 / HG_DV, 0.0).astype(BF16)
    sq = o * o
    sq_hi = sq.astype(BF16)
    sq_lo = (sq - sq_hi.astype(F32)).astype(BF16)
    ms = _mm(sq_hi, head_mean) + _mm(sq_lo, head_mean)
    o_hg = o * lax.rsqrt(ms + NORM_EPS) * ghg_ref[...]

    def branch(o_b, gate_ref, w_ref):
        return _mm((o_b * _silu(gate_ref[...].astype(F32))).astype(BF16), w_ref[...])

    def merge_gate(i):
        return _sigmoid(mg_ref[:, i * D_MODEL:(i + 1) * D_MODEL].astype(F32))

    merged = (merge_gate(0) * branch(ona_ref[...], gna_ref, wna_ref)
              + merge_gate(1) * branch(omla_ref[...], gmla_ref, wmla_ref)
              + merge_gate(2) * branch(o_hg, ghgate_ref, whg_ref))
    out = _mm(merged.astype(BF16), wout_ref[...])
    y = x_ref[...] + mod_ref[0, 2:3, :] * _rms(out, gpost_ref[...])
    y_ref[...] = y
    if has_next:
        modn_ref, gpren_ref, hn_ref = refs[16], refs[17], refs[-1]
        hn_ref[...] = _modulated_norm(y, gpren_ref[...], modn_ref).astype(BF16)


def _out_proj(x, z_head, z_mid, z_mg, mod3, g_post, g_hg, o_na, o_mla, o_f, o_b, w_na, w_mla, w_hg, w_out,
              rows_per_mod, mod_base, next_mod3=None, next_g_pre=None):
    m = x.shape[0]
    t = ROW_BLK
    tiles_per_mod = rows_per_mod // t
    has_next = next_mod3 is not None
    w512 = lambda: pl.BlockSpec((t, NA_W), lambda i: (i, 0))
    zcol = lambda col, w: pl.BlockSpec((t, w), lambda i: (i, col // w))
    mod_spec = pl.BlockSpec((1, 3, D_MODEL), lambda i: (mod_base + i // tiles_per_mod, 0, 0))
    row_spec = pl.BlockSpec((t, D_MODEL), lambda i: (i, 0))
    one = pl.Buffered(1)
    out = pl.pallas_call(
        functools.partial(_out_kernel, has_next=has_next),
        out_shape=(jax.ShapeDtypeStruct((m, D_MODEL), F32),)
                  + ((jax.ShapeDtypeStruct((m, D_MODEL), BF16),) if has_next else ()),
        grid=(m // t,),
        in_specs=[row_spec, mod_spec,
                  _const_spec((1, D_MODEL)), _const_spec((1, HG_W)),
                  w512(), w512(), w512(), w512(),
                  zcol(C_NA_G, NA_W), zcol(M_MLA_G, MLA_W), zcol(M_HG_G, HG_W),
                  zcol(0, MG_W),
                  pl.BlockSpec((NA_W, D_MODEL), lambda i: (0, 0), pipeline_mode=one),
                  pl.BlockSpec((MLA_W, D_MODEL), lambda i: (0, 0), pipeline_mode=one),
                  pl.BlockSpec((HG_W, D_MODEL), lambda i: (0, 0), pipeline_mode=one),
                  pl.BlockSpec((D_MODEL, D_MODEL), lambda i: (0, 0), pipeline_mode=one)]
                 + ([mod_spec, _const_spec((1, D_MODEL))] if has_next else []),
        out_specs=(row_spec,) + ((row_spec,) if has_next else ()),
        compiler_params=_params(("arbitrary",)),
        name="out_proj",
    )(x, mod3, g_post, g_hg, o_na, o_mla, o_f, o_b, z_head, z_mid, z_mid, z_mg, w_na, w_mla, w_hg, w_out,
      *((next_mod3, next_g_pre) if has_next else ()))
    return out if has_next else (out[0], None)


def kernel(x_prompt, x_sample, cache_na_k, cache_na_v, cache_mla_ckv, cache_mla_krope, state_hgrn_fwd, state_hgrn_bwd, c, c_ctx, w_ada, b_ada, g_pre, g_post, w_in, na_rpb, g_mla_q, w_mla_uq, g_mla_kv, w_mla_ukv, hg_lb_fwd, hg_lb_bwd, g_hg_out, w_br_na, w_br_mla, w_br_hg, w_out):
    bp, sp, _ = x_prompt.shape
    bs, ss, _ = x_sample.shape
    past = cache_na_k.shape[2]

    cvec = jnp.concatenate([c_ctx[None, :], c, jnp.zeros((8 - 1 - bs, D_MODEL), F32)], axis=0)
    mod = _modulation(cvec, w_ada, b_ada).reshape(DEPTH, 8, 3, D_MODEL)

    lb_f_all = _lower_bounds(hg_lb_fwd)
    lb_b_all = _lower_bounds(hg_lb_bwd)
    cos_t, sin_t = _rope_tables(ss)
    cache_k = cache_na_k.reshape(bs, DEPTH, past, NA_W)
    cache_v = cache_na_v.reshape(bs, DEPTH, past, NA_W)
    cache_kr_pad = _rope_lanes(cache_mla_krope)

    yp = x_prompt.reshape(bp * sp, D_MODEL)
    ys = x_sample.reshape(bs * ss, D_MODEL)
    new_k, new_v, new_ckv, new_kr, new_sf, new_sb = [], [], [], [], [], []
    w_head, w_tail = _pack_w_in(w_in)
    hp = _prenorm(yp, mod[0], g_pre[0][None, :], bp * sp, 0)
    hs = _prenorm(ys, mod[0], g_pre[0][None, :], ss, 1)
    for l in range(DEPTH):
        wq, wqp = _pack_w_uq(w_mla_uq[l])
        wuk, wuv = _pack_w_ukv(w_mla_ukv[l])
        w_na, w_mla, w_hg, w_o = (w.astype(BF16) for w in (w_br_na[l], w_br_mla[l], w_br_hg[l], w_out[l]))
        g_q, g_kv = (g_mla_q[l] * (MLA_QK ** -0.5 * LOG2E))[None, :], g_mla_kv[l][None, :]
        gpost, ghg = g_post[l][None, :], g_hg_out[l][None, :]
        mod3 = mod[l]
        nxt = (mod[l + 1], g_pre[l + 1][None, :]) if l + 1 < DEPTH else (None, None)

        zp_head, zp_mid, zp_mg = _in_proj(hp, w_head, w_tail, l)
        o_na, o_mla, ckvn, k_new, v_new, kr_new = _prompt_attn(zp_head, bp, g_q, g_kv, wq, wuk, wuv)
        o_f, o_b, sf, sb = _hgrn(zp_mid, bp, lb_f_all[l], lb_b_all[l])
        yp, hp = _out_proj(yp, zp_head, zp_mid, zp_mg, mod3, gpost, ghg, o_na, o_mla, o_f, o_b,
                           w_na, w_mla, w_hg, w_o, bp * sp, 0, *nxt)
        new_k.append(k_new.reshape(bp, sp, NA_HEADS, NA_HD))
        new_v.append(v_new.reshape(bp, sp, NA_HEADS, NA_HD))
        new_ckv.append(ckvn.reshape(bp, sp, MLA_KV_LORA))
        new_kr.append(kr_new.reshape(bp, sp, MLA_ROPE))
        new_sf.append(sf)
        new_sb.append(sb)

        zs_head, zs_mid, zs_mg = _in_proj(hs, w_head, w_tail, l)
        o_na = _sample_na(zs_head, cache_k, cache_v, _na_bias_table(na_rpb[l]), l, bs)
        kq_all, vp_all = _sample_mla_kv(zs_head, cache_mla_ckv, cache_kr_pad, cos_t, sin_t, g_kv, wuk, wuv, l, bs)
        o_mla = _sample_mla_attn(zs_head, cos_t, sin_t, g_q, wq, wqp, kq_all, vp_all, bs)
        o_f, o_b, _, _ = _hgrn(zs_mid, bs, lb_f_all[l], lb_b_all[l], state_hgrn_fwd, state_hgrn_bwd, l)
        ys, hs = _out_proj(ys, zs_head, zs_mid, zs_mg, mod3, gpost, ghg, o_na, o_mla, o_f, o_b,
                           w_na, w_mla, w_hg, w_o, ss, 1, *nxt)

    return (yp.reshape(bp, sp, D_MODEL), ys.reshape(bs, ss, D_MODEL),
            jnp.stack(new_k, axis=1), jnp.stack(new_v, axis=1), jnp.stack(new_ckv, axis=1),
            jnp.stack(new_kr, axis=1), jnp.stack(new_sf, axis=1), jnp.stack(new_sb, axis=1))
```
